```python
import jax, jax.numpy as jnp
from jax import lax
import numpy as np

D_MODEL = 2048
BATCH = 4
SEQ = 8192
DEPTH = 1
DEC_BATCH = 32
DEC_SEQ = 32
PAST_LEN = 1024

CHUNK = 64
SB_HEAD_DIM = 128
SB_HEADS = D_MODEL // (2 * SB_HEAD_DIM)
SB_BLOCK = 128
GLA_HEADS = 4
GLA_DK = D_MODEL // (2 * GLA_HEADS)
GLA_DV = D_MODEL // GLA_HEADS
GLA_LOW_RANK = 16
GLA_GATE_NORM = 16.0
MEM_TOKENS = 256
MEM_HEADS = 4
MEM_HEAD_DIM = D_MODEL // (2 * MEM_HEADS)
D_FF = ((8 * D_MODEL // 3 + 255) // 256) * 256
N_BRANCH = 3
EPS = 1e-6

SB_W = SB_HEADS * SB_HEAD_DIM
GLA_KW = GLA_HEADS * GLA_DK
GLA_VW = GLA_HEADS * GLA_DV
MEM_W = MEM_HEADS * MEM_HEAD_DIM
IN_SPLIT = (SB_W, SB_W, SB_W, GLA_KW, GLA_KW, GLA_VW, GLA_VW, GLA_LOW_RANK, MEM_W)
IN_W = SB_W * 3 + GLA_KW * 2 + GLA_VW * 2 + GLA_LOW_RANK + MEM_W

kernel_name = 'streaming_sb_gla_mem_hybrid_step'


def rms_norm(x, g):
    x32 = x.astype(jnp.float32)
    y = x32 * lax.rsqrt(jnp.mean(x32 * x32, axis=-1, keepdims=True) + EPS)
    return (y * g.astype(jnp.float32)).astype(x.dtype)


def swiglu(x, w_gu, w_d):
    g, u = jnp.split(x @ w_gu, 2, axis=-1)
    return (jax.nn.silu(g) * u) @ w_d


def sb_attend(q, k, v, q_pos, k_pos):
    z = jnp.einsum('bqhd,bkhd->bhqk', q.astype(jnp.float32), k.astype(jnp.float32)) * SB_HEAD_DIM ** -0.5
    mask = k_pos[None, :] < q_pos[:, None]
    c = jnp.where(mask, jax.nn.log_sigmoid(-z), 0.0)
    log_a = z + lax.cumsum(c, axis=3, reverse=True)
    a = jnp.exp(jnp.where(mask, log_a, -jnp.inf))
    return jnp.einsum('bhqk,bkhd->bqhd', a, v.astype(jnp.float32))


def sb_prompt(q, k, v):
    B, T = q.shape[0], q.shape[1]
    nb = T // SB_BLOCK
    qb = q.reshape(B, nb, SB_BLOCK, SB_HEADS, SB_HEAD_DIM).swapaxes(0, 1)
    starts = jnp.arange(nb, dtype=jnp.int32) * SB_BLOCK
    k_pos = jnp.arange(T, dtype=jnp.int32)

    def one_block(args):
        qi, s0 = args
        return sb_attend(qi, k, v, s0 + jnp.arange(SB_BLOCK, dtype=jnp.int32), k_pos)

    o = lax.map(one_block, (qb, starts))
    return o.swapaxes(0, 1).reshape(B, T, SB_HEADS, SB_HEAD_DIM)


def gla_chunk(s, qc, kc, vc, gc):
    L = qc.shape[1]
    b = jnp.cumsum(gc, axis=1)
    o_inter = jnp.einsum('blhd,bhde->blhe', qc * jnp.exp(b), s)
    causal = jnp.tril(jnp.ones((L, L), dtype=bool))[None, :, :, None, None]
    diff = b[:, :, None] - b[:, None, :]
    decay = jnp.where(causal, jnp.exp(jnp.where(causal, diff, 0.0)), 0.0)
    att = jnp.einsum('bthd,bshd,btshd->bhts', qc, kc, decay)
    o_intra = jnp.einsum('bhts,bshe->bthe', att, vc)
    b_last = b[:, -1]
    s_new = jnp.exp(b_last)[..., None] * s + jnp.einsum('bshd,bshe->bhde', kc * jnp.exp(b_last[:, None] - b), vc)
    return s_new, o_inter + o_intra


def gla_prompt(q, k, v, g):
    B, T = q.shape[0], q.shape[1]
    n = T // CHUNK

    def to_chunks(a):
        return a.reshape(B, n, CHUNK, a.shape[2], a.shape[3]).swapaxes(0, 1)

    s0 = jnp.zeros((B, GLA_HEADS, GLA_DK, GLA_DV), jnp.float32)
    s_fin, o = lax.scan(lambda s, xs: gla_chunk(s, xs[0], xs[1], xs[2], xs[3]), s0,
                        (to_chunks(q), to_chunks(k), to_chunks(v), to_chunks(g)))
    return s_fin, o.swapaxes(0, 1).reshape(B, T, GLA_HEADS, GLA_DV)


def mem_attend(q, mk, mv):
    s = jnp.einsum('bthd,bmhd->bhtm', q.astype(jnp.float32), mk.astype(jnp.float32)) * MEM_HEAD_DIM ** -0.5
    p = jax.nn.softmax(s, axis=-1)
    return jnp.einsum('bhtm,bmhd->bthd', p, mv.astype(jnp.float32))


def mixer(u, p, mem_k, mem_v, sb_past_k, sb_past_v, gla_s0):
    B, T = u.shape[0], u.shape[1]
    dt = u.dtype
    cuts = [int(c) for c in np.cumsum(IN_SPLIT)[:-1]]
    sq, sk, sv, gq, gk, gv, gr, ga, mq = jnp.split(u @ p['w_in'], cuts, axis=-1)

    def heads(a, h):
        return a.reshape(B, T, h, a.shape[-1] // h)

    sq, sk, sv = heads(sq, SB_HEADS), heads(sk, SB_HEADS), heads(sv, SB_HEADS)
    if sb_past_k is None:
        o_sb = sb_prompt(sq, sk, sv)
    else:
        P = sb_past_k.shape[1]
        keys = jnp.concatenate([sb_past_k.astype(dt), sk], axis=1)
        vals = jnp.concatenate([sb_past_v.astype(dt), sv], axis=1)
        o_sb = sb_attend(sq, keys, vals, P + jnp.arange(T, dtype=jnp.int32), jnp.arange(P + T, dtype=jnp.int32))
    g_log = jax.nn.log_sigmoid((ga @ p['gla_w_a2'] + p['gla_b_a2']).astype(jnp.float32)) / GLA_GATE_NORM
    g_log = heads(g_log, GLA_HEADS)
    qg = heads(gq, GLA_HEADS).astype(jnp.float32) * GLA_DK ** -0.5
    kg = heads(gk, GLA_HEADS).astype(jnp.float32)
    vg = heads(gv, GLA_HEADS).astype(jnp.float32)
    if gla_s0 is None:
        s_new, o_gla = gla_prompt(qg, kg, vg, g_log)
    else:
        s_new, o_gla = gla_chunk(gla_s0.astype(jnp.float32), qg, kg, vg, g_log)
    o_gla = rms_norm(o_gla, p['gla_norm_g']) * jax.nn.silu(heads(gr, GLA_HEADS).astype(jnp.float32))
    o_mem = mem_attend(heads(mq, MEM_HEADS), mem_k, mem_v)
    br_sb = o_sb.reshape(B, T, SB_W).astype(dt) @ p['w_sb_br']
    br_gla = o_gla.reshape(B, T, GLA_VW).astype(dt) @ p['w_gla_br']
    br_mem = o_mem.reshape(B, T, MEM_W).astype(dt) @ p['w_mem_br']
    gates = jax.nn.sigmoid(u @ p['w_gate'] + p['b_gate']).reshape(B, T, N_BRANCH, D_MODEL)
    merged = gates[:, :, 0] * br_sb + gates[:, :, 1] * br_gla + gates[:, :, 2] * br_mem
    return merged @ p['w_out'], sk, sv, s_new.astype(dt)


def layer(x, p, mem_k, mem_v, sb_past_k, sb_past_v, gla_s0):
    h = x + 0.5 * rms_norm(swiglu(rms_norm(x, p['ffn1_pre_g']), p['ffn1_w_gu'], p['ffn1_w_d']), p['ffn1_post_g'])
    m, sk, sv, s_new = mixer(rms_norm(h, p['mix_pre_g']), p, mem_k, mem_v, sb_past_k, sb_past_v, gla_s0)
    h = h + rms_norm(m, p['mix_post_g'])
    h = h + 0.5 * rms_norm(swiglu(rms_norm(h, p['ffn2_pre_g']), p['ffn2_w_gu'], p['ffn2_w_d']), p['ffn2_post_g'])
    return h, sk, sv, s_new


def setup_inputs(seed: int = 0) -> dict:
    key = jax.random.key(seed)
    ks = jax.random.split(key, 32)
    f32 = jnp.float32

    def nrm(k, shape, scale):
        return jax.random.normal(k, shape, f32) * scale

    def gain(k, n):
        return 1.0 + 0.02 * jax.random.normal(k, (DEPTH, n), f32)

    return {
        'x_prompt': nrm(ks[0], (BATCH, SEQ, D_MODEL), 1.0),
        'x_sample': nrm(ks[1], (DEC_BATCH, DEC_SEQ, D_MODEL), 1.0),
        'mem_prompt': nrm(ks[2], (BATCH, MEM_TOKENS, D_MODEL), 1.0),
        'cache_sb_k': nrm(ks[3], (DEPTH, DEC_BATCH, PAST_LEN, SB_HEADS, SB_HEAD_DIM), 1.0),
        'cache_sb_v': nrm(ks[4], (DEPTH, DEC_BATCH, PAST_LEN, SB_HEADS, SB_HEAD_DIM), 1.0),
        'state_gla': nrm(ks[5], (DEPTH, DEC_BATCH, GLA_HEADS, GLA_DK, GLA_DV), 1.0),
        'cache_mem_k': nrm(ks[6], (DEPTH, DEC_BATCH, MEM_TOKENS, MEM_HEADS, MEM_HEAD_DIM), 1.0),
        'cache_mem_v': nrm(ks[7], (DEPTH, DEC_BATCH, MEM_TOKENS, MEM_HEADS, MEM_HEAD_DIM), 1.0),
        'ffn1_pre_g': gain(ks[8], D_MODEL),
        'ffn1_w_gu': nrm(ks[9], (DEPTH, D_MODEL, 2 * D_FF), D_MODEL ** -0.5),
        'ffn1_w_d': nrm(ks[10], (DEPTH, D_FF, D_MODEL), D_FF ** -0.5),
        'ffn1_post_g': gain(ks[11], D_MODEL),
        'mix_pre_g': gain(ks[12], D_MODEL),
        'w_in': nrm(ks[13], (DEPTH, D_MODEL, IN_W), D_MODEL ** -0.5),
        'gla_w_a2': nrm(ks[14], (DEPTH, GLA_LOW_RANK, GLA_KW), GLA_LOW_RANK ** -0.5),
        'gla_b_a2': nrm(ks[15], (DEPTH, GLA_KW), 0.1),
        'gla_norm_g': gain(ks[16], GLA_DV),
        'mem_norm_g': gain(ks[17], D_MODEL),
        'w_mem_kv': nrm(ks[18], (DEPTH, D_MODEL, 2 * MEM_W), D_MODEL ** -0.5),
        'w_sb_br': nrm(ks[19], (DEPTH, SB_W, D_MODEL), SB_W ** -0.5),
        'w_gla_br': nrm(ks[20], (DEPTH, GLA_VW, D_MODEL), GLA_VW ** -0.5),
        'w_mem_br': nrm(ks[21], (DEPTH, MEM_W, D_MODEL), MEM_W ** -0.5),
        'w_gate': nrm(ks[22], (DEPTH, D_MODEL, N_BRANCH * D_MODEL), D_MODEL ** -0.5),
        'b_gate': nrm(ks[23], (DEPTH, N_BRANCH * D_MODEL), 0.02),
        'w_out': nrm(ks[24], (DEPTH, D_MODEL, D_MODEL), D_MODEL ** -0.5),
        'mix_post_g': gain(ks[25], D_MODEL),
        'ffn2_pre_g': gain(ks[26], D_MODEL),
        'ffn2_w_gu': nrm(ks[27], (DEPTH, D_MODEL, 2 * D_FF), D_MODEL ** -0.5),
        'ffn2_w_d': nrm(ks[28], (DEPTH, D_FF, D_MODEL), D_FF ** -0.5),
        'ffn2_post_g': gain(ks[29], D_MODEL),
    }


def reference(x_prompt, x_sample, mem_prompt, cache_sb_k, cache_sb_v, state_gla, cache_mem_k, cache_mem_v,
              ffn1_pre_g, ffn1_w_gu, ffn1_w_d, ffn1_post_g, mix_pre_g, w_in, gla_w_a2, gla_b_a2, gla_norm_g,
              mem_norm_g, w_mem_kv, w_sb_br, w_gla_br, w_mem_br, w_gate, b_gate, w_out, mix_post_g,
              ffn2_pre_g, ffn2_w_gu, ffn2_w_d, ffn2_post_g):
    h_p, h_s = x_prompt, x_sample
    Bp = x_prompt.shape[0]
    sbk_p, sbv_p, gla_p, memk_p, memv_p = [], [], [], [], []
    sbk_s, sbv_s, gla_s = [], [], []
    for l in range(DEPTH):
        p = dict(ffn1_pre_g=ffn1_pre_g[l], ffn1_w_gu=ffn1_w_gu[l], ffn1_w_d=ffn1_w_d[l], ffn1_post_g=ffn1_post_g[l],
                 mix_pre_g=mix_pre_g[l], w_in=w_in[l], gla_w_a2=gla_w_a2[l], gla_b_a2=gla_b_a2[l],
                 gla_norm_g=gla_norm_g[l], w_sb_br=w_sb_br[l], w_gla_br=w_gla_br[l], w_mem_br=w_mem_br[l],
                 w_gate=w_gate[l], b_gate=b_gate[l], w_out=w_out[l], mix_post_g=mix_post_g[l],
                 ffn2_pre_g=ffn2_pre_g[l], ffn2_w_gu=ffn2_w_gu[l], ffn2_w_d=ffn2_w_d[l], ffn2_post_g=ffn2_post_g[l])
        mk, mv = jnp.split(rms_norm(mem_prompt, mem_norm_g[l]) @ w_mem_kv[l], 2, axis=-1)
        mk = mk.reshape(Bp, MEM_TOKENS, MEM_HEADS, MEM_HEAD_DIM)
        mv = mv.reshape(Bp, MEM_TOKENS, MEM_HEADS, MEM_HEAD_DIM)
        h_p, k_p, v_p, s_p = layer(h_p, p, mk, mv, None, None, None)
        h_s, k_s, v_s, s_s = layer(h_s, p, cache_mem_k[l], cache_mem_v[l], cache_sb_k[l], cache_sb_v[l], state_gla[l])
        sbk_p.append(k_p); sbv_p.append(v_p); gla_p.append(s_p); memk_p.append(mk); memv_p.append(mv)
        sbk_s.append(k_s); sbv_s.append(v_s); gla_s.append(s_s)
    return (h_p, h_s, jnp.stack(sbk_p), jnp.stack(sbv_p), jnp.stack(gla_p), jnp.stack(memk_p), jnp.stack(memv_p),
            jnp.stack(sbk_s), jnp.stack(sbv_s), jnp.stack(gla_s))
```

```python
import functools

import jax
import jax.numpy as jnp
from jax import lax
from jax.experimental import pallas as pl
from jax.experimental.pallas import tpu as pltpu

F32 = jnp.float32
BF16 = jnp.bfloat16

EPS = 1e-6
V7X_LANES = 128
V7X_MXU_DIM = 256
V7X_VMEM_BYTES = 64 * 1024 * 1024
VMEM_LIMIT_BYTES = V7X_VMEM_BYTES - 8 * 1024 * 1024

SB_HEAD_DIM = 128
GLA_HEADS = 4
GLA_LOW_RANK = 16
GLA_GATE_NORM = 16.0
MEM_HEADS = 4
N_BRANCH = 3
GLA_CHUNK = 64


def _cparams(semantics):
    return pltpu.CompilerParams(dimension_semantics=semantics, vmem_limit_bytes=VMEM_LIMIT_BYTES)


def _rms(x, g):
    ms = jnp.mean(x * x, axis=-1, keepdims=True)
    return x * lax.rsqrt(ms + EPS) * g


def _log_sigmoid(x):
    return jnp.minimum(x, 0.0) - jnp.log(1.0 + jnp.exp(-jnp.abs(x)))


def _split_hi_lo(x):
    hi = x.astype(BF16)
    lo = (x - hi.astype(F32)).astype(BF16)
    return hi, lo


def _pick_tile(n, target):
    t = min(n, target)
    while n % t:
        t //= 2
    return t


def _ffn_kernel(x_ref, pre_g_ref, wg_ref, wu_ref, wd_ref, post_g_ref, *refs, emit_next):
    if emit_next:
        next_g_ref, h_ref, u_ref, xn_sc, acc_sc = refs
    else:
        h_ref, xn_sc, acc_sc = refs
    j = pl.program_id(1)

    @pl.when(j == 0)
    def _():
        xn_sc[...] = _rms(x_ref[...], pre_g_ref[...]).astype(BF16)
        acc_sc[...] = jnp.zeros_like(acc_sc)

    xn = xn_sc[...]
    g = jnp.dot(xn, wg_ref[...], preferred_element_type=F32)
    u = jnp.dot(xn, wu_ref[...], preferred_element_type=F32)
    act = (g * jax.nn.sigmoid(g) * u).astype(BF16)
    acc_sc[...] += jnp.dot(act, wd_ref[...], preferred_element_type=F32)

    @pl.when(j == pl.num_programs(1) - 1)
    def _():
        h = x_ref[...] + 0.5 * _rms(acc_sc[...], post_g_ref[...])
        h_ref[...] = h
        if emit_next:
            u_ref[...] = _rms(h, next_g_ref[...]).astype(BF16)


def _ffn(x, pre_g, w_gu, w_d, post_g, next_g=None, *, tm=512, tf=512):
    emit_next = next_g is not None
    n, d = x.shape
    d_ff = w_d.shape[0]
    tm = _pick_tile(n, tm)
    assert d_ff % tf == 0
    nf = d_ff // tf
    row = lambda i, j: (i, 0)
    vec = lambda i, j: (0, 0)
    out_shape = [jax.ShapeDtypeStruct((n, d), F32)]
    out_specs = [pl.BlockSpec((tm, d), row)]
    in_specs = [
        pl.BlockSpec((tm, d), row),
        pl.BlockSpec((1, d), vec),
        pl.BlockSpec((d, tf), lambda i, j: (0, j)),
        pl.BlockSpec((d, tf), lambda i, j: (0, nf + j)),
        pl.BlockSpec((tf, d), lambda i, j: (j, 0)),
        pl.BlockSpec((1, d), vec),
    ]
    args = [x, pre_g, w_gu, w_gu, w_d, post_g]
    if emit_next:
        in_specs.append(pl.BlockSpec((1, d), vec))
        args.append(next_g)
        out_shape.append(jax.ShapeDtypeStruct((n, d), BF16))
        out_specs.append(pl.BlockSpec((tm, d), row))
    outs = pl.pallas_call(
        functools.partial(_ffn_kernel, emit_next=emit_next),
        grid=(n // tm, nf),
        in_specs=in_specs,
        out_specs=out_specs,
        out_shape=out_shape,
        scratch_shapes=[pltpu.VMEM((tm, d), BF16), pltpu.VMEM((tm, d), F32)],
        compiler_params=_cparams(("parallel", "arbitrary")),
        name="ffn",
    )(*args)
    return outs if emit_next else outs[0]


def _proj_kernel(*refs, n_out, has_bias, has_norm, act):
    it = iter(refs)
    x_ref = next(it)
    g_ref = next(it) if has_norm else None
    w_refs = [next(it) for _ in range(n_out)]
    b_refs = [next(it) for _ in range(n_out)] if has_bias else None
    o_refs = [next(it) for _ in range(n_out)]
    if has_norm:
        xn_sc = next(it)

        @pl.when(pl.program_id(1) == 0)
        def _():
            xn_sc[...] = _rms(x_ref[...], g_ref[...]).astype(BF16)

        x = xn_sc[...]
    else:
        x = x_ref[...]
    for k in range(n_out):
        acc = jnp.dot(x, w_refs[k][...], preferred_element_type=F32)
        if has_bias:
            acc = acc + b_refs[k][...]
        if act == "sigmoid":
            acc = jax.nn.sigmoid(acc)
        o_refs[k][...] = acc.astype(o_refs[k].dtype)


def _proj(x, ws, out_dtypes, *, biases=None, norm_g=None, act=None, tm=1024, tn=512):
    n, kdim = x.shape
    ncols = ws[0].shape[1]
    tm = _pick_tile(n, tm)
    tn = _pick_tile(ncols, tn)
    n_out = len(ws)
    in_specs = [pl.BlockSpec((tm, kdim), lambda i, j: (i, 0))]
    args = [x]
    if norm_g is not None:
        in_specs.append(pl.BlockSpec((1, kdim), lambda i, j: (0, 0)))
        args.append(norm_g)
    in_specs += [pl.BlockSpec((kdim, tn), lambda i, j: (0, j))] * n_out
    args += list(ws)
    if biases is not None:
        in_specs += [pl.BlockSpec((1, tn), lambda i, j: (0, j))] * n_out
        args += list(biases)
    scratch = [pltpu.VMEM((tm, kdim), BF16)] if norm_g is not None else []
    return pl.pallas_call(
        functools.partial(_proj_kernel, n_out=n_out, has_bias=biases is not None,
                          has_norm=norm_g is not None, act=act),
        grid=(n // tm, ncols // tn),
        in_specs=in_specs,
        out_specs=[pl.BlockSpec((tm, tn), lambda i, j: (i, j))] * n_out,
        out_shape=[jax.ShapeDtypeStruct((n, ncols), dt) for dt in out_dtypes],
        scratch_shapes=scratch,
        compiler_params=_cparams(("parallel", "arbitrary")),
        name="proj",
    )(*args)


def _suffix_ones(n):
    r = lax.broadcasted_iota(jnp.int32, (n, n), 0)
    c = lax.broadcasted_iota(jnp.int32, (n, n), 1)
    return jnp.where(r >= c, 1.0, 0.0).astype(BF16)


def _sb_tile(q, k, v, u_mat, rsum, acc, *, scale, masked):
    z = lax.dot_general(q, k, (((1,), (1,)), ((), ())), preferred_element_type=F32) * scale
    c = _log_sigmoid(-z)
    if masked:
        rows = lax.broadcasted_iota(jnp.int32, z.shape, 0)
        cols = lax.broadcasted_iota(jnp.int32, z.shape, 1)
        mask = cols < rows
        c = jnp.where(mask, c, 0.0)
    c_hi, c_lo = _split_hi_lo(c)
    cs = (jnp.dot(c_hi, u_mat, preferred_element_type=F32)
          + jnp.dot(c_lo, u_mat, preferred_element_type=F32))
    a = jnp.exp(z + cs + rsum)
    if masked:
        a = jnp.where(mask, a, 0.0)
    acc = acc + jnp.dot(a.astype(BF16), v, preferred_element_type=F32)
    return rsum + cs[:, 0:1], acc


def _sb_kernel(*refs, bq, bk_past, n_past, scale):
    if n_past:
        q_ref, kn_ref, vn_ref, kp_ref, vp_ref, o_ref, kb_sc, vb_sc, kpb_sc, vpb_sc = refs
    else:
        q_ref, kn_ref, vn_ref, o_ref, kb_sc, vb_sc = refs
    qi = pl.program_id(2)

    @pl.when(qi == 0)
    def _():
        kb_sc[...] = kn_ref[0].astype(BF16)
        vb_sc[...] = vn_ref[0].astype(BF16)
        if n_past:
            kpb_sc[...] = kp_ref[0].astype(BF16)
            vpb_sc[...] = vp_ref[0].astype(BF16)

    q = q_ref[0]
    d = q.shape[-1]
    u_new = _suffix_ones(bq)
    tile = functools.partial(_sb_tile, scale=scale)

    start = pl.multiple_of(qi * bq, bq)
    rsum, acc = tile(q, kb_sc[pl.ds(start, bq), :], vb_sc[pl.ds(start, bq), :], u_new,
                     jnp.zeros((bq, 1), F32), jnp.zeros((bq, d), F32), masked=True)

    def new_body(i, carry):
        s = pl.multiple_of((qi - 1 - i) * bq, bq)
        return tile(q, kb_sc[pl.ds(s, bq), :], vb_sc[pl.ds(s, bq), :], u_new, *carry, masked=False)

    rsum, acc = lax.fori_loop(0, qi, new_body, (rsum, acc))

    if n_past:
        u_past = u_new if bk_past == bq else _suffix_ones(bk_past)
        for i in range(n_past - 1, -1, -1):
            sl = slice(i * bk_past, (i + 1) * bk_past)
            rsum, acc = tile(q, kpb_sc[sl, :], vpb_sc[sl, :], u_past, rsum, acc, masked=False)

    o_ref[0] = acc.astype(o_ref.dtype)


def _sb_attn(q, k_new, v_new, k_past, v_past, *, heads, bq=256, bk_past=256):
    b, t, hd_all = q.shape
    d = hd_all // heads
    bq = _pick_tile(t, bq)
    has_past = k_past is not None
    p = k_past.shape[1] if has_past else 0
    if has_past:
        bk_past = _pick_tile(p, bk_past)
    n_past = p // bk_past if has_past else 0
    seq = lambda bi, hi, qi: (bi, 0, hi)
    in_specs = [pl.BlockSpec((1, bq, d), lambda bi, hi, qi: (bi, qi, hi)),
                pl.BlockSpec((1, t, d), seq), pl.BlockSpec((1, t, d), seq)]
    args = [q, k_new, v_new]
    scratch = [pltpu.VMEM((t, d), BF16), pltpu.VMEM((t, d), BF16)]
    if has_past:
        in_specs += [pl.BlockSpec((1, p, d), seq), pl.BlockSpec((1, p, d), seq)]
        args += [k_past, v_past]
        scratch += [pltpu.VMEM((p, d), BF16), pltpu.VMEM((p, d), BF16)]
    return pl.pallas_call(
        functools.partial(_sb_kernel, bq=bq, bk_past=bk_past, n_past=n_past, scale=d ** -0.5),
        grid=(b, heads, t // bq),
        in_specs=in_specs,
        out_specs=pl.BlockSpec((1, bq, d), lambda bi, hi, qi: (bi, qi, hi)),
        out_shape=jax.ShapeDtypeStruct((b, t, hd_all), BF16),
        scratch_shapes=scratch,
        compiler_params=_cparams(("parallel", "parallel", "arbitrary")),
        name="sb_attn",
    )(*args)


def _gla_kernel(*refs, chunk, n_chunks, has_s0, dk_scale):
    if has_s0:
        q_ref, k_ref, v_ref, r_ref, ga_ref, wa_ref, ba_ref, ng_ref, s0_ref, o_ref, s_out_ref, s_sc = refs
    else:
        q_ref, k_ref, v_ref, r_ref, ga_ref, wa_ref, ba_ref, ng_ref, o_ref, s_out_ref, s_sc = refs
    tg = pl.program_id(2)

    @pl.when(tg == 0)
    def _():
        s_sc[...] = s0_ref[0, 0] if has_s0 else jnp.zeros_like(s_sc)

    rows = lax.broadcasted_iota(jnp.int32, (chunk, chunk), 0)
    cols = lax.broadcasted_iota(jnp.int32, (chunk, chunk), 1)
    causal = cols <= rows
    tri = jnp.where(causal, 1.0, 0.0).astype(BF16)
    ones_cols = jnp.ones((chunk, V7X_LANES), BF16)
    dv = s_sc.shape[1]
    mid = chunk // 2 - 1

    def body(c, carry):
        sl = pl.ds(pl.multiple_of(c * chunk, chunk), chunk)
        x = jnp.dot(ga_ref[0, sl, :], wa_ref[0], preferred_element_type=F32) + ba_ref[0]
        g = _log_sigmoid(x) / GLA_GATE_NORM
        g_hi, g_lo = _split_hi_lo(g)
        b = jnp.dot(tri, g_hi, preferred_element_type=F32) + jnp.dot(tri, g_lo, preferred_element_type=F32)
        b_last = b[chunk - 1:chunk, :]
        b_mid = b[mid:mid + 1, :]
        tn_dims = (((0,), (0,)), ((), ()))
        b_last_col = (lax.dot_general(g_hi, ones_cols, tn_dims, preferred_element_type=F32)
                      + lax.dot_general(g_lo, ones_cols, tn_dims, preferred_element_type=F32))
        decay = jnp.concatenate([jnp.exp(b_last_col)] * (dv // V7X_LANES), axis=1)

        q = q_ref[0, sl, :].astype(F32) * dk_scale
        k = k_ref[0, sl, :].astype(F32)
        v = v_ref[0, sl, :]
        s_prev = s_sc[...]
        o_inter = jnp.dot((q * jnp.exp(b)).astype(BF16), s_prev.astype(BF16), preferred_element_type=F32)
        q_m = (q * jnp.exp(b - b_mid)).astype(BF16)
        k_m = (k * jnp.exp(b_mid - b)).astype(BF16)
        att = lax.dot_general(q_m, k_m, (((1,), (1,)), ((), ())), preferred_element_type=F32)
        att = jnp.where(causal, att, 0.0)
        o = o_inter + jnp.dot(att.astype(BF16), v, preferred_element_type=F32)
        k_st = (k * jnp.exp(b_last - b)).astype(BF16)
        s_sc[...] = decay * s_prev + lax.dot_general(k_st, v, tn_dims, preferred_element_type=F32)

        r = r_ref[0, sl, :].astype(F32)
        o_ref[0, sl, :] = (_rms(o, ng_ref[...]) * (r * jax.nn.sigmoid(r))).astype(o_ref.dtype)
        return carry

    lax.fori_loop(0, n_chunks, body, 0)

    @pl.when(tg == pl.num_programs(2) - 1)
    def _():
        s_out_ref[0, 0] = s_sc[...]


def _gla(q, k, v, r, ga, wa, ba, ng, s0, *, chunk, tg=512):
    b, t, _ = q.shape
    heads, _, dk = wa.shape
    dv = v.shape[-1] // heads
    chunk = min(chunk, t)
    tg = _pick_tile(t, tg)
    assert tg % chunk == 0
    has_s0 = s0 is not None
    tok = lambda bi, hi, ti: (bi, ti, hi)
    in_specs = [
        pl.BlockSpec((1, tg, dk), tok), pl.BlockSpec((1, tg, dk), tok),
        pl.BlockSpec((1, tg, dv), tok), pl.BlockSpec((1, tg, dv), tok),
        pl.BlockSpec((1, tg, V7X_LANES), lambda bi, hi, ti: (bi, ti, 0)),
        pl.BlockSpec((1, V7X_LANES, dk), lambda bi, hi, ti: (hi, 0, 0)),
        pl.BlockSpec((1, 1, dk), lambda bi, hi, ti: (hi, 0, 0)),
        pl.BlockSpec((1, dv), lambda bi, hi, ti: (0, 0)),
    ]
    args = [q, k, v, r, ga, wa, ba, ng]
    state_spec = pl.BlockSpec((1, 1, dk, dv), lambda bi, hi, ti: (bi, hi, 0, 0))
    if has_s0:
        in_specs.append(state_spec)
        args.append(s0)
    return pl.pallas_call(
        functools.partial(_gla_kernel, chunk=chunk, n_chunks=tg // chunk, has_s0=has_s0, dk_scale=dk ** -0.5),
        grid=(b, heads, t // tg),
        in_specs=in_specs,
        out_specs=[pl.BlockSpec((1, tg, dv), tok), state_spec],
        out_shape=[jax.ShapeDtypeStruct((b, t, heads * dv), BF16),
                   jax.ShapeDtypeStruct((b, heads, dk, dv), F32)],
        scratch_shapes=[pltpu.VMEM((dk, dv), F32)],
        compiler_params=_cparams(("parallel", "parallel", "arbitrary")),
        name="gla",
    )(*args)


def _mem_kernel(q_ref, k_ref, v_ref, o_ref, *, heads, scale):
    hd = q_ref.shape[-1] // heads
    for h in range(heads):
        cs = slice(h * hd, (h + 1) * hd)
        q = q_ref[0, :, cs]
        k = k_ref[0, :, cs].astype(BF16)
        v = v_ref[0, :, cs].astype(BF16)
        s = lax.dot_general(q, k, (((1,), (1,)), ((), ())), preferred_element_type=F32) * scale
        e = jnp.exp(s - jnp.max(s, axis=-1, keepdims=True))
        p = e / jnp.sum(e, axis=-1, keepdims=True)
        o_ref[0, :, cs] = jnp.dot(p.astype(BF16), v, preferred_element_type=F32).astype(o_ref.dtype)


def _mem_attn(q, mk, mv, *, heads, tq=1024):
    b, t, w = q.shape
    m = mk.shape[1]
    tq = _pick_tile(t, tq)
    mem = lambda bi, ti: (bi, 0, 0)
    return pl.pallas_call(
        functools.partial(_mem_kernel, heads=heads, scale=(w // heads) ** -0.5),
        grid=(b, t // tq),
        in_specs=[pl.BlockSpec((1, tq, w), lambda bi, ti: (bi, ti, 0)),
                  pl.BlockSpec((1, m, w), mem), pl.BlockSpec((1, m, w), mem)],
        out_specs=pl.BlockSpec((1, tq, w), lambda bi, ti: (bi, ti, 0)),
        out_shape=jax.ShapeDtypeStruct((b, t, w), BF16),
        compiler_params=_cparams(("parallel", "arbitrary")),
        name="mem_attn",
    )(q, mk, mv)


def _merge_kernel(h_ref, osb_ref, ogla_ref, omem_ref, g0_ref, g1_ref, g2_ref, wsb_ref, wgla_ref, wmem_ref,
                  wout_ref, post_g_ref, h2_ref, acc_sc):
    j = pl.program_id(1)

    @pl.when(j == 0)
    def _():
        acc_sc[...] = jnp.zeros_like(acc_sc)

    br_sb = jnp.dot(osb_ref[...], wsb_ref[...], preferred_element_type=F32)
    br_gla = jnp.dot(ogla_ref[...], wgla_ref[...], preferred_element_type=F32)
    br_mem = jnp.dot(omem_ref[...], wmem_ref[...], preferred_element_type=F32)
    merged = (g0_ref[...].astype(F32) * br_sb + g1_ref[...].astype(F32) * br_gla
              + g2_ref[...].astype(F32) * br_mem)
    acc_sc[...] += jnp.dot(merged.astype(BF16), wout_ref[...], preferred_element_type=F32)

    @pl.when(j == pl.num_programs(1) - 1)
    def _():
        h2_ref[...] = h_ref[...] + _rms(acc_sc[...], post_g_ref[...])


def _merge(h, o_sb, o_gla, o_mem, gates, w_sb, w_gla, w_mem, w_out, post_g, *, tm=512, tn=256):
    n, d = h.shape
    tm = _pick_tile(n, tm)
    nj = d // tn
    row = lambda i, j: (i, 0)
    col = lambda i, j: (0, j)
    vec = lambda i, j: (0, 0)
    return pl.pallas_call(
        _merge_kernel,
        grid=(n // tm, nj),
        in_specs=[
            pl.BlockSpec((tm, d), row),
            pl.BlockSpec((tm, o_sb.shape[1]), row),
            pl.BlockSpec((tm, o_gla.shape[1]), row),
            pl.BlockSpec((tm, o_mem.shape[1]), row),
            pl.BlockSpec((tm, tn), lambda i, j: (i, j)),
            pl.BlockSpec((tm, tn), lambda i, j: (i, nj + j)),
            pl.BlockSpec((tm, tn), lambda i, j: (i, 2 * nj + j)),
            pl.BlockSpec((w_sb.shape[0], tn), col),
            pl.BlockSpec((w_gla.shape[0], tn), col),
            pl.BlockSpec((w_mem.shape[0], tn), col),
            pl.BlockSpec((tn, d), lambda i, j: (j, 0)),
            pl.BlockSpec((1, d), vec),
        ],
        out_specs=pl.BlockSpec((tm, d), row),
        out_shape=jax.ShapeDtypeStruct((n, d), F32),
        scratch_shapes=[pltpu.VMEM((tm, d), F32)],
        compiler_params=_cparams(("parallel", "arbitrary")),
        name="merge",
    )(h, o_sb, o_gla, o_mem, gates, gates, gates, w_sb, w_gla, w_mem, w_out, post_g)


def _prep_weights(p, d):
    sb_w = d // 2
    gla_kw = d // 2
    gla_vw = d
    mem_w = d // 2
    w_in = p["w_in"]
    c = 0
    pieces = {}
    for name, width in (("sq", sb_w), ("sk", sb_w), ("sv", sb_w), ("gq", gla_kw), ("gk", gla_kw),
                        ("gv", gla_vw), ("gr", gla_vw), ("ga", GLA_LOW_RANK), ("mq", mem_w)):
        pieces[name] = w_in[:, c:c + width].astype(BF16)
        c += width
    assert c == w_in.shape[1]
    pad = V7X_LANES - GLA_LOW_RANK
    dk = gla_kw // GLA_HEADS
    w = dict(pieces)
    w["ga"] = jnp.pad(pieces["ga"], ((0, 0), (0, pad)))
    wa = jnp.pad(p["gla_w_a2"].astype(BF16), ((0, pad), (0, 0)))
    w["wa"] = wa.reshape(V7X_LANES, GLA_HEADS, dk).transpose(1, 0, 2)
    w["ba"] = p["gla_b_a2"].reshape(GLA_HEADS, 1, dk)
    for name in ("ffn1_w_gu", "ffn1_w_d", "ffn2_w_gu", "ffn2_w_d", "w_sb_br", "w_gla_br", "w_mem_br",
                 "w_gate", "w_out"):
        w[name] = p[name].astype(BF16)
    for name in ("ffn1_pre_g", "ffn1_post_g", "mix_pre_g", "mix_post_g", "ffn2_pre_g", "ffn2_post_g",
                 "gla_norm_g", "b_gate"):
        w[name] = p[name].reshape(1, -1)
    return w


def _layer(x, w, mem_k, mem_v, sb_past_k, sb_past_v, gla_s0):
    b, t, d = x.shape
    n = b * t
    sb_heads = (d // 2) // SB_HEAD_DIM
    h1, u = _ffn(x.reshape(n, d), w["ffn1_pre_g"], w["ffn1_w_gu"], w["ffn1_w_d"], w["ffn1_post_g"],
                 next_g=w["mix_pre_g"])
    sq, sk, sv, gq, gk, mq = _proj(u, [w[k] for k in ("sq", "sk", "sv", "gq", "gk", "mq")],
                                   [BF16, F32, F32, BF16, BF16, BF16])
    gv, gr = _proj(u, [w["gv"], w["gr"]], [BF16, BF16])
    (ga,) = _proj(u, [w["ga"]], [BF16])
    (gates,) = _proj(u, [w["w_gate"]], [BF16], biases=[w["b_gate"]], act="sigmoid")

    r3 = lambda a: a.reshape(b, t, -1)
    o_sb = _sb_attn(r3(sq), r3(sk), r3(sv), sb_past_k, sb_past_v, heads=sb_heads)
    o_gla, s_new = _gla(r3(gq), r3(gk), r3(gv), r3(gr), r3(ga), w["wa"], w["ba"], w["gla_norm_g"], gla_s0,
                        chunk=GLA_CHUNK)
    o_mem = _mem_attn(r3(mq), mem_k, mem_v, heads=MEM_HEADS)

    h2 = _merge(h1, o_sb.reshape(n, -1), o_gla.reshape(n, -1), o_mem.reshape(n, -1), gates,
                w["w_sb_br"], w["w_gla_br"], w["w_mem_br"], w["w_out"], w["mix_post_g"])
    y = _ffn(h2, w["ffn2_pre_g"], w["ffn2_w_gu"], w["ffn2_w_d"], w["ffn2_post_g"])
    return y.reshape(b, t, d), sk, sv, s_new


def kernel(x_prompt, x_sample, mem_prompt, cache_sb_k, cache_sb_v, state_gla, cache_mem_k, cache_mem_v, ffn1_pre_g, ffn1_w_gu, ffn1_w_d, ffn1_post_g, mix_pre_g, w_in, gla_w_a2, gla_b_a2, gla_norm_g, mem_norm_g, w_mem_kv, w_sb_br, w_gla_br, w_mem_br, w_gate, b_gate, w_out, mix_post_g, ffn2_pre_g, ffn2_w_gu, ffn2_w_d, ffn2_post_g):
    params = dict(ffn1_pre_g=ffn1_pre_g, ffn1_w_gu=ffn1_w_gu, ffn1_w_d=ffn1_w_d, ffn1_post_g=ffn1_post_g,
                  mix_pre_g=mix_pre_g, w_in=w_in, gla_w_a2=gla_w_a2, gla_b_a2=gla_b_a2, gla_norm_g=gla_norm_g,
                  w_sb_br=w_sb_br, w_gla_br=w_gla_br, w_mem_br=w_mem_br, w_gate=w_gate, b_gate=b_gate,
                  w_out=w_out, mix_post_g=mix_post_g, ffn2_pre_g=ffn2_pre_g, ffn2_w_gu=ffn2_w_gu,
                  ffn2_w_d=ffn2_w_d, ffn2_post_g=ffn2_post_g)
    depth = w_in.shape[0]
    bp, tp, d = x_prompt.shape
    bs, ts, _ = x_sample.shape
    m = mem_prompt.shape[1]
    mem_w = d // 2
    h_p, h_s = x_prompt, x_sample
    outs = [[] for _ in range(8)]
    for l in range(depth):
        w = _prep_weights({k: v[l] for k, v in params.items()}, d)
        w_mkv = w_mem_kv[l].astype(BF16)
        mk, mv = _proj(mem_prompt.reshape(bp * m, d), [w_mkv[:, :mem_w], w_mkv[:, mem_w:]], [F32, F32],
                       norm_g=mem_norm_g[l].reshape(1, d))
        mk = mk.reshape(bp, m, mem_w)
        mv = mv.reshape(bp, m, mem_w)
        h_p, k_p, v_p, s_p = _layer(h_p, w, mk, mv, None, None, None)
        past = cache_sb_k.shape[2]
        h_s, k_s, v_s, s_s = _layer(h_s, w, cache_mem_k[l].reshape(bs, m, mem_w), cache_mem_v[l].reshape(bs, m, mem_w),
                                    cache_sb_k[l].reshape(bs, past, -1), cache_sb_v[l].reshape(bs, past, -1),
                                    state_gla[l])
        sb_heads = mem_w // SB_HEAD_DIM
        for lst, val in zip(outs, (k_p.reshape(bp, tp, sb_heads, SB_HEAD_DIM), v_p.reshape(bp, tp, sb_heads, SB_HEAD_DIM),
                                   s_p, mk.reshape(bp, m, MEM_HEADS, -1), mv.reshape(bp, m, MEM_HEADS, -1),
                                   k_s.reshape(bs, ts, sb_heads, SB_HEAD_DIM), v_s.reshape(bs, ts, sb_heads, SB_HEAD_DIM),
                                   s_s)):
            lst.append(val)
    return (h_p, h_s) + tuple(jnp.stack(o) for o in outs)
```

```python
import functools

import jax
import jax.numpy as jnp
from jax import lax
from jax.experimental import pallas as pl
from jax.experimental.pallas import tpu as pltpu

F32 = jnp.float32
BF16 = jnp.bfloat16

EPS = 1e-6
V7X_LANES = 128
V7X_MXU_DIM = 256
V7X_VMEM_BYTES = 64 * 1024 * 1024
VMEM_LIMIT_BYTES = V7X_VMEM_BYTES - 8 * 1024 * 1024

SB_HEAD_DIM = 128
GLA_HEADS = 4
GLA_LOW_RANK = 16
GLA_GATE_NORM = 16.0
MEM_HEADS = 4
N_BRANCH = 3
GLA_CHUNK = 64
SB_LOG_WEIGHT_CUTOFF = -110.0


def _cparams(semantics):
    return pltpu.CompilerParams(dimension_semantics=semantics, vmem_limit_bytes=VMEM_LIMIT_BYTES)


def _rms(x, g):
    ms = jnp.mean(x * x, axis=-1, keepdims=True)
    return x * lax.rsqrt(ms + EPS) * g


def _log_sigmoid(x):
    return jnp.minimum(x, 0.0) - jnp.log(1.0 + jnp.exp(-jnp.abs(x)))


def _split_hi_lo(x):
    hi = x.astype(BF16)
    lo = (x - hi.astype(F32)).astype(BF16)
    return hi, lo


def _pick_tile(n, target):
    t = min(n, target)
    while n % t:
        t //= 2
    return t


def _ffn_kernel(x_ref, pre_g_ref, wg_ref, wu_ref, wd_ref, post_g_ref, *refs, emit_next):
    if emit_next:
        next_g_ref, h_ref, u_ref, xn_sc, acc_sc = refs
    else:
        h_ref, xn_sc, acc_sc = refs
    j = pl.program_id(1)

    @pl.when(j == 0)
    def _():
        xn_sc[...] = _rms(x_ref[...], pre_g_ref[...]).astype(BF16)
        acc_sc[...] = jnp.zeros_like(acc_sc)

    xn = xn_sc[...]
    g = jnp.dot(xn, wg_ref[...], preferred_element_type=F32)
    u = jnp.dot(xn, wu_ref[...], preferred_element_type=F32)
    act = (g * jax.nn.sigmoid(g) * u).astype(BF16)
    acc_sc[...] += jnp.dot(act, wd_ref[...], preferred_element_type=F32)

    @pl.when(j == pl.num_programs(1) - 1)
    def _():
        h = x_ref[...] + 0.5 * _rms(acc_sc[...], post_g_ref[...])
        h_ref[...] = h
        if emit_next:
            u_ref[...] = _rms(h, next_g_ref[...]).astype(BF16)


def _ffn(x, pre_g, w_gu, w_d, post_g, next_g=None, *, tm=512, tf=512):
    emit_next = next_g is not None
    n, d = x.shape
    d_ff = w_d.shape[0]
    tm = _pick_tile(n, tm)
    assert d_ff % tf == 0
    nf = d_ff // tf
    row = lambda i, j: (i, 0)
    vec = lambda i, j: (0, 0)
    out_shape = [jax.ShapeDtypeStruct((n, d), F32)]
    out_specs = [pl.BlockSpec((tm, d), row)]
    in_specs = [
        pl.BlockSpec((tm, d), row),
        pl.BlockSpec((1, d), vec),
        pl.BlockSpec((d, tf), lambda i, j: (0, j)),
        pl.BlockSpec((d, tf), lambda i, j: (0, nf + j)),
        pl.BlockSpec((tf, d), lambda i, j: (j, 0)),
        pl.BlockSpec((1, d), vec),
    ]
    args = [x, pre_g, w_gu, w_gu, w_d, post_g]
    if emit_next:
        in_specs.append(pl.BlockSpec((1, d), vec))
        args.append(next_g)
        out_shape.append(jax.ShapeDtypeStruct((n, d), BF16))
        out_specs.append(pl.BlockSpec((tm, d), row))
    outs = pl.pallas_call(
        functools.partial(_ffn_kernel, emit_next=emit_next),
        grid=(n // tm, nf),
        in_specs=in_specs,
        out_specs=out_specs,
        out_shape=out_shape,
        scratch_shapes=[pltpu.VMEM((tm, d), BF16), pltpu.VMEM((tm, d), F32)],
        compiler_params=_cparams(("parallel", "arbitrary")),
        name="ffn",
    )(*args)
    return outs if emit_next else outs[0]


def _proj_kernel(*refs, n_out, has_bias, has_norm, act):
    it = iter(refs)
    x_ref = next(it)
    g_ref = next(it) if has_norm else None
    w_refs = [next(it) for _ in range(n_out)]
    b_refs = [next(it) for _ in range(n_out)] if has_bias else None
    o_refs = [next(it) for _ in range(n_out)]
    if has_norm:
        xn_sc = next(it)

        @pl.when(pl.program_id(1) == 0)
        def _():
            xn_sc[...] = _rms(x_ref[...], g_ref[...]).astype(BF16)

        x = xn_sc[...]
    else:
        x = x_ref[...]
    for k in range(n_out):
        acc = jnp.dot(x, w_refs[k][...], preferred_element_type=F32)
        if has_bias:
            acc = acc + b_refs[k][...]
        if act == "sigmoid":
            acc = jax.nn.sigmoid(acc)
        o_refs[k][...] = acc.astype(o_refs[k].dtype)


def _proj(x, ws, out_dtypes, *, biases=None, norm_g=None, act=None, tm=1024, tn=512):
    n, kdim = x.shape
    ncols = ws[0].shape[1]
    tm = _pick_tile(n, tm)
    tn = _pick_tile(ncols, tn)
    n_out = len(ws)
    in_specs = [pl.BlockSpec((tm, kdim), lambda i, j: (i, 0))]
    args = [x]
    if norm_g is not None:
        in_specs.append(pl.BlockSpec((1, kdim), lambda i, j: (0, 0)))
        args.append(norm_g)
    in_specs += [pl.BlockSpec((kdim, tn), lambda i, j: (0, j))] * n_out
    args += list(ws)
    if biases is not None:
        in_specs += [pl.BlockSpec((1, tn), lambda i, j: (0, j))] * n_out
        args += list(biases)
    scratch = [pltpu.VMEM((tm, kdim), BF16)] if norm_g is not None else []
    return pl.pallas_call(
        functools.partial(_proj_kernel, n_out=n_out, has_bias=biases is not None,
                          has_norm=norm_g is not None, act=act),
        grid=(n // tm, ncols // tn),
        in_specs=in_specs,
        out_specs=[pl.BlockSpec((tm, tn), lambda i, j: (i, j))] * n_out,
        out_shape=[jax.ShapeDtypeStruct((n, ncols), dt) for dt in out_dtypes],
        scratch_shapes=scratch,
        compiler_params=_cparams(("parallel", "arbitrary")),
        name="proj",
    )(*args)


def _suffix_ones(n):
    r = lax.broadcasted_iota(jnp.int32, (n, n), 0)
    c = lax.broadcasted_iota(jnp.int32, (n, n), 1)
    return jnp.where(r >= c, 1.0, 0.0).astype(BF16)


def _sb_tiles(qs, ks, vs, u_mat, rsums, *, scale, masked):
    n = len(qs)
    nt_dims = (((1,), (1,)), ((), ()))
    zs = [lax.dot_general(qs[g], ks[g], nt_dims, preferred_element_type=F32) * scale for g in range(n)]
    cs_in = [_log_sigmoid(-z) for z in zs]
    if masked:
        rows = lax.broadcasted_iota(jnp.int32, zs[0].shape, 0)
        cols = lax.broadcasted_iota(jnp.int32, zs[0].shape, 1)
        mask = cols < rows
        cs_in = [jnp.where(mask, c, 0.0) for c in cs_in]
    splits = [_split_hi_lo(c) for c in cs_in]
    css = [jnp.dot(hi, u_mat, preferred_element_type=F32) + jnp.dot(lo, u_mat, preferred_element_type=F32)
           for hi, lo in splits]
    weights = [jnp.exp(zs[g] + css[g] + rsums[g]) for g in range(n)]
    if masked:
        weights = [jnp.where(mask, a, 0.0) for a in weights]
    contribs = [jnp.dot(weights[g].astype(BF16), vs[g], preferred_element_type=F32) for g in range(n)]
    return [rsums[g] + css[g][:, 0:1] for g in range(n)], contribs


def _sb_kernel(*refs, rg, groups, t_new, bk_past, n_past, scale):
    if n_past:
        (q_ref, kn_ref, vn_ref, kp_ref, vp_ref, o_ref,
         kb_sc, vb_sc, kmax_sc, zb_sc, r_sc, acc_sc, kpb_sc, vpb_sc) = refs
    else:
        q_ref, kn_ref, vn_ref, o_ref, kb_sc, vb_sc, kmax_sc, zb_sc, r_sc, acc_sc = refs
    qi = pl.program_id(2)
    sweep_left = t_new > rg * groups

    @pl.when(qi == 0)
    def _():
        kb_sc[0:rg, :] = jnp.zeros((rg, kb_sc.shape[1]), BF16)
        vb_sc[0:rg, :] = jnp.zeros((rg, vb_sc.shape[1]), BF16)
        ch = min(t_new, 512)

        def conv(i, kmax2):
            src = pl.ds(pl.multiple_of(i * ch, ch), ch)
            dst = pl.ds(pl.multiple_of(rg + i * ch, rg), ch)
            kb = kn_ref[0, src, :].astype(BF16)
            kb_sc[dst, :] = kb
            vb_sc[dst, :] = vn_ref[0, src, :].astype(BF16)
            kf = kb.astype(F32)
            n2 = jnp.sum(kf * kf, axis=1, keepdims=True)
            return jnp.maximum(kmax2, jnp.max(n2, axis=0, keepdims=True))

        kmax2 = lax.fori_loop(0, t_new // ch, conv, jnp.zeros((1, 1), F32))
        kmax_sc[...] = jnp.broadcast_to(kmax2, kmax_sc.shape)
        if n_past:
            kpb_sc[...] = kp_ref[0].astype(BF16)
            vpb_sc[...] = vp_ref[0].astype(BF16)

    u_mat = _suffix_ones(rg)
    tiles = functools.partial(_sb_tiles, scale=scale)
    gslice = lambda g: slice(g * rg, (g + 1) * rg)
    qs = [q_ref[0, gslice(g), :] for g in range(groups)]
    first_tile = qi * groups

    def tile_rows(idx):
        return pl.ds(pl.multiple_of((jnp.maximum(idx, -1) + 1) * rg, rg), rg)

    def any_group_continues(idxs, rsums):
        go = None
        for g in range(groups):
            go_g = (idxs[g] >= 1) & (jnp.max(rsums[g] + zb_sc[gslice(g), :]) > SB_LOG_WEIGHT_CUTOFF)
            go = go_g if go is None else go | go_g
        return go.astype(jnp.int32)

    idxs = [first_tile + g for g in range(groups)]
    rsums, contribs = tiles(qs, [kb_sc[tile_rows(i), :] for i in idxs], [vb_sc[tile_rows(i), :] for i in idxs],
                            u_mat, [jnp.zeros((rg, 1), F32)] * groups, masked=True)
    for g in range(groups):
        acc_sc[gslice(g), :] = contribs[g]
        r_sc[gslice(g), :] = rsums[g]

    if sweep_left:
        kmax = jnp.sqrt(kmax_sc[0:1, 0:1]) * (scale * 1.01)
        for g in range(groups):
            qf = qs[g].astype(F32)
            zb_sc[gslice(g), :] = jnp.sqrt(jnp.sum(qf * qf, axis=1, keepdims=True)) * kmax

        def body(carry):
            dist, _ = carry
            idxs = [first_tile + g - dist for g in range(groups)]
            rsums, contribs = tiles(qs, [kb_sc[tile_rows(i), :] for i in idxs],
                                    [vb_sc[tile_rows(i), :] for i in idxs], u_mat,
                                    [r_sc[gslice(g), :] for g in range(groups)], masked=False)
            for g in range(groups):
                acc_sc[gslice(g), :] += contribs[g]
                r_sc[gslice(g), :] = rsums[g]
            return dist + 1, any_group_continues(idxs, rsums)

        lax.while_loop(lambda carry: carry[1] != 0, body, (jnp.int32(1), any_group_continues(idxs, rsums)))

    if n_past:
        u_past = u_mat if bk_past == rg else _suffix_ones(bk_past)
        rsums = [r_sc[gslice(g), :] for g in range(groups)]
        for i in range(n_past - 1, -1, -1):
            sl = slice(i * bk_past, (i + 1) * bk_past)
            rsums, contribs = tiles(qs, [kpb_sc[sl, :]] * groups, [vpb_sc[sl, :]] * groups, u_past, rsums,
                                    masked=False)
            for g in range(groups):
                acc_sc[gslice(g), :] += contribs[g]

    o_ref[0] = acc_sc[...].astype(o_ref.dtype)


def _sb_attn(q, k_new, v_new, k_past, v_past, *, heads, rg=V7X_MXU_DIM, groups=4, bk_past=V7X_MXU_DIM):
    b, t, hd_all = q.shape
    d = hd_all // heads
    rg = _pick_tile(t, rg)
    groups = _pick_tile(t // rg, groups)
    bq = rg * groups
    has_past = k_past is not None
    p = k_past.shape[1] if has_past else 0
    if has_past:
        bk_past = _pick_tile(p, bk_past)
    n_past = p // bk_past if has_past else 0
    seq = lambda bi, hi, qi: (bi, 0, hi)
    in_specs = [pl.BlockSpec((1, bq, d), lambda bi, hi, qi: (bi, qi, hi)),
                pl.BlockSpec((1, t, d), seq), pl.BlockSpec((1, t, d), seq)]
    args = [q, k_new, v_new]
    scratch = [pltpu.VMEM((rg + t, d), BF16), pltpu.VMEM((rg + t, d), BF16),
               pltpu.VMEM((8, V7X_LANES), F32),
               pltpu.VMEM((bq, 1), F32), pltpu.VMEM((bq, 1), F32),
               pltpu.VMEM((bq, d), F32)]
    if has_past:
        in_specs += [pl.BlockSpec((1, p, d), seq), pl.BlockSpec((1, p, d), seq)]
        args += [k_past, v_past]
        scratch += [pltpu.VMEM((p, d), BF16), pltpu.VMEM((p, d), BF16)]
    return pl.pallas_call(
        functools.partial(_sb_kernel, rg=rg, groups=groups, t_new=t, bk_past=bk_past, n_past=n_past,
                          scale=d ** -0.5),
        grid=(b, heads, t // bq),
        in_specs=in_specs,
        out_specs=pl.BlockSpec((1, bq, d), lambda bi, hi, qi: (bi, qi, hi)),
        out_shape=jax.ShapeDtypeStruct((b, t, hd_all), BF16),
        scratch_shapes=scratch,
        compiler_params=_cparams(("parallel", "parallel", "arbitrary")),
        name="sb_attn",
    )(*args)


def _gla_kernel(*refs, chunk, n_chunks, has_s0, dk_scale):
    if has_s0:
        q_ref, k_ref, v_ref, r_ref, ga_ref, wa_ref, ba_ref, ng_ref, s0_ref, o_ref, s_out_ref, s_sc = refs
    else:
        q_ref, k_ref, v_ref, r_ref, ga_ref, wa_ref, ba_ref, ng_ref, o_ref, s_out_ref, s_sc = refs
    tg = pl.program_id(2)

    @pl.when(tg == 0)
    def _():
        s_sc[...] = s0_ref[0, 0] if has_s0 else jnp.zeros_like(s_sc)

    rows = lax.broadcasted_iota(jnp.int32, (chunk, chunk), 0)
    cols = lax.broadcasted_iota(jnp.int32, (chunk, chunk), 1)
    causal = cols <= rows
    tri = jnp.where(causal, 1.0, 0.0).astype(BF16)
    ones_cols = jnp.ones((chunk, V7X_LANES), BF16)
    dv = s_sc.shape[1]
    mid = chunk // 2 - 1

    def body(c, carry):
        sl = pl.ds(pl.multiple_of(c * chunk, chunk), chunk)
        x = jnp.dot(ga_ref[0, sl, :], wa_ref[0], preferred_element_type=F32) + ba_ref[0]
        g = _log_sigmoid(x) / GLA_GATE_NORM
        g_hi, g_lo = _split_hi_lo(g)
        b = jnp.dot(tri, g_hi, preferred_element_type=F32) + jnp.dot(tri, g_lo, preferred_element_type=F32)
        b_last = b[chunk - 1:chunk, :]
        b_mid = b[mid:mid + 1, :]
        tn_dims = (((0,), (0,)), ((), ()))
        b_last_col = (lax.dot_general(g_hi, ones_cols, tn_dims, preferred_element_type=F32)
                      + lax.dot_general(g_lo, ones_cols, tn_dims, preferred_element_type=F32))
        decay = jnp.concatenate([jnp.exp(b_last_col)] * (dv // V7X_LANES), axis=1)

        q = q_ref[0, sl, :].astype(F32) * dk_scale
        k = k_ref[0, sl, :].astype(F32)
        v = v_ref[0, sl, :]
        s_prev = s_sc[...]
        o_inter = jnp.dot((q * jnp.exp(b)).astype(BF16), s_prev.astype(BF16), preferred_element_type=F32)
        q_m = (q * jnp.exp(b - b_mid)).astype(BF16)
        k_m = (k * jnp.exp(b_mid - b)).astype(BF16)
        att = lax.dot_general(q_m, k_m, (((1,), (1,)), ((), ())), preferred_element_type=F32)
        att = jnp.where(causal, att, 0.0)
        o = o_inter + jnp.dot(att.astype(BF16), v, preferred_element_type=F32)
        k_st = (k * jnp.exp(b_last - b)).astype(BF16)
        s_sc[...] = decay * s_prev + lax.dot_general(k_st, v, tn_dims, preferred_element_type=F32)

        r = r_ref[0, sl, :].astype(F32)
        o_ref[0, sl, :] = (_rms(o, ng_ref[...]) * (r * jax.nn.sigmoid(r))).astype(o_ref.dtype)
        return carry

    lax.fori_loop(0, n_chunks, body, 0)

    @pl.when(tg == pl.num_programs(2) - 1)
    def _():
        s_out_ref[0, 0] = s_sc[...]


def _gla(q, k, v, r, ga, wa, ba, ng, s0, *, chunk, tg=512):
    b, t, _ = q.shape
    heads, _, dk = wa.shape
    dv = v.shape[-1] // heads
    chunk = min(chunk, t)
    tg = _pick_tile(t, tg)
    assert tg % chunk == 0
    has_s0 = s0 is not None
    tok = lambda bi, hi, ti: (bi, ti, hi)
    in_specs = [
        pl.BlockSpec((1, tg, dk), tok), pl.BlockSpec((1, tg, dk), tok),
        pl.BlockSpec((1, tg, dv), tok), pl.BlockSpec((1, tg, dv), tok),
        pl.BlockSpec((1, tg, V7X_LANES), lambda bi, hi, ti: (bi, ti, 0)),
        pl.BlockSpec((1, V7X_LANES, dk), lambda bi, hi, ti: (hi, 0, 0)),
        pl.BlockSpec((1, 1, dk), lambda bi, hi, ti: (hi, 0, 0)),
        pl.BlockSpec((1, dv), lambda bi, hi, ti: (0, 0)),
    ]
    args = [q, k, v, r, ga, wa, ba, ng]
    state_spec = pl.BlockSpec((1, 1, dk, dv), lambda bi, hi, ti: (bi, hi, 0, 0))
    if has_s0:
        in_specs.append(state_spec)
        args.append(s0)
    return pl.pallas_call(
        functools.partial(_gla_kernel, chunk=chunk, n_chunks=tg // chunk, has_s0=has_s0, dk_scale=dk ** -0.5),
        grid=(b, heads, t // tg),
        in_specs=in_specs,
        out_specs=[pl.BlockSpec((1, tg, dv), tok), state_spec],
        out_shape=[jax.ShapeDtypeStruct((b, t, heads * dv), BF16),
                   jax.ShapeDtypeStruct((b, heads, dk, dv), F32)],
        scratch_shapes=[pltpu.VMEM((dk, dv), F32)],
        compiler_params=_cparams(("parallel", "parallel", "arbitrary")),
        name="gla",
    )(*args)


def _mem_kernel(q_ref, k_ref, v_ref, o_ref, *, heads, scale):
    hd = q_ref.shape[-1] // heads
    for h in range(heads):
        cs = slice(h * hd, (h + 1) * hd)
        q = q_ref[0, :, cs]
        k = k_ref[0, :, cs].astype(BF16)
        v = v_ref[0, :, cs].astype(BF16)
        s = lax.dot_general(q, k, (((1,), (1,)), ((), ())), preferred_element_type=F32) * scale
        e = jnp.exp(s - jnp.max(s, axis=-1, keepdims=True))
        p = e / jnp.sum(e, axis=-1, keepdims=True)
        o_ref[0, :, cs] = jnp.dot(p.astype(BF16), v, preferred_element_type=F32).astype(o_ref.dtype)


def _mem_attn(q, mk, mv, *, heads, tq=1024):
    b, t, w = q.shape
    m = mk.shape[1]
    tq = _pick_tile(t, tq)
    mem = lambda bi, ti: (bi, 0, 0)
    return pl.pallas_call(
        functools.partial(_mem_kernel, heads=heads, scale=(w // heads) ** -0.5),
        grid=(b, t // tq),
        in_specs=[pl.BlockSpec((1, tq, w), lambda bi, ti: (bi, ti, 0)),
                  pl.BlockSpec((1, m, w), mem), pl.BlockSpec((1, m, w), mem)],
        out_specs=pl.BlockSpec((1, tq, w), lambda bi, ti: (bi, ti, 0)),
        out_shape=jax.ShapeDtypeStruct((b, t, w), BF16),
        compiler_params=_cparams(("parallel", "arbitrary")),
        name="mem_attn",
    )(q, mk, mv)


def _merge_kernel(h_ref, osb_ref, ogla_ref, omem_ref, g0_ref, g1_ref, g2_ref, wsb_ref, wgla_ref, wmem_ref,
                  wout_ref, post_g_ref, h2_ref, acc_sc):
    j = pl.program_id(1)

    @pl.when(j == 0)
    def _():
        acc_sc[...] = jnp.zeros_like(acc_sc)

    br_sb = jnp.dot(osb_ref[...], wsb_ref[...], preferred_element_type=F32)
    br_gla = jnp.dot(ogla_ref[...], wgla_ref[...], preferred_element_type=F32)
    br_mem = jnp.dot(omem_ref[...], wmem_ref[...], preferred_element_type=F32)
    merged = (g0_ref[...].astype(F32) * br_sb + g1_ref[...].astype(F32) * br_gla
              + g2_ref[...].astype(F32) * br_mem)
    acc_sc[...] += jnp.dot(merged.astype(BF16), wout_ref[...], preferred_element_type=F32)

    @pl.when(j == pl.num_programs(1) - 1)
    def _():
        h2_ref[...] = h_ref[...] + _rms(acc_sc[...], post_g_ref[...])


def _merge(h, o_sb, o_gla, o_mem, gates, w_sb, w_gla, w_mem, w_out, post_g, *, tm=512, tn=256):
    n, d = h.shape
    tm = _pick_tile(n, tm)
    nj = d // tn
    row = lambda i, j: (i, 0)
    col = lambda i, j: (0, j)
    vec = lambda i, j: (0, 0)
    return pl.pallas_call(
        _merge_kernel,
        grid=(n // tm, nj),
        in_specs=[
            pl.BlockSpec((tm, d), row),
            pl.BlockSpec((tm, o_sb.shape[1]), row),
            pl.BlockSpec((tm, o_gla.shape[1]), row),
            pl.BlockSpec((tm, o_mem.shape[1]), row),
            pl.BlockSpec((tm, tn), lambda i, j: (i, j)),
            pl.BlockSpec((tm, tn), lambda i, j: (i, nj + j)),
            pl.BlockSpec((tm, tn), lambda i, j: (i, 2 * nj + j)),
            pl.BlockSpec((w_sb.shape[0], tn), col),
            pl.BlockSpec((w_gla.shape[0], tn), col),
            pl.BlockSpec((w_mem.shape[0], tn), col),
            pl.BlockSpec((tn, d), lambda i, j: (j, 0)),
            pl.BlockSpec((1, d), vec),
        ],
        out_specs=pl.BlockSpec((tm, d), row),
        out_shape=jax.ShapeDtypeStruct((n, d), F32),
        scratch_shapes=[pltpu.VMEM((tm, d), F32)],
        compiler_params=_cparams(("parallel", "arbitrary")),
        name="merge",
    )(h, o_sb, o_gla, o_mem, gates, gates, gates, w_sb, w_gla, w_mem, w_out, post_g)


def _prep_weights(p, d):
    sb_w = d // 2
    gla_kw = d // 2
    gla_vw = d
    mem_w = d // 2
    w_in = p["w_in"]
    c = 0
    pieces = {}
    for name, width in (("sq", sb_w), ("sk", sb_w), ("sv", sb_w), ("gq", gla_kw), ("gk", gla_kw),
                        ("gv", gla_vw), ("gr", gla_vw), ("ga", GLA_LOW_RANK), ("mq", mem_w)):
        pieces[name] = w_in[:, c:c + width].astype(BF16)
        c += width
    assert c == w_in.shape[1]
    pad = V7X_LANES - GLA_LOW_RANK
    dk = gla_kw // GLA_HEADS
    w = dict(pieces)
    w["ga"] = jnp.pad(pieces["ga"], ((0, 0), (0, pad)))
    wa = jnp.pad(p["gla_w_a2"].astype(BF16), ((0, pad), (0, 0)))
    w["wa"] = wa.reshape(V7X_LANES, GLA_HEADS, dk).transpose(1, 0, 2)
    w["ba"] = p["gla_b_a2"].reshape(GLA_HEADS, 1, dk)
    for name in ("ffn1_w_gu", "ffn1_w_d", "ffn2_w_gu", "ffn2_w_d", "w_sb_br", "w_gla_br", "w_mem_br",
                 "w_gate", "w_out"):
        w[name] = p[name].astype(BF16)
    for name in ("ffn1_pre_g", "ffn1_post_g", "mix_pre_g", "mix_post_g", "ffn2_pre_g", "ffn2_post_g",
                 "gla_norm_g", "b_gate"):
        w[name] = p[name].reshape(1, -1)
    return w


def _layer(x, w, mem_k, mem_v, sb_past_k, sb_past_v, gla_s0):
    b, t, d = x.shape
    n = b * t
    sb_heads = (d // 2) // SB_HEAD_DIM
    h1, u = _ffn(x.reshape(n, d), w["ffn1_pre_g"], w["ffn1_w_gu"], w["ffn1_w_d"], w["ffn1_post_g"],
                 next_g=w["mix_pre_g"])
    sq, sk, sv, gq, gk, mq = _proj(u, [w[k] for k in ("sq", "sk", "sv", "gq", "gk", "mq")],
                                   [BF16, F32, F32, BF16, BF16, BF16])
    gv, gr = _proj(u, [w["gv"], w["gr"]], [BF16, BF16])
    (ga,) = _proj(u, [w["ga"]], [BF16])
    (gates,) = _proj(u, [w["w_gate"]], [BF16], biases=[w["b_gate"]], act="sigmoid")

    r3 = lambda a: a.reshape(b, t, -1)
    o_sb = _sb_attn(r3(sq), r3(sk), r3(sv), sb_past_k, sb_past_v, heads=sb_heads)
    o_gla, s_new = _gla(r3(gq), r3(gk), r3(gv), r3(gr), r3(ga), w["wa"], w["ba"], w["gla_norm_g"], gla_s0,
                        chunk=GLA_CHUNK)
    o_mem = _mem_attn(r3(mq), mem_k, mem_v, heads=MEM_HEADS)

    h2 = _merge(h1, o_sb.reshape(n, -1), o_gla.reshape(n, -1), o_mem.reshape(n, -1), gates,
                w["w_sb_br"], w["w_gla_br"], w["w_mem_br"], w["w_out"], w["mix_post_g"])
    y = _ffn(h2, w["ffn2_pre_g"], w["ffn2_w_gu"], w["ffn2_w_d"], w["ffn2_post_g"])
    return y.reshape(b, t, d), sk, sv, s_new


def kernel(x_prompt, x_sample, mem_prompt, cache_sb_k, cache_sb_v, state_gla, cache_mem_k, cache_mem_v, ffn1_pre_g, ffn1_w_gu, ffn1_w_d, ffn1_post_g, mix_pre_g, w_in, gla_w_a2, gla_b_a2, gla_norm_g, mem_norm_g, w_mem_kv, w_sb_br, w_gla_br, w_mem_br, w_gate, b_gate, w_out, mix_post_g, ffn2_pre_g, ffn2_w_gu, ffn2_w_d, ffn2_post_g):
    params = dict(ffn1_pre_g=ffn1_pre_g, ffn1_w_gu=ffn1_w_gu, ffn1_w_d=ffn1_w_d, ffn1_post_g=ffn1_post_g,
                  mix_pre_g=mix_pre_g, w_in=w_in, gla_w_a2=gla_w_a2, gla_b_a2=gla_b_a2, gla_norm_g=gla_norm_g,
                  w_sb_br=w_sb_br, w_gla_br=w_gla_br, w_mem_br=w_mem_br, w_gate=w_gate, b_gate=b_gate,
                  w_out=w_out, mix_post_g=mix_post_g, ffn2_pre_g=ffn2_pre_g, ffn2_w_gu=ffn2_w_gu,
                  ffn2_w_d=ffn2_w_d, ffn2_post_g=ffn2_post_g)
    depth = w_in.shape[0]
    bp, tp, d = x_prompt.shape
    bs, ts, _ = x_sample.shape
    m = mem_prompt.shape[1]
    mem_w = d // 2
    h_p, h_s = x_prompt, x_sample
    outs = [[] for _ in range(8)]
    for l in range(depth):
        w = _prep_weights({k: v[l] for k, v in params.items()}, d)
        w_mkv = w_mem_kv[l].astype(BF16)
        mk, mv = _proj(mem_prompt.reshape(bp * m, d), [w_mkv[:, :mem_w], w_mkv[:, mem_w:]], [F32, F32],
                       norm_g=mem_norm_g[l].reshape(1, d))
        mk = mk.reshape(bp, m, mem_w)
        mv = mv.reshape(bp, m, mem_w)
        h_p, k_p, v_p, s_p = _layer(h_p, w, mk, mv, None, None, None)
        past = cache_sb_k.shape[2]
        h_s, k_s, v_s, s_s = _layer(h_s, w, cache_mem_k[l].reshape(bs, m, mem_w), cache_mem_v[l].reshape(bs, m, mem_w),
                                    cache_sb_k[l].reshape(bs, past, -1), cache_sb_v[l].reshape(bs, past, -1),
                                    state_gla[l])
        sb_heads = mem_w // SB_HEAD_DIM
        for lst, val in zip(outs, (k_p.reshape(bp, tp, sb_heads, SB_HEAD_DIM), v_p.reshape(bp, tp, sb_heads, SB_HEAD_DIM),
                                   s_p, mk.reshape(bp, m, MEM_HEADS, -1), mv.reshape(bp, m, MEM_HEADS, -1),
                                   k_s.reshape(bs, ts, sb_heads, SB_HEAD_DIM), v_s.reshape(bs, ts, sb_heads, SB_HEAD_DIM),
                                   s_s)):
            lst.append(val)
    return (h_p, h_s) + tuple(jnp.stack(o) for o in outs)
```

```python
import functools

import jax
import jax.numpy as jnp
from jax import lax
from jax.experimental import pallas as pl
from jax.experimental.pallas import tpu as pltpu

F32 = jnp.float32
BF16 = jnp.bfloat16

EPS = 1e-6
V7X_LANES = 128
V7X_MXU_DIM = 256
V7X_VMEM_BYTES = 64 * 1024 * 1024
VMEM_LIMIT_BYTES = V7X_VMEM_BYTES - 8 * 1024 * 1024

SB_HEAD_DIM = 128
GLA_HEADS = 4
GLA_LOW_RANK = 16
GLA_GATE_NORM = 16.0
MEM_HEADS = 4
N_BRANCH = 3
GLA_CHUNK = 64
SB_LOG_WEIGHT_CUTOFF = -110.0


def _cparams(semantics):
    return pltpu.CompilerParams(dimension_semantics=semantics, vmem_limit_bytes=VMEM_LIMIT_BYTES)


def _rms(x, g):
    ms = jnp.mean(x * x, axis=-1, keepdims=True)
    return x * lax.rsqrt(ms + EPS) * g


def _log_sigmoid(x):
    return jnp.minimum(x, 0.0) - jnp.log(1.0 + jnp.exp(-jnp.abs(x)))


def _split_hi_lo(x):
    hi = x.astype(BF16)
    lo = (x - hi.astype(F32)).astype(BF16)
    return hi, lo


def _pick_tile(n, target):
    t = min(n, target)
    while n % t:
        t //= 2
    return t


def _ffn_kernel(x_ref, pre_g_ref, wg_ref, wu_ref, wd_ref, post_g_ref, *refs, emit_next):
    if emit_next:
        next_g_ref, h_ref, u_ref, xn_sc, acc_sc = refs
    else:
        h_ref, xn_sc, acc_sc = refs
    j = pl.program_id(1)

    @pl.when(j == 0)
    def _():
        xn_sc[...] = _rms(x_ref[...], pre_g_ref[...]).astype(BF16)
        acc_sc[...] = jnp.zeros_like(acc_sc)

    xn = xn_sc[...]
    g = jnp.dot(xn, wg_ref[...], preferred_element_type=F32)
    u = jnp.dot(xn, wu_ref[...], preferred_element_type=F32)
    act = (g * jax.nn.sigmoid(g) * u).astype(BF16)
    acc_sc[...] += jnp.dot(act, wd_ref[...], preferred_element_type=F32)

    @pl.when(j == pl.num_programs(1) - 1)
    def _():
        h = x_ref[...] + 0.5 * _rms(acc_sc[...], post_g_ref[...])
        h_ref[...] = h
        if emit_next:
            u_ref[...] = _rms(h, next_g_ref[...]).astype(BF16)


def _ffn(x, pre_g, w_gu, w_d, post_g, next_g=None, *, tm=512, tf=512):
    emit_next = next_g is not None
    n, d = x.shape
    d_ff = w_d.shape[0]
    tm = _pick_tile(n, tm)
    assert d_ff % tf == 0
    nf = d_ff // tf
    row = lambda i, j: (i, 0)
    vec = lambda i, j: (0, 0)
    out_shape = [jax.ShapeDtypeStruct((n, d), F32)]
    out_specs = [pl.BlockSpec((tm, d), row)]
    in_specs = [
        pl.BlockSpec((tm, d), row),
        pl.BlockSpec((1, d), vec),
        pl.BlockSpec((d, tf), lambda i, j: (0, j)),
        pl.BlockSpec((d, tf), lambda i, j: (0, nf + j)),
        pl.BlockSpec((tf, d), lambda i, j: (j, 0)),
        pl.BlockSpec((1, d), vec),
    ]
    args = [x, pre_g, w_gu, w_gu, w_d, post_g]
    if emit_next:
        in_specs.append(pl.BlockSpec((1, d), vec))
        args.append(next_g)
        out_shape.append(jax.ShapeDtypeStruct((n, d), BF16))
        out_specs.append(pl.BlockSpec((tm, d), row))
    outs = pl.pallas_call(
        functools.partial(_ffn_kernel, emit_next=emit_next),
        grid=(n // tm, nf),
        in_specs=in_specs,
        out_specs=out_specs,
        out_shape=out_shape,
        scratch_shapes=[pltpu.VMEM((tm, d), BF16), pltpu.VMEM((tm, d), F32)],
        compiler_params=_cparams(("parallel", "arbitrary")),
        name="ffn",
    )(*args)
    return outs if emit_next else outs[0]


def _proj_kernel(*refs, n_out, has_bias, has_norm, act):
    it = iter(refs)
    x_ref = next(it)
    g_ref = next(it) if has_norm else None
    w_refs = [next(it) for _ in range(n_out)]
    b_refs = [next(it) for _ in range(n_out)] if has_bias else None
    o_refs = [next(it) for _ in range(n_out)]
    x = _rms(x_ref[...], g_ref[...]).astype(BF16) if has_norm else x_ref[...]
    for k in range(n_out):
        acc = jnp.dot(x, w_refs[k][...], preferred_element_type=F32)
        if has_bias:
            acc = acc + b_refs[k][...]
        if act == "sigmoid":
            acc = jax.nn.sigmoid(acc)
        o_refs[k][...] = acc.astype(o_refs[k].dtype)


def _resident(shape):
    return pl.BlockSpec(shape, lambda *_: (0,) * len(shape), pipeline_mode=pl.Buffered(1))


def _proj(x, ws, out_dtypes, *, biases=None, norm_g=None, act=None, tm=512):
    n, kdim = x.shape
    tm = _pick_tile(n, tm)
    n_out = len(ws)
    in_specs = [pl.BlockSpec((tm, kdim), lambda i: (i, 0))]
    args = [x]
    if norm_g is not None:
        in_specs.append(_resident((1, kdim)))
        args.append(norm_g)
    in_specs += [_resident(w.shape) for w in ws]
    args += list(ws)
    if biases is not None:
        in_specs += [_resident(b.shape) for b in biases]
        args += list(biases)
    return pl.pallas_call(
        functools.partial(_proj_kernel, n_out=n_out, has_bias=biases is not None,
                          has_norm=norm_g is not None, act=act),
        grid=(n // tm,),
        in_specs=in_specs,
        out_specs=[pl.BlockSpec((tm, w.shape[1]), lambda i: (i, 0)) for w in ws],
        out_shape=[jax.ShapeDtypeStruct((n, w.shape[1]), dt) for w, dt in zip(ws, out_dtypes)],
        compiler_params=_cparams(("parallel",)),
        name="proj",
    )(*args)


def _suffix_ones(n):
    r = lax.broadcasted_iota(jnp.int32, (n, n), 0)
    c = lax.broadcasted_iota(jnp.int32, (n, n), 1)
    return jnp.where(r >= c, 1.0, 0.0).astype(BF16)


def _sb_tiles(qs, ks, vs, u_mat, rsums, *, scale, masked):
    n = len(qs)
    nt_dims = (((1,), (1,)), ((), ()))
    zs = [lax.dot_general(qs[g], ks[g], nt_dims, preferred_element_type=F32) * scale for g in range(n)]
    cs_in = [_log_sigmoid(-z) for z in zs]
    if masked:
        rows = lax.broadcasted_iota(jnp.int32, zs[0].shape, 0)
        cols = lax.broadcasted_iota(jnp.int32, zs[0].shape, 1)
        mask = cols < rows
        cs_in = [jnp.where(mask, c, 0.0) for c in cs_in]
    splits = [_split_hi_lo(c) for c in cs_in]
    css = [jnp.dot(hi, u_mat, preferred_element_type=F32) + jnp.dot(lo, u_mat, preferred_element_type=F32)
           for hi, lo in splits]
    weights = [jnp.exp(zs[g] + css[g] + rsums[g]) for g in range(n)]
    if masked:
        weights = [jnp.where(mask, a, 0.0) for a in weights]
    contribs = [jnp.dot(weights[g].astype(BF16), vs[g], preferred_element_type=F32) for g in range(n)]
    return [rsums[g] + css[g][:, 0:1] for g in range(n)], contribs


def _sb_kernel(*refs, rg, groups, t_new, bk_past, n_past, scale):
    if n_past:
        (q_ref, kn_ref, vn_ref, kp_ref, vp_ref, o_ref,
         kb_sc, vb_sc, kmax_sc, zb_sc, r_sc, acc_sc, kpb_sc, vpb_sc) = refs
    else:
        q_ref, kn_ref, vn_ref, o_ref, kb_sc, vb_sc, kmax_sc, zb_sc, r_sc, acc_sc = refs
    qi = pl.program_id(2)
    sweep_left = t_new > rg

    @pl.when(qi == 0)
    def _():
        kb_sc[0:rg, :] = jnp.zeros((rg, kb_sc.shape[1]), BF16)
        vb_sc[0:rg, :] = jnp.zeros((rg, vb_sc.shape[1]), BF16)
        ch = min(t_new, 512)

        def conv(i, kmax2):
            src = pl.ds(pl.multiple_of(i * ch, ch), ch)
            dst = pl.ds(pl.multiple_of(rg + i * ch, rg), ch)
            kb = kn_ref[0, src, :].astype(BF16)
            kb_sc[dst, :] = kb
            vb_sc[dst, :] = vn_ref[0, src, :].astype(BF16)
            kf = kb.astype(F32)
            n2 = jnp.sum(kf * kf, axis=1, keepdims=True)
            return jnp.maximum(kmax2, jnp.max(n2, axis=0, keepdims=True))

        kmax2 = lax.fori_loop(0, t_new // ch, conv, jnp.zeros((1, 1), F32))
        kmax_sc[...] = jnp.broadcast_to(kmax2, kmax_sc.shape)
        if n_past:
            kpb_sc[...] = kp_ref[0].astype(BF16)
            vpb_sc[...] = vp_ref[0].astype(BF16)

    u_mat = _suffix_ones(rg)
    tiles = functools.partial(_sb_tiles, scale=scale)
    gslice = lambda g: slice(g * rg, (g + 1) * rg)
    qs = [q_ref[0, gslice(g), :] for g in range(groups)]
    first_tile = qi * groups

    def tile_rows(idx):
        return pl.ds(pl.multiple_of((jnp.maximum(idx, -1) + 1) * rg, rg), rg)

    def any_group_continues(idxs, rsums):
        go = None
        for g in range(groups):
            go_g = (idxs[g] >= 1) & (jnp.max(rsums[g] + zb_sc[gslice(g), :]) > SB_LOG_WEIGHT_CUTOFF)
            go = go_g if go is None else go | go_g
        return go.astype(jnp.int32)

    idxs = [first_tile + g for g in range(groups)]
    rsums, contribs = tiles(qs, [kb_sc[tile_rows(i), :] for i in idxs], [vb_sc[tile_rows(i), :] for i in idxs],
                            u_mat, [jnp.zeros((rg, 1), F32)] * groups, masked=True)
    for g in range(groups):
        acc_sc[gslice(g), :] = contribs[g]
        r_sc[gslice(g), :] = rsums[g]

    if sweep_left:
        kmax = jnp.sqrt(kmax_sc[0:1, 0:1]) * (scale * 1.01)
        for g in range(groups):
            qf = qs[g].astype(F32)
            zb_sc[gslice(g), :] = jnp.sqrt(jnp.sum(qf * qf, axis=1, keepdims=True)) * kmax

        def body(carry):
            dist, _ = carry
            idxs = [first_tile + g - dist for g in range(groups)]
            rsums, contribs = tiles(qs, [kb_sc[tile_rows(i), :] for i in idxs],
                                    [vb_sc[tile_rows(i), :] for i in idxs], u_mat,
                                    [r_sc[gslice(g), :] for g in range(groups)], masked=False)
            for g in range(groups):
                acc_sc[gslice(g), :] += contribs[g]
                r_sc[gslice(g), :] = rsums[g]
            return dist + 1, any_group_continues(idxs, rsums)

        lax.while_loop(lambda carry: carry[1] != 0, body, (jnp.int32(1), any_group_continues(idxs, rsums)))

    if n_past:
        u_past = u_mat if bk_past == rg else _suffix_ones(bk_past)
        rsums = [r_sc[gslice(g), :] for g in range(groups)]
        for i in range(n_past - 1, -1, -1):
            sl = slice(i * bk_past, (i + 1) * bk_past)
            rsums, contribs = tiles(qs, [kpb_sc[sl, :]] * groups, [vpb_sc[sl, :]] * groups, u_past, rsums,
                                    masked=False)
            for g in range(groups):
                acc_sc[gslice(g), :] += contribs[g]

    o_ref[0] = acc_sc[...].astype(o_ref.dtype)


def _sb_attn(q, k_new, v_new, k_past, v_past, *, heads, rg=V7X_MXU_DIM, groups=4, bk_past=V7X_MXU_DIM):
    b, t, hd_all = q.shape
    d = hd_all // heads
    rg = _pick_tile(t, rg)
    groups = _pick_tile(t // rg, groups)
    bq = rg * groups
    has_past = k_past is not None
    p = k_past.shape[1] if has_past else 0
    if has_past:
        bk_past = _pick_tile(p, bk_past)
    n_past = p // bk_past if has_past else 0
    seq = lambda bi, hi, qi: (bi, 0, hi)
    in_specs = [pl.BlockSpec((1, bq, d), lambda bi, hi, qi: (bi, qi, hi)),
                pl.BlockSpec((1, t, d), seq), pl.BlockSpec((1, t, d), seq)]
    args = [q, k_new, v_new]
    scratch = [pltpu.VMEM((rg + t, d), BF16), pltpu.VMEM((rg + t, d), BF16),
               pltpu.VMEM((8, V7X_LANES), F32),
               pltpu.VMEM((bq, 1), F32), pltpu.VMEM((bq, 1), F32),
               pltpu.VMEM((bq, d), F32)]
    if has_past:
        in_specs += [pl.BlockSpec((1, p, d), seq), pl.BlockSpec((1, p, d), seq)]
        args += [k_past, v_past]
        scratch += [pltpu.VMEM((p, d), BF16), pltpu.VMEM((p, d), BF16)]
    return pl.pallas_call(
        functools.partial(_sb_kernel, rg=rg, groups=groups, t_new=t, bk_past=bk_past, n_past=n_past,
                          scale=d ** -0.5),
        grid=(b, heads, t // bq),
        in_specs=in_specs,
        out_specs=pl.BlockSpec((1, bq, d), lambda bi, hi, qi: (bi, qi, hi)),
        out_shape=jax.ShapeDtypeStruct((b, t, hd_all), BF16),
        scratch_shapes=scratch,
        compiler_params=_cparams(("parallel", "parallel", "arbitrary")),
        name="sb_attn",
    )(*args)


def _gla_kernel(*refs, chunk, n_chunks, heads, has_s0, dk_scale):
    if has_s0:
        q_ref, k_ref, v_ref, r_ref, ga_ref, wa_ref, ba_ref, ng_ref, s0_ref, o_ref, s_out_ref, s_sc = refs
    else:
        q_ref, k_ref, v_ref, r_ref, ga_ref, wa_ref, ba_ref, ng_ref, o_ref, s_out_ref, s_sc = refs
    tg = pl.program_id(1)

    @pl.when(tg == 0)
    def _():
        s_sc[...] = s0_ref[0] if has_s0 else jnp.zeros_like(s_sc)

    rows = lax.broadcasted_iota(jnp.int32, (chunk, chunk), 0)
    cols = lax.broadcasted_iota(jnp.int32, (chunk, chunk), 1)
    causal = cols <= rows
    tri = jnp.where(causal, 1.0, 0.0).astype(BF16)
    ones_cols = jnp.ones((chunk, V7X_LANES), BF16)
    _, dk, dv = s_sc.shape
    mid = chunk // 2 - 1
    nt_dims = (((1,), (1,)), ((), ()))
    tn_dims = (((0,), (0,)), ((), ()))
    hs = range(heads)
    kcols = lambda h: slice(h * dk, (h + 1) * dk)
    vcols = lambda h: slice(h * dv, (h + 1) * dv)

    def body(c, carry):
        sl = pl.ds(pl.multiple_of(c * chunk, chunk), chunk)
        ga = ga_ref[0, sl, :]
        gs = [_log_sigmoid(jnp.dot(ga, wa_ref[h], preferred_element_type=F32) + ba_ref[h]) / GLA_GATE_NORM
              for h in hs]
        g_split = [_split_hi_lo(g) for g in gs]
        bs = [jnp.dot(tri, hi, preferred_element_type=F32) + jnp.dot(tri, lo, preferred_element_type=F32)
              for hi, lo in g_split]
        b_last_cols = [lax.dot_general(hi, ones_cols, tn_dims, preferred_element_type=F32)
                       + lax.dot_general(lo, ones_cols, tn_dims, preferred_element_type=F32)
                       for hi, lo in g_split]
        qs = [q_ref[0, sl, kcols(h)].astype(F32) * dk_scale for h in hs]
        ks = [k_ref[0, sl, kcols(h)].astype(F32) for h in hs]
        vs = [v_ref[0, sl, vcols(h)] for h in hs]
        s_prev = [s_sc[h] for h in hs]
        q_in = [(qs[h] * jnp.exp(bs[h])).astype(BF16) for h in hs]
        q_m = [(qs[h] * jnp.exp(bs[h] - bs[h][mid:mid + 1, :])).astype(BF16) for h in hs]
        k_m = [(ks[h] * jnp.exp(bs[h][mid:mid + 1, :] - bs[h])).astype(BF16) for h in hs]
        k_st = [(ks[h] * jnp.exp(bs[h][chunk - 1:chunk, :] - bs[h])).astype(BF16) for h in hs]
        o_inter = [jnp.dot(q_in[h], s_prev[h].astype(BF16), preferred_element_type=F32) for h in hs]
        att = [jnp.where(causal, lax.dot_general(q_m[h], k_m[h], nt_dims, preferred_element_type=F32), 0.0)
               for h in hs]
        outs = [o_inter[h] + jnp.dot(att[h].astype(BF16), vs[h], preferred_element_type=F32) for h in hs]
        for h in hs:
            decay = jnp.concatenate([jnp.exp(b_last_cols[h])] * (dv // V7X_LANES), axis=1)
            s_sc[h] = decay * s_prev[h] + lax.dot_general(k_st[h], vs[h], tn_dims, preferred_element_type=F32)
        for h in hs:
            r = r_ref[0, sl, vcols(h)].astype(F32)
            o_ref[0, sl, vcols(h)] = (_rms(outs[h], ng_ref[...]) * (r * jax.nn.sigmoid(r))).astype(o_ref.dtype)
        return carry

    lax.fori_loop(0, n_chunks, body, 0)

    @pl.when(tg == pl.num_programs(1) - 1)
    def _():
        s_out_ref[0] = s_sc[...]


def _gla(q, k, v, r, ga, wa, ba, ng, s0, *, chunk, tg=512):
    b, t, _ = q.shape
    heads, _, dk = wa.shape
    dv = v.shape[-1] // heads
    chunk = min(chunk, t)
    tg = _pick_tile(t, tg)
    assert tg % chunk == 0
    has_s0 = s0 is not None
    tok = lambda bi, ti: (bi, ti, 0)
    in_specs = [
        pl.BlockSpec((1, tg, heads * dk), tok), pl.BlockSpec((1, tg, heads * dk), tok),
        pl.BlockSpec((1, tg, heads * dv), tok), pl.BlockSpec((1, tg, heads * dv), tok),
        pl.BlockSpec((1, tg, V7X_LANES), tok),
        _resident(wa.shape), _resident(ba.shape), _resident(ng.shape),
    ]
    args = [q, k, v, r, ga, wa, ba, ng]
    state_spec = pl.BlockSpec((1, heads, dk, dv), lambda bi, ti: (bi, 0, 0, 0))
    if has_s0:
        in_specs.append(state_spec)
        args.append(s0)
    return pl.pallas_call(
        functools.partial(_gla_kernel, chunk=chunk, n_chunks=tg // chunk, heads=heads, has_s0=has_s0,
                          dk_scale=dk ** -0.5),
        grid=(b, t // tg),
        in_specs=in_specs,
        out_specs=[pl.BlockSpec((1, tg, heads * dv), tok), state_spec],
        out_shape=[jax.ShapeDtypeStruct((b, t, heads * dv), BF16),
                   jax.ShapeDtypeStruct((b, heads, dk, dv), F32)],
        scratch_shapes=[pltpu.VMEM((heads, dk, dv), F32)],
        compiler_params=_cparams(("parallel", "arbitrary")),
        name="gla",
    )(*args)


def _mem_kernel(q_ref, k_ref, v_ref, o_ref, *, heads, scale):
    hd = q_ref.shape[-1] // heads
    for h in range(heads):
        cs = slice(h * hd, (h + 1) * hd)
        q = q_ref[0, :, cs]
        k = k_ref[0, :, cs].astype(BF16)
        v = v_ref[0, :, cs].astype(BF16)
        s = lax.dot_general(q, k, (((1,), (1,)), ((), ())), preferred_element_type=F32) * scale
        e = jnp.exp(s - jnp.max(s, axis=-1, keepdims=True))
        p = e / jnp.sum(e, axis=-1, keepdims=True)
        o_ref[0, :, cs] = jnp.dot(p.astype(BF16), v, preferred_element_type=F32).astype(o_ref.dtype)


def _mem_attn(q, mk, mv, *, heads, tq=1024):
    b, t, w = q.shape
    m = mk.shape[1]
    tq = _pick_tile(t, tq)
    mem = lambda bi, ti: (bi, 0, 0)
    return pl.pallas_call(
        functools.partial(_mem_kernel, heads=heads, scale=(w // heads) ** -0.5),
        grid=(b, t // tq),
        in_specs=[pl.BlockSpec((1, tq, w), lambda bi, ti: (bi, ti, 0)),
                  pl.BlockSpec((1, m, w), mem), pl.BlockSpec((1, m, w), mem)],
        out_specs=pl.BlockSpec((1, tq, w), lambda bi, ti: (bi, ti, 0)),
        out_shape=jax.ShapeDtypeStruct((b, t, w), BF16),
        compiler_params=_cparams(("parallel", "arbitrary")),
        name="mem_attn",
    )(q, mk, mv)


def _merge_kernel(h_ref, osb_ref, ogla_ref, omem_ref, g0_ref, g1_ref, g2_ref, wsb_ref, wgla_ref, wmem_ref,
                  wout_ref, post_g_ref, h2_ref, acc_sc):
    j = pl.program_id(1)

    @pl.when(j == 0)
    def _():
        acc_sc[...] = jnp.zeros_like(acc_sc)

    br_sb = jnp.dot(osb_ref[...], wsb_ref[...], preferred_element_type=F32)
    br_gla = jnp.dot(ogla_ref[...], wgla_ref[...], preferred_element_type=F32)
    br_mem = jnp.dot(omem_ref[...], wmem_ref[...], preferred_element_type=F32)
    merged = (g0_ref[...].astype(F32) * br_sb + g1_ref[...].astype(F32) * br_gla
              + g2_ref[...].astype(F32) * br_mem)
    acc_sc[...] += jnp.dot(merged.astype(BF16), wout_ref[...], preferred_element_type=F32)

    @pl.when(j == pl.num_programs(1) - 1)
    def _():
        h2_ref[...] = h_ref[...] + _rms(acc_sc[...], post_g_ref[...])


def _merge(h, o_sb, o_gla, o_mem, gates, w_sb, w_gla, w_mem, w_out, post_g, *, tm=512, tn=512):
    n, d = h.shape
    tm = _pick_tile(n, tm)
    nj = d // tn
    row = lambda i, j: (i, 0)
    col = lambda i, j: (0, j)
    vec = lambda i, j: (0, 0)
    return pl.pallas_call(
        _merge_kernel,
        grid=(n // tm, nj),
        in_specs=[
            pl.BlockSpec((tm, d), row),
            pl.BlockSpec((tm, o_sb.shape[1]), row),
            pl.BlockSpec((tm, o_gla.shape[1]), row),
            pl.BlockSpec((tm, o_mem.shape[1]), row),
            pl.BlockSpec((tm, tn), lambda i, j: (i, j)),
            pl.BlockSpec((tm, tn), lambda i, j: (i, nj + j)),
            pl.BlockSpec((tm, tn), lambda i, j: (i, 2 * nj + j)),
            pl.BlockSpec((w_sb.shape[0], tn), col),
            pl.BlockSpec((w_gla.shape[0], tn), col),
            pl.BlockSpec((w_mem.shape[0], tn), col),
            pl.BlockSpec((tn, d), lambda i, j: (j, 0)),
            pl.BlockSpec((1, d), vec),
        ],
        out_specs=pl.BlockSpec((tm, d), row),
        out_shape=jax.ShapeDtypeStruct((n, d), F32),
        scratch_shapes=[pltpu.VMEM((tm, d), F32)],
        compiler_params=_cparams(("parallel", "arbitrary")),
        name="merge",
    )(h, o_sb, o_gla, o_mem, gates, gates, gates, w_sb, w_gla, w_mem, w_out, post_g)


def _prep_weights(p, d):
    sb_w = d // 2
    gla_kw = d // 2
    gla_vw = d
    mem_w = d // 2
    w_in = p["w_in"]
    c = 0
    pieces = {}
    for name, width in (("sq", sb_w), ("sk", sb_w), ("sv", sb_w), ("gq", gla_kw), ("gk", gla_kw),
                        ("gv", gla_vw), ("gr", gla_vw), ("ga", GLA_LOW_RANK), ("mq", mem_w)):
        pieces[name] = w_in[:, c:c + width].astype(BF16)
        c += width
    assert c == w_in.shape[1]
    pad = V7X_LANES - GLA_LOW_RANK
    dk = gla_kw // GLA_HEADS
    w = dict(pieces)
    w["ga"] = jnp.pad(pieces["ga"], ((0, 0), (0, pad)))
    wa = jnp.pad(p["gla_w_a2"].astype(BF16), ((0, pad), (0, 0)))
    w["wa"] = wa.reshape(V7X_LANES, GLA_HEADS, dk).transpose(1, 0, 2)
    w["ba"] = p["gla_b_a2"].reshape(GLA_HEADS, 1, dk)
    for name in ("ffn1_w_gu", "ffn1_w_d", "ffn2_w_gu", "ffn2_w_d", "w_sb_br", "w_gla_br", "w_mem_br",
                 "w_gate", "w_out"):
        w[name] = p[name].astype(BF16)
    for name in ("ffn1_pre_g", "ffn1_post_g", "mix_pre_g", "mix_post_g", "ffn2_pre_g", "ffn2_post_g",
                 "gla_norm_g", "b_gate"):
        w[name] = p[name].reshape(1, -1)
    return w


def _layer(x, w, mem_k, mem_v, sb_past_k, sb_past_v, gla_s0):
    b, t, d = x.shape
    n = b * t
    sb_heads = (d // 2) // SB_HEAD_DIM
    h1, u = _ffn(x.reshape(n, d), w["ffn1_pre_g"], w["ffn1_w_gu"], w["ffn1_w_d"], w["ffn1_post_g"],
                 next_g=w["mix_pre_g"])
    sq, sk, sv, gq, gk, mq = _proj(u, [w[k] for k in ("sq", "sk", "sv", "gq", "gk", "mq")],
                                   [BF16, F32, F32, BF16, BF16, BF16])
    gv, gr, ga = _proj(u, [w["gv"], w["gr"], w["ga"]], [BF16, BF16, BF16])
    (gates,) = _proj(u, [w["w_gate"]], [BF16], biases=[w["b_gate"]], act="sigmoid")

    r3 = lambda a: a.reshape(b, t, -1)
    o_sb = _sb_attn(r3(sq), r3(sk), r3(sv), sb_past_k, sb_past_v, heads=sb_heads)
    o_gla, s_new = _gla(r3(gq), r3(gk), r3(gv), r3(gr), r3(ga), w["wa"], w["ba"], w["gla_norm_g"], gla_s0,
                        chunk=GLA_CHUNK)
    o_mem = _mem_attn(r3(mq), mem_k, mem_v, heads=MEM_HEADS)

    h2 = _merge(h1, o_sb.reshape(n, -1), o_gla.reshape(n, -1), o_mem.reshape(n, -1), gates,
                w["w_sb_br"], w["w_gla_br"], w["w_mem_br"], w["w_out"], w["mix_post_g"])
    y = _ffn(h2, w["ffn2_pre_g"], w["ffn2_w_gu"], w["ffn2_w_d"], w["ffn2_post_g"])
    return y.reshape(b, t, d), sk, sv, s_new


def kernel(x_prompt, x_sample, mem_prompt, cache_sb_k, cache_sb_v, state_gla, cache_mem_k, cache_mem_v, ffn1_pre_g, ffn1_w_gu, ffn1_w_d, ffn1_post_g, mix_pre_g, w_in, gla_w_a2, gla_b_a2, gla_norm_g, mem_norm_g, w_mem_kv, w_sb_br, w_gla_br, w_mem_br, w_gate, b_gate, w_out, mix_post_g, ffn2_pre_g, ffn2_w_gu, ffn2_w_d, ffn2_post_g):
    params = dict(ffn1_pre_g=ffn1_pre_g, ffn1_w_gu=ffn1_w_gu, ffn1_w_d=ffn1_w_d, ffn1_post_g=ffn1_post_g,
                  mix_pre_g=mix_pre_g, w_in=w_in, gla_w_a2=gla_w_a2, gla_b_a2=gla_b_a2, gla_norm_g=gla_norm_g,
                  w_sb_br=w_sb_br, w_gla_br=w_gla_br, w_mem_br=w_mem_br, w_gate=w_gate, b_gate=b_gate,
                  w_out=w_out, mix_post_g=mix_post_g, ffn2_pre_g=ffn2_pre_g, ffn2_w_gu=ffn2_w_gu,
                  ffn2_w_d=ffn2_w_d, ffn2_post_g=ffn2_post_g)
    depth = w_in.shape[0]
    bp, tp, d = x_prompt.shape
    bs, ts, _ = x_sample.shape
    m = mem_prompt.shape[1]
    mem_w = d // 2
    h_p, h_s = x_prompt, x_sample
    outs = [[] for _ in range(8)]
    for l in range(depth):
        w = _prep_weights({k: v[l] for k, v in params.items()}, d)
        w_mkv = w_mem_kv[l].astype(BF16)
        mk, mv = _proj(mem_prompt.reshape(bp * m, d), [w_mkv[:, :mem_w], w_mkv[:, mem_w:]], [F32, F32],
                       norm_g=mem_norm_g[l].reshape(1, d))
        mk = mk.reshape(bp, m, mem_w)
        mv = mv.reshape(bp, m, mem_w)
        h_p, k_p, v_p, s_p = _layer(h_p, w, mk, mv, None, None, None)
        past = cache_sb_k.shape[2]
        h_s, k_s, v_s, s_s = _layer(h_s, w, cache_mem_k[l].reshape(bs, m, mem_w), cache_mem_v[l].reshape(bs, m, mem_w),
                                    cache_sb_k[l].reshape(bs, past, -1), cache_sb_v[l].reshape(bs, past, -1),
                                    state_gla[l])
        sb_heads = mem_w // SB_HEAD_DIM
        for lst, val in zip(outs, (k_p.reshape(bp, tp, sb_heads, SB_HEAD_DIM), v_p.reshape(bp, tp, sb_heads, SB_HEAD_DIM),
                                   s_p, mk.reshape(bp, m, MEM_HEADS, -1), mv.reshape(bp, m, MEM_HEADS, -1),
                                   k_s.reshape(bs, ts, sb_heads, SB_HEAD_DIM), v_s.reshape(bs, ts, sb_heads, SB_HEAD_DIM),
                                   s_s)):
            lst.append(val)
    return (h_p, h_s) + tuple(jnp.stack(o) for o in outs)
```

```python
import functools

import jax
import jax.numpy as jnp
from jax import lax
from jax.experimental import pallas as pl
from jax.experimental.pallas import tpu as pltpu

F32 = jnp.float32
BF16 = jnp.bfloat16

EPS = 1e-6
V7X_LANES = 128
V7X_MXU_DIM = 256
V7X_VMEM_BYTES = 64 * 1024 * 1024
VMEM_LIMIT_BYTES = V7X_VMEM_BYTES - 8 * 1024 * 1024

SB_HEAD_DIM = 128
GLA_HEADS = 4
GLA_LOW_RANK = 16
GLA_GATE_NORM = 16.0
MEM_HEADS = 4
N_BRANCH = 3
GLA_CHUNK = 64
SB_LOG_WEIGHT_CUTOFF = -110.0


def _cparams(semantics):
    return pltpu.CompilerParams(dimension_semantics=semantics, vmem_limit_bytes=VMEM_LIMIT_BYTES)


def _rms(x, g):
    ms = jnp.mean(x * x, axis=-1, keepdims=True)
    return x * lax.rsqrt(ms + EPS) * g


def _log_sigmoid(x):
    return jnp.minimum(x, 0.0) - jnp.log(1.0 + jnp.exp(-jnp.abs(x)))


def _split_hi_lo(x):
    hi = x.astype(BF16)
    lo = (x - hi.astype(F32)).astype(BF16)
    return hi, lo


def _pick_tile(n, target):
    t = min(n, target)
    while n % t:
        t //= 2
    return t


def _ffn_kernel(x_ref, pre_g_ref, wg_ref, wu_ref, wd_ref, post_g_ref, *refs, emit_next):
    if emit_next:
        next_g_ref, h_ref, u_ref, xn_sc, acc_sc = refs
    else:
        h_ref, xn_sc, acc_sc = refs
    j = pl.program_id(1)

    @pl.when(j == 0)
    def _():
        xn_sc[...] = _rms(x_ref[...], pre_g_ref[...]).astype(BF16)
        acc_sc[...] = jnp.zeros_like(acc_sc)

    xn = xn_sc[...]
    g = jnp.dot(xn, wg_ref[...], preferred_element_type=F32)
    u = jnp.dot(xn, wu_ref[...], preferred_element_type=F32)
    act = (g * jax.nn.sigmoid(g) * u).astype(BF16)
    acc_sc[...] += jnp.dot(act, wd_ref[...], preferred_element_type=F32)

    @pl.when(j == pl.num_programs(1) - 1)
    def _():
        h = x_ref[...] + 0.5 * _rms(acc_sc[...], post_g_ref[...])
        h_ref[...] = h
        if emit_next:
            u_ref[...] = _rms(h, next_g_ref[...]).astype(BF16)


def _ffn(x, pre_g, w_gu, w_d, post_g, next_g=None, *, tm=512, tf=512):
    emit_next = next_g is not None
    n, d = x.shape
    d_ff = w_d.shape[0]
    tm = _pick_tile(n, tm)
    assert d_ff % tf == 0
    nf = d_ff // tf
    row = lambda i, j: (i, 0)
    vec = lambda i, j: (0, 0)
    out_shape = [jax.ShapeDtypeStruct((n, d), F32)]
    out_specs = [pl.BlockSpec((tm, d), row)]
    in_specs = [
        pl.BlockSpec((tm, d), row),
        pl.BlockSpec((1, d), vec),
        pl.BlockSpec((d, tf), lambda i, j: (0, j)),
        pl.BlockSpec((d, tf), lambda i, j: (0, nf + j)),
        pl.BlockSpec((tf, d), lambda i, j: (j, 0)),
        pl.BlockSpec((1, d), vec),
    ]
    args = [x, pre_g, w_gu, w_gu, w_d, post_g]
    if emit_next:
        in_specs.append(pl.BlockSpec((1, d), vec))
        args.append(next_g)
        out_shape.append(jax.ShapeDtypeStruct((n, d), BF16))
        out_specs.append(pl.BlockSpec((tm, d), row))
    outs = pl.pallas_call(
        functools.partial(_ffn_kernel, emit_next=emit_next),
        grid=(n // tm, nf),
        in_specs=in_specs,
        out_specs=out_specs,
        out_shape=out_shape,
        scratch_shapes=[pltpu.VMEM((tm, d), BF16), pltpu.VMEM((tm, d), F32)],
        compiler_params=_cparams(("parallel", "arbitrary")),
        name="ffn",
    )(*args)
    return outs if emit_next else outs[0]


def _proj_kernel(*refs, n_out, has_bias, has_norm, act):
    it = iter(refs)
    x_ref = next(it)
    g_ref = next(it) if has_norm else None
    w_refs = [next(it) for _ in range(n_out)]
    b_refs = [next(it) for _ in range(n_out)] if has_bias else None
    o_refs = [next(it) for _ in range(n_out)]
    x = _rms(x_ref[...], g_ref[...]).astype(BF16) if has_norm else x_ref[...]
    for k in range(n_out):
        acc = jnp.dot(x, w_refs[k][...], preferred_element_type=F32)
        if has_bias:
            acc = acc + b_refs[k][...]
        if act == "sigmoid":
            acc = jax.nn.sigmoid(acc)
        o_refs[k][...] = acc.astype(o_refs[k].dtype)


def _resident(shape):
    return pl.BlockSpec(shape, lambda *_: (0,) * len(shape), pipeline_mode=pl.Buffered(1))


def _proj(x, ws, out_dtypes, *, biases=None, norm_g=None, act=None, tm=512):
    n, kdim = x.shape
    tm = _pick_tile(n, tm)
    n_out = len(ws)
    in_specs = [pl.BlockSpec((tm, kdim), lambda i: (i, 0))]
    args = [x]
    if norm_g is not None:
        in_specs.append(_resident((1, kdim)))
        args.append(norm_g)
    in_specs += [_resident(w.shape) for w in ws]
    args += list(ws)
    if biases is not None:
        in_specs += [_resident(b.shape) for b in biases]
        args += list(biases)
    return pl.pallas_call(
        functools.partial(_proj_kernel, n_out=n_out, has_bias=biases is not None,
                          has_norm=norm_g is not None, act=act),
        grid=(n // tm,),
        in_specs=in_specs,
        out_specs=[pl.BlockSpec((tm, w.shape[1]), lambda i: (i, 0)) for w in ws],
        out_shape=[jax.ShapeDtypeStruct((n, w.shape[1]), dt) for w, dt in zip(ws, out_dtypes)],
        compiler_params=_cparams(("parallel",)),
        name="proj",
    )(*args)


def _suffix_ones(n):
    r = lax.broadcasted_iota(jnp.int32, (n, n), 0)
    c = lax.broadcasted_iota(jnp.int32, (n, n), 1)
    return jnp.where(r >= c, 1.0, 0.0).astype(BF16)


def _sb_tiles(qs, ks, vs, u_mat, rsums, *, scale, masked):
    n = len(qs)
    nt_dims = (((1,), (1,)), ((), ()))
    zs = [lax.dot_general(qs[g], ks[g], nt_dims, preferred_element_type=F32) * scale for g in range(n)]
    cs_in = [_log_sigmoid(-z) for z in zs]
    if masked:
        rows = lax.broadcasted_iota(jnp.int32, zs[0].shape, 0)
        cols = lax.broadcasted_iota(jnp.int32, zs[0].shape, 1)
        mask = cols < rows
        cs_in = [jnp.where(mask, c, 0.0) for c in cs_in]
    splits = [_split_hi_lo(c) for c in cs_in]
    css = [jnp.dot(hi, u_mat, preferred_element_type=F32) + jnp.dot(lo, u_mat, preferred_element_type=F32)
           for hi, lo in splits]
    weights = [jnp.exp(zs[g] + css[g] + rsums[g]) for g in range(n)]
    if masked:
        weights = [jnp.where(mask, a, 0.0) for a in weights]
    contribs = [jnp.dot(weights[g].astype(BF16), vs[g], preferred_element_type=F32) for g in range(n)]
    return [rsums[g] + css[g][:, 0:1] for g in range(n)], contribs


def _sb_kernel(q_ref, kn_ref, vn_ref, o_ref, kb_sc, vb_sc, kmax_sc, zb_sc, r_sc, acc_sc, *,
               rg, groups, t_new, scale):
    qi = pl.program_id(2)
    sweep_left = t_new > rg

    @pl.when(qi == 0)
    def _():
        kb_sc[0:rg, :] = jnp.zeros((rg, kb_sc.shape[1]), BF16)
        vb_sc[0:rg, :] = jnp.zeros((rg, vb_sc.shape[1]), BF16)
        ch = min(t_new, 512)

        def conv(i, kmax2):
            src = pl.ds(pl.multiple_of(i * ch, ch), ch)
            dst = pl.ds(pl.multiple_of(rg + i * ch, rg), ch)
            kb = kn_ref[0, src, :].astype(BF16)
            kb_sc[dst, :] = kb
            vb_sc[dst, :] = vn_ref[0, src, :].astype(BF16)
            kf = kb.astype(F32)
            n2 = jnp.sum(kf * kf, axis=1, keepdims=True)
            return jnp.maximum(kmax2, jnp.max(n2, axis=0, keepdims=True))

        kmax2 = lax.fori_loop(0, t_new // ch, conv, jnp.zeros((1, 1), F32))
        kmax_sc[...] = jnp.broadcast_to(kmax2, kmax_sc.shape)

    u_mat = _suffix_ones(rg)
    tiles = functools.partial(_sb_tiles, scale=scale)
    gslice = lambda g: slice(g * rg, (g + 1) * rg)
    qs = [q_ref[0, gslice(g), :] for g in range(groups)]
    first_tile = qi * groups

    def tile_rows(idx):
        return pl.ds(pl.multiple_of((jnp.maximum(idx, -1) + 1) * rg, rg), rg)

    def any_group_continues(idxs, rsums):
        go = None
        for g in range(groups):
            go_g = (idxs[g] >= 1) & (jnp.max(rsums[g] + zb_sc[gslice(g), :]) > SB_LOG_WEIGHT_CUTOFF)
            go = go_g if go is None else go | go_g
        return go.astype(jnp.int32)

    idxs = [first_tile + g for g in range(groups)]
    rsums, contribs = tiles(qs, [kb_sc[tile_rows(i), :] for i in idxs], [vb_sc[tile_rows(i), :] for i in idxs],
                            u_mat, [jnp.zeros((rg, 1), F32)] * groups, masked=True)
    for g in range(groups):
        acc_sc[gslice(g), :] = contribs[g]
        r_sc[gslice(g), :] = rsums[g]

    if sweep_left:
        kmax = jnp.sqrt(kmax_sc[0:1, 0:1]) * (scale * 1.01)
        for g in range(groups):
            qf = qs[g].astype(F32)
            zb_sc[gslice(g), :] = jnp.sqrt(jnp.sum(qf * qf, axis=1, keepdims=True)) * kmax

        def body(carry):
            dist, _ = carry
            idxs = [first_tile + g - dist for g in range(groups)]
            rsums, contribs = tiles(qs, [kb_sc[tile_rows(i), :] for i in idxs],
                                    [vb_sc[tile_rows(i), :] for i in idxs], u_mat,
                                    [r_sc[gslice(g), :] for g in range(groups)], masked=False)
            for g in range(groups):
                acc_sc[gslice(g), :] += contribs[g]
                r_sc[gslice(g), :] = rsums[g]
            return dist + 1, any_group_continues(idxs, rsums)

        lax.while_loop(lambda carry: carry[1] != 0, body, (jnp.int32(1), any_group_continues(idxs, rsums)))

    o_ref[0] = acc_sc[...].astype(o_ref.dtype)


def _sb_attn(q, k, v, *, heads, rg=V7X_MXU_DIM, groups=4):
    b, t, hd_all = q.shape
    d = hd_all // heads
    rg = _pick_tile(t, rg)
    groups = _pick_tile(t // rg, groups)
    bq = rg * groups
    seq = lambda bi, hi, qi: (bi, 0, hi)
    blk = lambda bi, hi, qi: (bi, qi, hi)
    return pl.pallas_call(
        functools.partial(_sb_kernel, rg=rg, groups=groups, t_new=t, scale=d ** -0.5),
        grid=(b, heads, t // bq),
        in_specs=[pl.BlockSpec((1, bq, d), blk), pl.BlockSpec((1, t, d), seq), pl.BlockSpec((1, t, d), seq)],
        out_specs=pl.BlockSpec((1, bq, d), blk),
        out_shape=jax.ShapeDtypeStruct((b, t, hd_all), BF16),
        scratch_shapes=[pltpu.VMEM((rg + t, d), BF16), pltpu.VMEM((rg + t, d), BF16),
                        pltpu.VMEM((8, V7X_LANES), F32),
                        pltpu.VMEM((bq, 1), F32), pltpu.VMEM((bq, 1), F32),
                        pltpu.VMEM((bq, d), F32)],
        compiler_params=_cparams(("parallel", "parallel", "arbitrary")),
        name="sb_attn",
    )(q, k, v)


def _sb_decode_kernel(q_ref, kn_ref, vn_ref, kp_ref, vp_ref, o_ref, *, heads, bk, scale):
    t = q_ref.shape[1]
    d = q_ref.shape[2] // heads
    past = kp_ref.shape[1] // heads
    hs = range(heads)
    cols = lambda h: slice(h * d, (h + 1) * d)
    tiles = functools.partial(_sb_tiles, scale=scale)
    qs = [q_ref[0, :, cols(h)] for h in hs]
    rsums, accs = tiles(qs, [kn_ref[0, :, cols(h)].astype(BF16) for h in hs],
                        [vn_ref[0, :, cols(h)].astype(BF16) for h in hs], _suffix_ones(t),
                        [jnp.zeros((t, 1), F32)] * heads, masked=True)
    u_past = _suffix_ones(bk)
    for i in range(past // bk - 1, -1, -1):
        head_rows = lambda h: pl.ds(i * bk * heads + h, bk, stride=heads)
        rsums, contribs = tiles(qs, [kp_ref[0, head_rows(h), :].astype(BF16) for h in hs],
                                [vp_ref[0, head_rows(h), :].astype(BF16) for h in hs], u_past, rsums,
                                masked=False)
        accs = [a + c for a, c in zip(accs, contribs)]
    for h in hs:
        o_ref[0, :, cols(h)] = accs[h].astype(o_ref.dtype)


def _sb_decode(q, k_new, v_new, k_cache, v_cache, *, bk=V7X_MXU_DIM):
    b, t, hd_all = q.shape
    _, p, heads, d = k_cache.shape
    bk = _pick_tile(p, bk)
    new = pl.BlockSpec((1, t, hd_all), lambda bi: (bi, 0, 0))
    old = pl.BlockSpec((1, p * heads, d), lambda bi: (bi, 0, 0))
    return pl.pallas_call(
        functools.partial(_sb_decode_kernel, heads=heads, bk=bk, scale=d ** -0.5),
        grid=(b,),
        in_specs=[new, new, new, old, old],
        out_specs=new,
        out_shape=jax.ShapeDtypeStruct((b, t, hd_all), BF16),
        compiler_params=_cparams(("parallel",)),
        name="sb_decode",
    )(q, k_new, v_new, k_cache.reshape(b, p * heads, d), v_cache.reshape(b, p * heads, d))


def _gla_kernel(*refs, chunk, n_chunks, heads, has_s0, dk_scale):
    if has_s0:
        q_ref, k_ref, v_ref, r_ref, ga_ref, wa_ref, ba_ref, ng_ref, s0_ref, o_ref, s_out_ref, s_sc = refs
    else:
        q_ref, k_ref, v_ref, r_ref, ga_ref, wa_ref, ba_ref, ng_ref, o_ref, s_out_ref, s_sc = refs
    tg = pl.program_id(1)

    @pl.when(tg == 0)
    def _():
        s_sc[...] = s0_ref[0] if has_s0 else jnp.zeros_like(s_sc)

    rows = lax.broadcasted_iota(jnp.int32, (chunk, chunk), 0)
    cols = lax.broadcasted_iota(jnp.int32, (chunk, chunk), 1)
    causal = cols <= rows
    tri = jnp.where(causal, 1.0, 0.0).astype(BF16)
    ones_cols = jnp.ones((chunk, V7X_LANES), BF16)
    _, dk, dv = s_sc.shape
    mid = chunk // 2 - 1
    nt_dims = (((1,), (1,)), ((), ()))
    tn_dims = (((0,), (0,)), ((), ()))
    hs = range(heads)
    kcols = lambda h: slice(h * dk, (h + 1) * dk)
    vcols = lambda h: slice(h * dv, (h + 1) * dv)

    def body(c, carry):
        sl = pl.ds(pl.multiple_of(c * chunk, chunk), chunk)
        ga = ga_ref[0, sl, :]
        gs = [_log_sigmoid(jnp.dot(ga, wa_ref[h], preferred_element_type=F32) + ba_ref[h]) / GLA_GATE_NORM
              for h in hs]
        g_split = [_split_hi_lo(g) for g in gs]
        bs = [jnp.dot(tri, hi, preferred_element_type=F32) + jnp.dot(tri, lo, preferred_element_type=F32)
              for hi, lo in g_split]
        b_last_cols = [lax.dot_general(hi, ones_cols, tn_dims, preferred_element_type=F32)
                       + lax.dot_general(lo, ones_cols, tn_dims, preferred_element_type=F32)
                       for hi, lo in g_split]
        qs = [q_ref[0, sl, kcols(h)].astype(F32) * dk_scale for h in hs]
        ks = [k_ref[0, sl, kcols(h)].astype(F32) for h in hs]
        vs = [v_ref[0, sl, vcols(h)] for h in hs]
        s_prev = [s_sc[h] for h in hs]
        q_in = [(qs[h] * jnp.exp(bs[h])).astype(BF16) for h in hs]
        q_m = [(qs[h] * jnp.exp(bs[h] - bs[h][mid:mid + 1, :])).astype(BF16) for h in hs]
        k_m = [(ks[h] * jnp.exp(bs[h][mid:mid + 1, :] - bs[h])).astype(BF16) for h in hs]
        k_st = [(ks[h] * jnp.exp(bs[h][chunk - 1:chunk, :] - bs[h])).astype(BF16) for h in hs]
        o_inter = [jnp.dot(q_in[h], s_prev[h].astype(BF16), preferred_element_type=F32) for h in hs]
        att = [jnp.where(causal, lax.dot_general(q_m[h], k_m[h], nt_dims, preferred_element_type=F32), 0.0)
               for h in hs]
        outs = [o_inter[h] + jnp.dot(att[h].astype(BF16), vs[h], preferred_element_type=F32) for h in hs]
        for h in hs:
            decay = jnp.concatenate([jnp.exp(b_last_cols[h])] * (dv // V7X_LANES), axis=1)
            s_sc[h] = decay * s_prev[h] + lax.dot_general(k_st[h], vs[h], tn_dims, preferred_element_type=F32)
        for h in hs:
            r = r_ref[0, sl, vcols(h)].astype(F32)
            o_ref[0, sl, vcols(h)] = (_rms(outs[h], ng_ref[...]) * (r * jax.nn.sigmoid(r))).astype(o_ref.dtype)
        return carry

    lax.fori_loop(0, n_chunks, body, 0)

    @pl.when(tg == pl.num_programs(1) - 1)
    def _():
        s_out_ref[0] = s_sc[...]


def _gla(q, k, v, r, ga, wa, ba, ng, s0, *, chunk, tg=512):
    b, t, _ = q.shape
    heads, _, dk = wa.shape
    dv = v.shape[-1] // heads
    chunk = min(chunk, t)
    tg = _pick_tile(t, tg)
    assert tg % chunk == 0
    has_s0 = s0 is not None
    tok = lambda bi, ti: (bi, ti, 0)
    in_specs = [
        pl.BlockSpec((1, tg, heads * dk), tok), pl.BlockSpec((1, tg, heads * dk), tok),
        pl.BlockSpec((1, tg, heads * dv), tok), pl.BlockSpec((1, tg, heads * dv), tok),
        pl.BlockSpec((1, tg, V7X_LANES), tok),
        _resident(wa.shape), _resident(ba.shape), _resident(ng.shape),
    ]
    args = [q, k, v, r, ga, wa, ba, ng]
    state_spec = pl.BlockSpec((1, heads, dk, dv), lambda bi, ti: (bi, 0, 0, 0))
    if has_s0:
        in_specs.append(state_spec)
        args.append(s0)
    return pl.pallas_call(
        functools.partial(_gla_kernel, chunk=chunk, n_chunks=tg // chunk, heads=heads, has_s0=has_s0,
                          dk_scale=dk ** -0.5),
        grid=(b, t // tg),
        in_specs=in_specs,
        out_specs=[pl.BlockSpec((1, tg, heads * dv), tok), state_spec],
        out_shape=[jax.ShapeDtypeStruct((b, t, heads * dv), BF16),
                   jax.ShapeDtypeStruct((b, heads, dk, dv), F32)],
        scratch_shapes=[pltpu.VMEM((heads, dk, dv), F32)],
        compiler_params=_cparams(("parallel", "arbitrary")),
        name="gla",
    )(*args)


def _mem_kernel(q_ref, k_ref, v_ref, o_ref, *, heads, scale):
    hd = q_ref.shape[-1] // heads
    for h in range(heads):
        cs = slice(h * hd, (h + 1) * hd)
        q = q_ref[0, :, cs]
        k = k_ref[0, :, cs].astype(BF16)
        v = v_ref[0, :, cs].astype(BF16)
        s = lax.dot_general(q, k, (((1,), (1,)), ((), ())), preferred_element_type=F32) * scale
        e = jnp.exp(s - jnp.max(s, axis=-1, keepdims=True))
        p = e / jnp.sum(e, axis=-1, keepdims=True)
        o_ref[0, :, cs] = jnp.dot(p.astype(BF16), v, preferred_element_type=F32).astype(o_ref.dtype)


def _mem_attn(q, mk, mv, *, heads, tq=1024):
    b, t, w = q.shape
    m = mk.shape[1]
    tq = _pick_tile(t, tq)
    mem = lambda bi, ti: (bi, 0, 0)
    return pl.pallas_call(
        functools.partial(_mem_kernel, heads=heads, scale=(w // heads) ** -0.5),
        grid=(b, t // tq),
        in_specs=[pl.BlockSpec((1, tq, w), lambda bi, ti: (bi, ti, 0)),
                  pl.BlockSpec((1, m, w), mem), pl.BlockSpec((1, m, w), mem)],
        out_specs=pl.BlockSpec((1, tq, w), lambda bi, ti: (bi, ti, 0)),
        out_shape=jax.ShapeDtypeStruct((b, t, w), BF16),
        compiler_params=_cparams(("parallel", "arbitrary")),
        name="mem_attn",
    )(q, mk, mv)


def _merge_kernel(h_ref, osb_ref, ogla_ref, omem_ref, gates_ref, wsb_ref, wgla_ref, wmem_ref, wout_ref,
                  post_g_ref, h2_ref):
    d = h_ref.shape[1]
    branches = ((osb_ref, wsb_ref), (ogla_ref, wgla_ref), (omem_ref, wmem_ref))
    merged = None
    for k, (o_ref, w_ref) in enumerate(branches):
        term = gates_ref[:, k * d:(k + 1) * d].astype(F32) * jnp.dot(o_ref[...], w_ref[...],
                                                                    preferred_element_type=F32)
        merged = term if merged is None else merged + term
    m = jnp.dot(merged.astype(BF16), wout_ref[...], preferred_element_type=F32)
    h2_ref[...] = h_ref[...] + _rms(m, post_g_ref[...])


def _merge(h, o_sb, o_gla, o_mem, gates, w_sb, w_gla, w_mem, w_out, post_g, *, tm=256):
    n, d = h.shape
    tm = _pick_tile(n, tm)
    row = lambda a: pl.BlockSpec((tm, a.shape[1]), lambda i: (i, 0))
    return pl.pallas_call(
        _merge_kernel,
        grid=(n // tm,),
        in_specs=[row(h), row(o_sb), row(o_gla), row(o_mem), row(gates),
                  _resident(w_sb.shape), _resident(w_gla.shape), _resident(w_mem.shape),
                  _resident(w_out.shape), _resident(post_g.shape)],
        out_specs=row(h),
        out_shape=jax.ShapeDtypeStruct((n, d), F32),
        compiler_params=_cparams(("parallel",)),
        name="merge",
    )(h, o_sb, o_gla, o_mem, gates, w_sb, w_gla, w_mem, w_out, post_g)


def _prep_weights(p, d):
    sb_w = d // 2
    gla_kw = d // 2
    gla_vw = d
    mem_w = d // 2
    w_in = p["w_in"]
    c = 0
    pieces = {}
    for name, width in (("sq", sb_w), ("sk", sb_w), ("sv", sb_w), ("gq", gla_kw), ("gk", gla_kw),
                        ("gv", gla_vw), ("gr", gla_vw), ("ga", GLA_LOW_RANK), ("mq", mem_w)):
        pieces[name] = w_in[:, c:c + width].astype(BF16)
        c += width
    assert c == w_in.shape[1]
    pad = V7X_LANES - GLA_LOW_RANK
    dk = gla_kw // GLA_HEADS
    w = dict(pieces)
    w["ga"] = jnp.pad(pieces["ga"], ((0, 0), (0, pad)))
    wa = jnp.pad(p["gla_w_a2"].astype(BF16), ((0, pad), (0, 0)))
    w["wa"] = wa.reshape(V7X_LANES, GLA_HEADS, dk).transpose(1, 0, 2)
    w["ba"] = p["gla_b_a2"].reshape(GLA_HEADS, 1, dk)
    for name in ("ffn1_w_gu", "ffn1_w_d", "ffn2_w_gu", "ffn2_w_d", "w_sb_br", "w_gla_br", "w_mem_br",
                 "w_gate", "w_out"):
        w[name] = p[name].astype(BF16)
    for name in ("ffn1_pre_g", "ffn1_post_g", "mix_pre_g", "mix_post_g", "ffn2_pre_g", "ffn2_post_g",
                 "gla_norm_g", "b_gate"):
        w[name] = p[name].reshape(1, -1)
    return w


def _layer(x, w, mem_k, mem_v, sb_past_k, sb_past_v, gla_s0):
    b, t, d = x.shape
    n = b * t
    sb_heads = (d // 2) // SB_HEAD_DIM
    h1, u = _ffn(x.reshape(n, d), w["ffn1_pre_g"], w["ffn1_w_gu"], w["ffn1_w_d"], w["ffn1_post_g"],
                 next_g=w["mix_pre_g"])
    sq, sk, sv, gq, gk, mq = _proj(u, [w[k] for k in ("sq", "sk", "sv", "gq", "gk", "mq")],
                                   [BF16, F32, F32, BF16, BF16, BF16])
    gv, gr, ga = _proj(u, [w["gv"], w["gr"], w["ga"]], [BF16, BF16, BF16])
    (gates,) = _proj(u, [w["w_gate"]], [BF16], biases=[w["b_gate"]], act="sigmoid")

    r3 = lambda a: a.reshape(b, t, -1)
    if sb_past_k is None:
        o_sb = _sb_attn(r3(sq), r3(sk), r3(sv), heads=sb_heads)
    else:
        o_sb = _sb_decode(r3(sq), r3(sk), r3(sv), sb_past_k, sb_past_v)
    o_gla, s_new = _gla(r3(gq), r3(gk), r3(gv), r3(gr), r3(ga), w["wa"], w["ba"], w["gla_norm_g"], gla_s0,
                        chunk=GLA_CHUNK)
    o_mem = _mem_attn(r3(mq), mem_k, mem_v, heads=MEM_HEADS)

    h2 = _merge(h1, o_sb.reshape(n, -1), o_gla.reshape(n, -1), o_mem.reshape(n, -1), gates,
                w["w_sb_br"], w["w_gla_br"], w["w_mem_br"], w["w_out"], w["mix_post_g"])
    y = _ffn(h2, w["ffn2_pre_g"], w["ffn2_w_gu"], w["ffn2_w_d"], w["ffn2_post_g"])
    return y.reshape(b, t, d), sk, sv, s_new


def kernel(x_prompt, x_sample, mem_prompt, cache_sb_k, cache_sb_v, state_gla, cache_mem_k, cache_mem_v, ffn1_pre_g, ffn1_w_gu, ffn1_w_d, ffn1_post_g, mix_pre_g, w_in, gla_w_a2, gla_b_a2, gla_norm_g, mem_norm_g, w_mem_kv, w_sb_br, w_gla_br, w_mem_br, w_gate, b_gate, w_out, mix_post_g, ffn2_pre_g, ffn2_w_gu, ffn2_w_d, ffn2_post_g):
    params = dict(ffn1_pre_g=ffn1_pre_g, ffn1_w_gu=ffn1_w_gu, ffn1_w_d=ffn1_w_d, ffn1_post_g=ffn1_post_g,
                  mix_pre_g=mix_pre_g, w_in=w_in, gla_w_a2=gla_w_a2, gla_b_a2=gla_b_a2, gla_norm_g=gla_norm_g,
                  w_sb_br=w_sb_br, w_gla_br=w_gla_br, w_mem_br=w_mem_br, w_gate=w_gate, b_gate=b_gate,
                  w_out=w_out, mix_post_g=mix_post_g, ffn2_pre_g=ffn2_pre_g, ffn2_w_gu=ffn2_w_gu,
                  ffn2_w_d=ffn2_w_d, ffn2_post_g=ffn2_post_g)
    depth = w_in.shape[0]
    bp, tp, d = x_prompt.shape
    bs, ts, _ = x_sample.shape
    m = mem_prompt.shape[1]
    mem_w = d // 2
    h_p, h_s = x_prompt, x_sample
    outs = [[] for _ in range(8)]
    for l in range(depth):
        w = _prep_weights({k: v[l] for k, v in params.items()}, d)
        w_mkv = w_mem_kv[l].astype(BF16)
        mk, mv = _proj(mem_prompt.reshape(bp * m, d), [w_mkv[:, :mem_w], w_mkv[:, mem_w:]], [F32, F32],
                       norm_g=mem_norm_g[l].reshape(1, d))
        mk = mk.reshape(bp, m, mem_w)
        mv = mv.reshape(bp, m, mem_w)
        h_p, k_p, v_p, s_p = _layer(h_p, w, mk, mv, None, None, None)
        h_s, k_s, v_s, s_s = _layer(h_s, w, cache_mem_k[l].reshape(bs, m, mem_w), cache_mem_v[l].reshape(bs, m, mem_w),
                                    cache_sb_k[l], cache_sb_v[l], state_gla[l])
        sb_heads = mem_w // SB_HEAD_DIM
        for lst, val in zip(outs, (k_p.reshape(bp, tp, sb_heads, SB_HEAD_DIM), v_p.reshape(bp, tp, sb_heads, SB_HEAD_DIM),
                                   s_p, mk.reshape(bp, m, MEM_HEADS, -1), mv.reshape(bp, m, MEM_HEADS, -1),
                                   k_s.reshape(bs, ts, sb_heads, SB_HEAD_DIM), v_s.reshape(bs, ts, sb_heads, SB_HEAD_DIM),
                                   s_s)):
            lst.append(val)
    return (h_p, h_s) + tuple(jnp.stack(o) for o in outs)
```

```python
import functools

import jax
import jax.numpy as jnp
from jax import lax
from jax.experimental import pallas as pl
from jax.experimental.pallas import tpu as pltpu

F32 = jnp.float32
BF16 = jnp.bfloat16

EPS = 1e-6
V7X_LANES = 128
V7X_MXU_DIM = 256
V7X_VMEM_BYTES = 64 * 1024 * 1024
VMEM_LIMIT_BYTES = V7X_VMEM_BYTES - 4 * 1024 * 1024

SB_HEAD_DIM = 128
GLA_HEADS = 4
GLA_LOW_RANK = 16
GLA_GATE_NORM = 16.0
MEM_HEADS = 4
N_BRANCH = 3
GLA_CHUNK = 64
FFN_NORM_ROWS = 128
SB_LOG_WEIGHT_CUTOFF = -110.0


def _cparams(semantics):
    return pltpu.CompilerParams(dimension_semantics=semantics, vmem_limit_bytes=VMEM_LIMIT_BYTES)


def _rms(x, g):
    ms = jnp.mean(x * x, axis=-1, keepdims=True)
    return x * lax.rsqrt(ms + EPS) * g


def _log_sigmoid(x):
    return jnp.minimum(x, 0.0) - jnp.log(1.0 + jnp.exp(-jnp.abs(x)))


def _split_hi_lo(x):
    hi = x.astype(BF16)
    lo = (x - hi.astype(F32)).astype(BF16)
    return hi, lo


def _pick_tile(n, target):
    t = min(n, target)
    while n % t:
        t //= 2
    return t


def _ffn_kernel(x_ref, pre_g_ref, wg_ref, wu_ref, wd_ref, post_g_ref, *refs, emit_next):
    if emit_next:
        next_g_ref, h_ref, u_ref, xn_sc, acc_sc = refs
    else:
        h_ref, xn_sc, acc_sc = refs
    j = pl.program_id(1)
    tm = x_ref.shape[0]
    rc = min(tm, FFN_NORM_ROWS)

    def for_row_chunks(fn):
        def body(c, carry):
            fn(pl.ds(pl.multiple_of(c * rc, rc), rc))
            return carry
        lax.fori_loop(0, tm // rc, body, 0)

    @pl.when(j == 0)
    def _():
        def prologue(rows):
            xn_sc[rows, :] = _rms(x_ref[rows, :], pre_g_ref[...]).astype(BF16)
            acc_sc[rows, :] = jnp.zeros((rc, acc_sc.shape[1]), F32)
        for_row_chunks(prologue)

    xn = xn_sc[...]
    g = jnp.dot(xn, wg_ref[...], preferred_element_type=F32)
    u = jnp.dot(xn, wu_ref[...], preferred_element_type=F32)
    act = (g * jax.nn.sigmoid(g) * u).astype(BF16)
    acc_sc[...] += jnp.dot(act, wd_ref[...], preferred_element_type=F32)

    @pl.when(j == pl.num_programs(1) - 1)
    def _():
        def epilogue(rows):
            h = x_ref[rows, :] + 0.5 * _rms(acc_sc[rows, :], post_g_ref[...])
            h_ref[rows, :] = h
            if emit_next:
                u_ref[rows, :] = _rms(h, next_g_ref[...]).astype(BF16)
        for_row_chunks(epilogue)


def _ffn(x, pre_g, w_gu, w_d, post_g, next_g=None, *, tm=1024, tf=512):
    emit_next = next_g is not None
    n, d = x.shape
    d_ff = w_d.shape[0]
    tm = _pick_tile(n, tm)
    tf = max(t for t in range(V7X_LANES, tf + 1, V7X_LANES) if d_ff % t == 0)
    nf = d_ff // tf
    row = lambda i, j: (i, 0)
    out_row = pl.BlockSpec((tm, d), row, pipeline_mode=pl.Buffered(1))
    out_shape = [jax.ShapeDtypeStruct((n, d), F32)]
    out_specs = [out_row]
    in_specs = [
        pl.BlockSpec((tm, d), row),
        _resident((1, d)),
        pl.BlockSpec((d, tf), lambda i, j: (0, j)),
        pl.BlockSpec((d, tf), lambda i, j: (0, nf + j)),
        pl.BlockSpec((tf, d), lambda i, j: (j, 0)),
        _resident((1, d)),
    ]
    args = [x, pre_g, w_gu, w_gu, w_d, post_g]
    if emit_next:
        in_specs.append(_resident((1, d)))
        args.append(next_g)
        out_shape.append(jax.ShapeDtypeStruct((n, d), BF16))
        out_specs.append(out_row)
    outs = pl.pallas_call(
        functools.partial(_ffn_kernel, emit_next=emit_next),
        grid=(n // tm, nf),
        in_specs=in_specs,
        out_specs=out_specs,
        out_shape=out_shape,
        scratch_shapes=[pltpu.VMEM((tm, d), BF16), pltpu.VMEM((tm, d), F32)],
        compiler_params=_cparams(("parallel", "arbitrary")),
        name="ffn",
    )(*args)
    return outs if emit_next else outs[0]


def _proj_kernel(*refs, n_out, has_bias, has_norm, act):
    it = iter(refs)
    x_ref = next(it)
    g_ref = next(it) if has_norm else None
    w_refs = [next(it) for _ in range(n_out)]
    b_refs = [next(it) for _ in range(n_out)] if has_bias else None
    o_refs = [next(it) for _ in range(n_out)]
    x = _rms(x_ref[...], g_ref[...]).astype(BF16) if has_norm else x_ref[...]
    for k in range(n_out):
        acc = jnp.dot(x, w_refs[k][...], preferred_element_type=F32)
        if has_bias:
            acc = acc + b_refs[k][...]
        if act == "sigmoid":
            acc = jax.nn.sigmoid(acc)
        o_refs[k][...] = acc.astype(o_refs[k].dtype)


def _resident(shape):
    return pl.BlockSpec(shape, lambda *_: (0,) * len(shape), pipeline_mode=pl.Buffered(1))


def _proj(x, ws, out_dtypes, *, biases=None, norm_g=None, act=None, tm=512):
    n, kdim = x.shape
    tm = _pick_tile(n, tm)
    n_out = len(ws)
    in_specs = [pl.BlockSpec((tm, kdim), lambda i: (i, 0))]
    args = [x]
    if norm_g is not None:
        in_specs.append(_resident((1, kdim)))
        args.append(norm_g)
    in_specs += [_resident(w.shape) for w in ws]
    args += list(ws)
    if biases is not None:
        in_specs += [_resident(b.shape) for b in biases]
        args += list(biases)
    return pl.pallas_call(
        functools.partial(_proj_kernel, n_out=n_out, has_bias=biases is not None,
                          has_norm=norm_g is not None, act=act),
        grid=(n // tm,),
        in_specs=in_specs,
        out_specs=[pl.BlockSpec((tm, w.shape[1]), lambda i: (i, 0)) for w in ws],
        out_shape=[jax.ShapeDtypeStruct((n, w.shape[1]), dt) for w, dt in zip(ws, out_dtypes)],
        compiler_params=_cparams(("parallel",)),
        name="proj",
    )(*args)


def _suffix_ones(n):
    r = lax.broadcasted_iota(jnp.int32, (n, n), 0)
    c = lax.broadcasted_iota(jnp.int32, (n, n), 1)
    return jnp.where(r >= c, 1.0, 0.0).astype(BF16)


def _sb_tiles(qs, ks, vs, u_mat, rsums, *, scale, masked):
    n = len(qs)
    nt_dims = (((1,), (1,)), ((), ()))
    zs = [lax.dot_general(qs[g], ks[g], nt_dims, preferred_element_type=F32) * scale for g in range(n)]
    cs_in = [_log_sigmoid(-z) for z in zs]
    if masked:
        rows = lax.broadcasted_iota(jnp.int32, zs[0].shape, 0)
        cols = lax.broadcasted_iota(jnp.int32, zs[0].shape, 1)
        mask = cols < rows
        cs_in = [jnp.where(mask, c, 0.0) for c in cs_in]
    splits = [_split_hi_lo(c) for c in cs_in]
    css = [jnp.dot(hi, u_mat, preferred_element_type=F32) + jnp.dot(lo, u_mat, preferred_element_type=F32)
           for hi, lo in splits]
    weights = [jnp.exp(zs[g] + css[g] + rsums[g]) for g in range(n)]
    if masked:
        weights = [jnp.where(mask, a, 0.0) for a in weights]
    contribs = [jnp.dot(weights[g].astype(BF16), vs[g], preferred_element_type=F32) for g in range(n)]
    return [rsums[g] + css[g][:, 0:1] for g in range(n)], contribs


def _sb_kernel(q_ref, kn_ref, vn_ref, o_ref, kb_sc, vb_sc, kmax_sc, zb_sc, r_sc, acc_sc, *,
               rg, groups, t_new, scale):
    qi = pl.program_id(2)
    sweep_left = t_new > rg

    @pl.when(qi == 0)
    def _():
        kb_sc[0:rg, :] = jnp.zeros((rg, kb_sc.shape[1]), BF16)
        vb_sc[0:rg, :] = jnp.zeros((rg, vb_sc.shape[1]), BF16)
        ch = min(t_new, 512)

        def conv(i, kmax2):
            src = pl.ds(pl.multiple_of(i * ch, ch), ch)
            dst = pl.ds(pl.multiple_of(rg + i * ch, rg), ch)
            kb = kn_ref[0, src, :].astype(BF16)
            kb_sc[dst, :] = kb
            vb_sc[dst, :] = vn_ref[0, src, :].astype(BF16)
            kf = kb.astype(F32)
            n2 = jnp.sum(kf * kf, axis=1, keepdims=True)
            return jnp.maximum(kmax2, jnp.max(n2, axis=0, keepdims=True))

        kmax2 = lax.fori_loop(0, t_new // ch, conv, jnp.zeros((1, 1), F32))
        kmax_sc[...] = jnp.broadcast_to(kmax2, kmax_sc.shape)

    u_mat = _suffix_ones(rg)
    tiles = functools.partial(_sb_tiles, scale=scale)
    gslice = lambda g: slice(g * rg, (g + 1) * rg)
    qs = [q_ref[0, gslice(g), :] for g in range(groups)]
    first_tile = qi * groups

    def tile_rows(idx):
        return pl.ds(pl.multiple_of((jnp.maximum(idx, -1) + 1) * rg, rg), rg)

    def any_group_continues(idxs, rsums):
        go = None
        for g in range(groups):
            go_g = (idxs[g] >= 1) & (jnp.max(rsums[g] + zb_sc[gslice(g), :]) > SB_LOG_WEIGHT_CUTOFF)
            go = go_g if go is None else go | go_g
        return go.astype(jnp.int32)

    idxs = [first_tile + g for g in range(groups)]
    rsums, contribs = tiles(qs, [kb_sc[tile_rows(i), :] for i in idxs], [vb_sc[tile_rows(i), :] for i in idxs],
                            u_mat, [jnp.zeros((rg, 1), F32)] * groups, masked=True)
    for g in range(groups):
        acc_sc[gslice(g), :] = contribs[g]
        r_sc[gslice(g), :] = rsums[g]

    if sweep_left:
        kmax = jnp.sqrt(kmax_sc[0:1, 0:1]) * (scale * 1.01)
        for g in range(groups):
            qf = qs[g].astype(F32)
            zb_sc[gslice(g), :] = jnp.sqrt(jnp.sum(qf * qf, axis=1, keepdims=True)) * kmax

        def body(carry):
            dist, _ = carry
            idxs = [first_tile + g - dist for g in range(groups)]
            rsums, contribs = tiles(qs, [kb_sc[tile_rows(i), :] for i in idxs],
                                    [vb_sc[tile_rows(i), :] for i in idxs], u_mat,
                                    [r_sc[gslice(g), :] for g in range(groups)], masked=False)
            for g in range(groups):
                acc_sc[gslice(g), :] += contribs[g]
                r_sc[gslice(g), :] = rsums[g]
            return dist + 1, any_group_continues(idxs, rsums)

        lax.while_loop(lambda carry: carry[1] != 0, body, (jnp.int32(1), any_group_continues(idxs, rsums)))

    o_ref[0] = acc_sc[...].astype(o_ref.dtype)


def _sb_attn(q, k, v, *, heads, rg=V7X_MXU_DIM, groups=4):
    b, t, hd_all = q.shape
    d = hd_all // heads
    rg = _pick_tile(t, rg)
    groups = _pick_tile(t // rg, groups)
    bq = rg * groups
    seq = lambda bi, hi, qi: (bi, 0, hi)
    blk = lambda bi, hi, qi: (bi, qi, hi)
    return pl.pallas_call(
        functools.partial(_sb_kernel, rg=rg, groups=groups, t_new=t, scale=d ** -0.5),
        grid=(b, heads, t // bq),
        in_specs=[pl.BlockSpec((1, bq, d), blk), pl.BlockSpec((1, t, d), seq), pl.BlockSpec((1, t, d), seq)],
        out_specs=pl.BlockSpec((1, bq, d), blk),
        out_shape=jax.ShapeDtypeStruct((b, t, hd_all), BF16),
        scratch_shapes=[pltpu.VMEM((rg + t, d), BF16), pltpu.VMEM((rg + t, d), BF16),
                        pltpu.VMEM((8, V7X_LANES), F32),
                        pltpu.VMEM((bq, 1), F32), pltpu.VMEM((bq, 1), F32),
                        pltpu.VMEM((bq, d), F32)],
        compiler_params=_cparams(("parallel", "parallel", "arbitrary")),
        name="sb_attn",
    )(q, k, v)


def _sb_decode_kernel(q_ref, kn_ref, vn_ref, kp_ref, vp_ref, o_ref, *, heads, bk, scale):
    t = q_ref.shape[1]
    d = q_ref.shape[2] // heads
    past = kp_ref.shape[1] // heads
    hs = range(heads)
    cols = lambda h: slice(h * d, (h + 1) * d)
    tiles = functools.partial(_sb_tiles, scale=scale)
    qs = [q_ref[0, :, cols(h)] for h in hs]
    rsums, accs = tiles(qs, [kn_ref[0, :, cols(h)].astype(BF16) for h in hs],
                        [vn_ref[0, :, cols(h)].astype(BF16) for h in hs], _suffix_ones(t),
                        [jnp.zeros((t, 1), F32)] * heads, masked=True)
    u_past = _suffix_ones(bk)
    for i in range(past // bk - 1, -1, -1):
        head_rows = lambda h: pl.ds(i * bk * heads + h, bk, stride=heads)
        rsums, contribs = tiles(qs, [kp_ref[0, head_rows(h), :].astype(BF16) for h in hs],
                                [vp_ref[0, head_rows(h), :].astype(BF16) for h in hs], u_past, rsums,
                                masked=False)
        accs = [a + c for a, c in zip(accs, contribs)]
    for h in hs:
        o_ref[0, :, cols(h)] = accs[h].astype(o_ref.dtype)


def _sb_decode(q, k_new, v_new, k_cache, v_cache, *, bk=V7X_MXU_DIM):
    b, t, hd_all = q.shape
    _, p, heads, d = k_cache.shape
    bk = _pick_tile(p, bk)
    new = pl.BlockSpec((1, t, hd_all), lambda bi: (bi, 0, 0))
    old = pl.BlockSpec((1, p * heads, d), lambda bi: (bi, 0, 0))
    return pl.pallas_call(
        functools.partial(_sb_decode_kernel, heads=heads, bk=bk, scale=d ** -0.5),
        grid=(b,),
        in_specs=[new, new, new, old, old],
        out_specs=new,
        out_shape=jax.ShapeDtypeStruct((b, t, hd_all), BF16),
        compiler_params=_cparams(("parallel",)),
        name="sb_decode",
    )(q, k_new, v_new, k_cache.reshape(b, p * heads, d), v_cache.reshape(b, p * heads, d))


def _gla_kernel(*refs, chunk, n_chunks, heads, has_s0, dk_scale):
    if has_s0:
        q_ref, k_ref, v_ref, r_ref, ga_ref, wa_ref, ba_ref, ng_ref, s0_ref, o_ref, s_out_ref, s_sc = refs
    else:
        q_ref, k_ref, v_ref, r_ref, ga_ref, wa_ref, ba_ref, ng_ref, o_ref, s_out_ref, s_sc = refs
    tg = pl.program_id(1)

    @pl.when(tg == 0)
    def _():
        s_sc[...] = s0_ref[0] if has_s0 else jnp.zeros_like(s_sc)

    rows = lax.broadcasted_iota(jnp.int32, (chunk, chunk), 0)
    cols = lax.broadcasted_iota(jnp.int32, (chunk, chunk), 1)
    causal = cols <= rows
    tri = jnp.where(causal, 1.0, 0.0).astype(BF16)
    ones_cols = jnp.ones((chunk, V7X_LANES), BF16)
    _, dk, dv = s_sc.shape
    mid = chunk // 2 - 1
    nt_dims = (((1,), (1,)), ((), ()))
    tn_dims = (((0,), (0,)), ((), ()))
    hs = range(heads)
    kcols = lambda h: slice(h * dk, (h + 1) * dk)
    vcols = lambda h: slice(h * dv, (h + 1) * dv)

    def body(c, carry):
        sl = pl.ds(pl.multiple_of(c * chunk, chunk), chunk)
        ga = ga_ref[0, sl, :]
        gs = [_log_sigmoid(jnp.dot(ga, wa_ref[h], preferred_element_type=F32) + ba_ref[h]) / GLA_GATE_NORM
              for h in hs]
        g_split = [_split_hi_lo(g) for g in gs]
        bs = [jnp.dot(tri, hi, preferred_element_type=F32) + jnp.dot(tri, lo, preferred_element_type=F32)
              for hi, lo in g_split]
        b_last_cols = [lax.dot_general(hi, ones_cols, tn_dims, preferred_element_type=F32)
                       + lax.dot_general(lo, ones_cols, tn_dims, preferred_element_type=F32)
                       for hi, lo in g_split]
        qs = [q_ref[0, sl, kcols(h)].astype(F32) * dk_scale for h in hs]
        ks = [k_ref[0, sl, kcols(h)].astype(F32) for h in hs]
        vs = [v_ref[0, sl, vcols(h)] for h in hs]
        s_prev = [s_sc[h] for h in hs]
        q_in = [(qs[h] * jnp.exp(bs[h])).astype(BF16) for h in hs]
        q_m = [(qs[h] * jnp.exp(bs[h] - bs[h][mid:mid + 1, :])).astype(BF16) for h in hs]
        k_m = [(ks[h] * jnp.exp(bs[h][mid:mid + 1, :] - bs[h])).astype(BF16) for h in hs]
        k_st = [(ks[h] * jnp.exp(bs[h][chunk - 1:chunk, :] - bs[h])).astype(BF16) for h in hs]
        o_inter = [jnp.dot(q_in[h], s_prev[h].astype(BF16), preferred_element_type=F32) for h in hs]
        att = [jnp.where(causal, lax.dot_general(q_m[h], k_m[h], nt_dims, preferred_element_type=F32), 0.0)
               for h in hs]
        outs = [o_inter[h] + jnp.dot(att[h].astype(BF16), vs[h], preferred_element_type=F32) for h in hs]
        for h in hs:
            decay = jnp.concatenate([jnp.exp(b_last_cols[h])] * (dv // V7X_LANES), axis=1)
            s_sc[h] = decay * s_prev[h] + lax.dot_general(k_st[h], vs[h], tn_dims, preferred_element_type=F32)
        for h in hs:
            r = r_ref[0, sl, vcols(h)].astype(F32)
            o_ref[0, sl, vcols(h)] = (_rms(outs[h], ng_ref[...]) * (r * jax.nn.sigmoid(r))).astype(o_ref.dtype)
        return carry

    lax.fori_loop(0, n_chunks, body, 0)

    @pl.when(tg == pl.num_programs(1) - 1)
    def _():
        s_out_ref[0] = s_sc[...]


def _gla(q, k, v, r, ga, wa, ba, ng, s0, *, chunk, tg=512):
    b, t, _ = q.shape
    heads, _, dk = wa.shape
    dv = v.shape[-1] // heads
    chunk = min(chunk, t)
    tg = _pick_tile(t, tg)
    assert tg % chunk == 0
    has_s0 = s0 is not None
    tok = lambda bi, ti: (bi, ti, 0)
    in_specs = [
        pl.BlockSpec((1, tg, heads * dk), tok), pl.BlockSpec((1, tg, heads * dk), tok),
        pl.BlockSpec((1, tg, heads * dv), tok), pl.BlockSpec((1, tg, heads * dv), tok),
        pl.BlockSpec((1, tg, V7X_LANES), tok),
        _resident(wa.shape), _resident(ba.shape), _resident(ng.shape),
    ]
    args = [q, k, v, r, ga, wa, ba, ng]
    state_spec = pl.BlockSpec((1, heads, dk, dv), lambda bi, ti: (bi, 0, 0, 0))
    if has_s0:
        in_specs.append(state_spec)
        args.append(s0)
    return pl.pallas_call(
        functools.partial(_gla_kernel, chunk=chunk, n_chunks=tg // chunk, heads=heads, has_s0=has_s0,
                          dk_scale=dk ** -0.5),
        grid=(b, t // tg),
        in_specs=in_specs,
        out_specs=[pl.BlockSpec((1, tg, heads * dv), tok), state_spec],
        out_shape=[jax.ShapeDtypeStruct((b, t, heads * dv), BF16),
                   jax.ShapeDtypeStruct((b, heads, dk, dv), F32)],
        scratch_shapes=[pltpu.VMEM((heads, dk, dv), F32)],
        compiler_params=_cparams(("parallel", "arbitrary")),
        name="gla",
    )(*args)


def _mem_kernel(q_ref, k_ref, v_ref, o_ref, *, heads, scale):
    hd = q_ref.shape[-1] // heads
    for h in range(heads):
        cs = slice(h * hd, (h + 1) * hd)
        q = q_ref[0, :, cs]
        k = k_ref[0, :, cs].astype(BF16)
        v = v_ref[0, :, cs].astype(BF16)
        s = lax.dot_general(q, k, (((1,), (1,)), ((), ())), preferred_element_type=F32) * scale
        e = jnp.exp(s - jnp.max(s, axis=-1, keepdims=True))
        p = e / jnp.sum(e, axis=-1, keepdims=True)
        o_ref[0, :, cs] = jnp.dot(p.astype(BF16), v, preferred_element_type=F32).astype(o_ref.dtype)


def _mem_attn(q, mk, mv, *, heads, tq=1024):
    b, t, w = q.shape
    m = mk.shape[1]
    tq = _pick_tile(t, tq)
    mem = lambda bi, ti: (bi, 0, 0)
    return pl.pallas_call(
        functools.partial(_mem_kernel, heads=heads, scale=(w // heads) ** -0.5),
        grid=(b, t // tq),
        in_specs=[pl.BlockSpec((1, tq, w), lambda bi, ti: (bi, ti, 0)),
                  pl.BlockSpec((1, m, w), mem), pl.BlockSpec((1, m, w), mem)],
        out_specs=pl.BlockSpec((1, tq, w), lambda bi, ti: (bi, ti, 0)),
        out_shape=jax.ShapeDtypeStruct((b, t, w), BF16),
        compiler_params=_cparams(("parallel", "arbitrary")),
        name="mem_attn",
    )(q, mk, mv)


def _merge_kernel(h_ref, osb_ref, ogla_ref, omem_ref, gates_ref, wsb_ref, wgla_ref, wmem_ref, wout_ref,
                  post_g_ref, h2_ref):
    d = h_ref.shape[1]
    branches = ((osb_ref, wsb_ref), (ogla_ref, wgla_ref), (omem_ref, wmem_ref))
    merged = None
    for k, (o_ref, w_ref) in enumerate(branches):
        term = gates_ref[:, k * d:(k + 1) * d].astype(F32) * jnp.dot(o_ref[...], w_ref[...],
                                                                    preferred_element_type=F32)
        merged = term if merged is None else merged + term
    m = jnp.dot(merged.astype(BF16), wout_ref[...], preferred_element_type=F32)
    h2_ref[...] = h_ref[...] + _rms(m, post_g_ref[...])


def _merge(h, o_sb, o_gla, o_mem, gates, w_sb, w_gla, w_mem, w_out, post_g, *, tm=256):
    n, d = h.shape
    tm = _pick_tile(n, tm)
    row = lambda a: pl.BlockSpec((tm, a.shape[1]), lambda i: (i, 0))
    return pl.pallas_call(
        _merge_kernel,
        grid=(n // tm,),
        in_specs=[row(h), row(o_sb), row(o_gla), row(o_mem), row(gates),
                  _resident(w_sb.shape), _resident(w_gla.shape), _resident(w_mem.shape),
                  _resident(w_out.shape), _resident(post_g.shape)],
        out_specs=row(h),
        out_shape=jax.ShapeDtypeStruct((n, d), F32),
        compiler_params=_cparams(("parallel",)),
        name="merge",
    )(h, o_sb, o_gla, o_mem, gates, w_sb, w_gla, w_mem, w_out, post_g)


def _prep_weights(p, d):
    sb_w = d // 2
    gla_kw = d // 2
    gla_vw = d
    mem_w = d // 2
    w_in = p["w_in"]
    c = 0
    pieces = {}
    for name, width in (("sq", sb_w), ("sk", sb_w), ("sv", sb_w), ("gq", gla_kw), ("gk", gla_kw),
                        ("gv", gla_vw), ("gr", gla_vw), ("ga", GLA_LOW_RANK), ("mq", mem_w)):
        pieces[name] = w_in[:, c:c + width].astype(BF16)
        c += width
    assert c == w_in.shape[1]
    pad = V7X_LANES - GLA_LOW_RANK
    dk = gla_kw // GLA_HEADS
    w = dict(pieces)
    w["ga"] = jnp.pad(pieces["ga"], ((0, 0), (0, pad)))
    wa = jnp.pad(p["gla_w_a2"].astype(BF16), ((0, pad), (0, 0)))
    w["wa"] = wa.reshape(V7X_LANES, GLA_HEADS, dk).transpose(1, 0, 2)
    w["ba"] = p["gla_b_a2"].reshape(GLA_HEADS, 1, dk)
    for name in ("ffn1_w_gu", "ffn1_w_d", "ffn2_w_gu", "ffn2_w_d", "w_sb_br", "w_gla_br", "w_mem_br",
                 "w_gate", "w_out"):
        w[name] = p[name].astype(BF16)
    for name in ("ffn1_pre_g", "ffn1_post_g", "mix_pre_g", "mix_post_g", "ffn2_pre_g", "ffn2_post_g",
                 "gla_norm_g", "b_gate"):
        w[name] = p[name].reshape(1, -1)
    return w


def _layer(x, w, mem_k, mem_v, sb_past_k, sb_past_v, gla_s0):
    b, t, d = x.shape
    n = b * t
    sb_heads = (d // 2) // SB_HEAD_DIM
    h1, u = _ffn(x.reshape(n, d), w["ffn1_pre_g"], w["ffn1_w_gu"], w["ffn1_w_d"], w["ffn1_post_g"],
                 next_g=w["mix_pre_g"])
    sq, sk, sv, gq, gk, mq = _proj(u, [w[k] for k in ("sq", "sk", "sv", "gq", "gk", "mq")],
                                   [BF16, F32, F32, BF16, BF16, BF16])
    gv, gr, ga = _proj(u, [w["gv"], w["gr"], w["ga"]], [BF16, BF16, BF16])
    (gates,) = _proj(u, [w["w_gate"]], [BF16], biases=[w["b_gate"]], act="sigmoid")

    r3 = lambda a: a.reshape(b, t, -1)
    if sb_past_k is None:
        o_sb = _sb_attn(r3(sq), r3(sk), r3(sv), heads=sb_heads)
    else:
        o_sb = _sb_decode(r3(sq), r3(sk), r3(sv), sb_past_k, sb_past_v)
    o_gla, s_new = _gla(r3(gq), r3(gk), r3(gv), r3(gr), r3(ga), w["wa"], w["ba"], w["gla_norm_g"], gla_s0,
                        chunk=GLA_CHUNK)
    o_mem = _mem_attn(r3(mq), mem_k, mem_v, heads=MEM_HEADS)

    h2 = _merge(h1, o_sb.reshape(n, -1), o_gla.reshape(n, -1), o_mem.reshape(n, -1), gates,
                w["w_sb_br"], w["w_gla_br"], w["w_mem_br"], w["w_out"], w["mix_post_g"])
    y = _ffn(h2, w["ffn2_pre_g"], w["ffn2_w_gu"], w["ffn2_w_d"], w["ffn2_post_g"])
    return y.reshape(b, t, d), sk, sv, s_new


def kernel(x_prompt, x_sample, mem_prompt, cache_sb_k, cache_sb_v, state_gla, cache_mem_k, cache_mem_v, ffn1_pre_g, ffn1_w_gu, ffn1_w_d, ffn1_post_g, mix_pre_g, w_in, gla_w_a2, gla_b_a2, gla_norm_g, mem_norm_g, w_mem_kv, w_sb_br, w_gla_br, w_mem_br, w_gate, b_gate, w_out, mix_post_g, ffn2_pre_g, ffn2_w_gu, ffn2_w_d, ffn2_post_g):
    params = dict(ffn1_pre_g=ffn1_pre_g, ffn1_w_gu=ffn1_w_gu, ffn1_w_d=ffn1_w_d, ffn1_post_g=ffn1_post_g,
                  mix_pre_g=mix_pre_g, w_in=w_in, gla_w_a2=gla_w_a2, gla_b_a2=gla_b_a2, gla_norm_g=gla_norm_g,
                  w_sb_br=w_sb_br, w_gla_br=w_gla_br, w_mem_br=w_mem_br, w_gate=w_gate, b_gate=b_gate,
                  w_out=w_out, mix_post_g=mix_post_g, ffn2_pre_g=ffn2_pre_g, ffn2_w_gu=ffn2_w_gu,
                  ffn2_w_d=ffn2_w_d, ffn2_post_g=ffn2_post_g)
    depth = w_in.shape[0]
    bp, tp, d = x_prompt.shape
    bs, ts, _ = x_sample.shape
    m = mem_prompt.shape[1]
    mem_w = d // 2
    h_p, h_s = x_prompt, x_sample
    outs = [[] for _ in range(8)]
    for l in range(depth):
        w = _prep_weights({k: v[l] for k, v in params.items()}, d)
        w_mkv = w_mem_kv[l].astype(BF16)
        mk, mv = _proj(mem_prompt.reshape(bp * m, d), [w_mkv[:, :mem_w], w_mkv[:, mem_w:]], [F32, F32],
                       norm_g=mem_norm_g[l].reshape(1, d))
        mk = mk.reshape(bp, m, mem_w)
        mv = mv.reshape(bp, m, mem_w)
        h_p, k_p, v_p, s_p = _layer(h_p, w, mk, mv, None, None, None)
        h_s, k_s, v_s, s_s = _layer(h_s, w, cache_mem_k[l].reshape(bs, m, mem_w), cache_mem_v[l].reshape(bs, m, mem_w),
                                    cache_sb_k[l], cache_sb_v[l], state_gla[l])
        sb_heads = mem_w // SB_HEAD_DIM
        for lst, val in zip(outs, (k_p.reshape(bp, tp, sb_heads, SB_HEAD_DIM), v_p.reshape(bp, tp, sb_heads, SB_HEAD_DIM),
                                   s_p, mk.reshape(bp, m, MEM_HEADS, -1), mv.reshape(bp, m, MEM_HEADS, -1),
                                   k_s.reshape(bs, ts, sb_heads, SB_HEAD_DIM), v_s.reshape(bs, ts, sb_heads, SB_HEAD_DIM),
                                   s_s)):
            lst.append(val)
    return (h_p, h_s) + tuple(jnp.stack(o) for o in outs)
```

```python
import functools

import jax
import jax.numpy as jnp
from jax import lax
from jax.experimental import pallas as pl
from jax.experimental.pallas import tpu as pltpu

F32 = jnp.float32
BF16 = jnp.bfloat16

EPS = 1e-6
LOG2_E = 1.4426950408889634
V7X_LANES = 128
V7X_MXU_DIM = 256
V7X_VMEM_BYTES = 64 * 1024 * 1024
VMEM_LIMIT_BYTES = V7X_VMEM_BYTES - 4 * 1024 * 1024

SB_HEAD_DIM = 128
GLA_HEADS = 4
GLA_LOW_RANK = 16
GLA_GATE_NORM = 16.0
MEM_HEADS = 4
N_BRANCH = 3
GLA_CHUNK = 64
FFN_NORM_ROWS = 128
SB_LOG_WEIGHT_CUTOFF = -110.0


def _cparams(semantics):
    return pltpu.CompilerParams(dimension_semantics=semantics, vmem_limit_bytes=VMEM_LIMIT_BYTES)


def _rms(x, g):
    ms = jnp.mean(x * x, axis=-1, keepdims=True)
    return x * lax.rsqrt(ms + EPS) * g


def _log_sigmoid(x):
    return jnp.minimum(x, 0.0) - jnp.log(1.0 + jnp.exp(-jnp.abs(x)))


def _split_hi_lo(x):
    bits = lax.bitcast_convert_type(x, jnp.uint32) & jnp.uint32(0xFFFF0000)
    hi = lax.bitcast_convert_type(bits, F32)
    return hi.astype(BF16), (x - hi).astype(BF16)


def _pick_tile(n, target):
    t = min(n, target)
    while n % t:
        t //= 2
    return t


def _ffn_kernel(x_ref, pre_g_ref, wg_ref, wu_ref, wd_ref, post_g_ref, *refs, emit_next):
    if emit_next:
        next_g_ref, h_ref, u_ref, xn_sc, acc_sc = refs
    else:
        h_ref, xn_sc, acc_sc = refs
    j = pl.program_id(1)
    tm = x_ref.shape[0]
    rc = min(tm, FFN_NORM_ROWS)

    def for_row_chunks(fn):
        def body(c, carry):
            fn(pl.ds(pl.multiple_of(c * rc, rc), rc))
            return carry
        lax.fori_loop(0, tm // rc, body, 0)

    @pl.when(j == 0)
    def _():
        def prologue(rows):
            xn_sc[rows, :] = _rms(x_ref[rows, :], pre_g_ref[...]).astype(BF16)
            acc_sc[rows, :] = jnp.zeros((rc, acc_sc.shape[1]), F32)
        for_row_chunks(prologue)

    xn = xn_sc[...]
    g = jnp.dot(xn, wg_ref[...], preferred_element_type=F32)
    u = jnp.dot(xn, wu_ref[...], preferred_element_type=F32)
    act = (g * jax.nn.sigmoid(g) * u).astype(BF16)
    acc_sc[...] += jnp.dot(act, wd_ref[...], preferred_element_type=F32)

    @pl.when(j == pl.num_programs(1) - 1)
    def _():
        def epilogue(rows):
            h = x_ref[rows, :] + 0.5 * _rms(acc_sc[rows, :], post_g_ref[...])
            h_ref[rows, :] = h
            if emit_next:
                u_ref[rows, :] = _rms(h, next_g_ref[...]).astype(BF16)
        for_row_chunks(epilogue)


def _ffn(x, pre_g, w_gu, w_d, post_g, next_g=None, *, tm=1024, tf=512):
    emit_next = next_g is not None
    n, d = x.shape
    d_ff = w_d.shape[0]
    tm = _pick_tile(n, tm)
    tf = max(t for t in range(V7X_LANES, tf + 1, V7X_LANES) if d_ff % t == 0)
    nf = d_ff // tf
    row = lambda i, j: (i, 0)
    once = dict(pipeline_mode=pl.Buffered(1))
    out_row = pl.BlockSpec((tm, d), row, **(once if emit_next else {}))
    out_shape = [jax.ShapeDtypeStruct((n, d), F32)]
    out_specs = [out_row]
    in_specs = [
        pl.BlockSpec((tm, d), row, **({} if emit_next else once)),
        _resident((1, d)),
        pl.BlockSpec((d, tf), lambda i, j: (0, j)),
        pl.BlockSpec((d, tf), lambda i, j: (0, nf + j)),
        pl.BlockSpec((tf, d), lambda i, j: (j, 0)),
        _resident((1, d)),
    ]
    args = [x, pre_g, w_gu, w_gu, w_d, post_g]
    if emit_next:
        in_specs.append(_resident((1, d)))
        args.append(next_g)
        out_shape.append(jax.ShapeDtypeStruct((n, d), BF16))
        out_specs.append(out_row)
    outs = pl.pallas_call(
        functools.partial(_ffn_kernel, emit_next=emit_next),
        grid=(n // tm, nf),
        in_specs=in_specs,
        out_specs=out_specs,
        out_shape=out_shape,
        scratch_shapes=[pltpu.VMEM((tm, d), BF16), pltpu.VMEM((tm, d), F32)],
        compiler_params=_cparams(("parallel", "arbitrary")),
        name="ffn",
    )(*args)
    return outs if emit_next else outs[0]


def _proj_kernel(*refs, n_out, has_bias, has_norm, act):
    it = iter(refs)
    x_ref = next(it)
    g_ref = next(it) if has_norm else None
    w_refs = [next(it) for _ in range(n_out)]
    b_refs = [next(it) for _ in range(n_out)] if has_bias else None
    o_refs = [next(it) for _ in range(n_out)]
    x = _rms(x_ref[...], g_ref[...]).astype(BF16) if has_norm else x_ref[...]
    for k in range(n_out):
        acc = jnp.dot(x, w_refs[k][...], preferred_element_type=F32)
        if has_bias:
            acc = acc + b_refs[k][...]
        if act == "sigmoid":
            acc = jax.nn.sigmoid(acc)
        o_refs[k][...] = acc.astype(o_refs[k].dtype)


def _resident(shape):
    return pl.BlockSpec(shape, lambda *_: (0,) * len(shape), pipeline_mode=pl.Buffered(1))


def _proj(x, ws, out_dtypes, *, biases=None, norm_g=None, act=None, tm=512):
    n, kdim = x.shape
    tm = _pick_tile(n, tm)
    n_out = len(ws)
    in_specs = [pl.BlockSpec((tm, kdim), lambda i: (i, 0))]
    args = [x]
    if norm_g is not None:
        in_specs.append(_resident((1, kdim)))
        args.append(norm_g)
    in_specs += [_resident(w.shape) for w in ws]
    args += list(ws)
    if biases is not None:
        in_specs += [_resident(b.shape) for b in biases]
        args += list(biases)
    return pl.pallas_call(
        functools.partial(_proj_kernel, n_out=n_out, has_bias=biases is not None,
                          has_norm=norm_g is not None, act=act),
        grid=(n // tm,),
        in_specs=in_specs,
        out_specs=[pl.BlockSpec((tm, w.shape[1]), lambda i: (i, 0)) for w in ws],
        out_shape=[jax.ShapeDtypeStruct((n, w.shape[1]), dt) for w, dt in zip(ws, out_dtypes)],
        compiler_params=_cparams(("parallel",)),
        name="proj",
    )(*args)


def _suffix_ones(n):
    r = lax.broadcasted_iota(jnp.int32, (n, n), 0)
    c = lax.broadcasted_iota(jnp.int32, (n, n), 1)
    return jnp.where(r >= c, 1.0, 0.0).astype(BF16)


def _sb_tiles(qs, ks, vs, u_mat, psums, *, scale, masked):
    n = len(qs)
    nt_dims = (((1,), (1,)), ((), ()))
    raw = [lax.dot_general(qs[g], ks[g], nt_dims, preferred_element_type=F32) for g in range(n)]
    zs = [r * scale for r in raw]
    sign = jnp.uint32(0x80000000)
    neg_abs = [lax.bitcast_convert_type(lax.bitcast_convert_type(r, jnp.uint32) | sign, F32) for r in raw]
    sps = [jnp.maximum(zs[g], 0.0) + jnp.log(1.0 + jnp.exp2(neg_abs[g] * (scale * LOG2_E))) for g in range(n)]
    if masked:
        rows = lax.broadcasted_iota(jnp.int32, zs[0].shape, 0)
        cols = lax.broadcasted_iota(jnp.int32, zs[0].shape, 1)
        mask = cols < rows
        sps = [jnp.where(mask, s, 0.0) for s in sps]
    splits = [_split_hi_lo(s) for s in sps]
    css = [jnp.dot(hi, u_mat, preferred_element_type=F32) + jnp.dot(lo, u_mat, preferred_element_type=F32)
           for hi, lo in splits]
    weights = [jnp.exp(zs[g] - css[g] - psums[g]) for g in range(n)]
    if masked:
        weights = [jnp.where(mask, a, 0.0) for a in weights]
    contribs = [jnp.dot(weights[g].astype(BF16), vs[g], preferred_element_type=F32) for g in range(n)]
    return [psums[g] + css[g][:, 0:1] for g in range(n)], contribs


def _sb_kernel(q_ref, kn_ref, vn_ref, o_ref, kb_sc, vb_sc, kmax_sc, zb_sc, r_sc, acc_sc, *,
               rg, groups, t_new, scale):
    qi = pl.program_id(2)
    sweep_left = t_new > rg

    @pl.when(qi == 0)
    def _():
        kb_sc[0:rg, :] = jnp.zeros((rg, kb_sc.shape[1]), BF16)
        vb_sc[0:rg, :] = jnp.zeros((rg, vb_sc.shape[1]), BF16)
        ch = min(t_new, 512)

        def conv(i, kmax2):
            src = pl.ds(pl.multiple_of(i * ch, ch), ch)
            dst = pl.ds(pl.multiple_of(rg + i * ch, rg), ch)
            kb = kn_ref[0, src, :].astype(BF16)
            kb_sc[dst, :] = kb
            vb_sc[dst, :] = vn_ref[0, src, :].astype(BF16)
            kf = kb.astype(F32)
            n2 = jnp.sum(kf * kf, axis=1, keepdims=True)
            return jnp.maximum(kmax2, jnp.max(n2, axis=0, keepdims=True))

        kmax2 = lax.fori_loop(0, t_new // ch, conv, jnp.zeros((1, 1), F32))
        kmax_sc[...] = jnp.broadcast_to(kmax2, kmax_sc.shape)

    u_mat = _suffix_ones(rg)
    tiles = functools.partial(_sb_tiles, scale=scale)
    gslice = lambda g: slice(g * rg, (g + 1) * rg)
    qs = [q_ref[0, gslice(g), :] for g in range(groups)]
    first_tile = qi * groups

    def tile_rows(idx):
        return pl.ds(pl.multiple_of((jnp.maximum(idx, -1) + 1) * rg, rg), rg)

    def any_group_continues(idxs, psums):
        go = None
        for g in range(groups):
            go_g = (idxs[g] >= 1) & (jnp.max(zb_sc[gslice(g), :] - psums[g]) > SB_LOG_WEIGHT_CUTOFF)
            go = go_g if go is None else go | go_g
        return go.astype(jnp.int32)

    idxs = [first_tile + g for g in range(groups)]
    psums, contribs = tiles(qs, [kb_sc[tile_rows(i), :] for i in idxs], [vb_sc[tile_rows(i), :] for i in idxs],
                            u_mat, [jnp.zeros((rg, 1), F32)] * groups, masked=True)
    for g in range(groups):
        acc_sc[gslice(g), :] = contribs[g]
        r_sc[gslice(g), :] = psums[g]

    if sweep_left:
        kmax = jnp.sqrt(kmax_sc[0:1, 0:1]) * (scale * 1.01)
        for g in range(groups):
            qf = qs[g].astype(F32)
            zb_sc[gslice(g), :] = jnp.sqrt(jnp.sum(qf * qf, axis=1, keepdims=True)) * kmax

        def body(carry):
            dist, _ = carry
            idxs = [first_tile + g - dist for g in range(groups)]
            psums, contribs = tiles(qs, [kb_sc[tile_rows(i), :] for i in idxs],
                                    [vb_sc[tile_rows(i), :] for i in idxs], u_mat,
                                    [r_sc[gslice(g), :] for g in range(groups)], masked=False)
            for g in range(groups):
                acc_sc[gslice(g), :] += contribs[g]
                r_sc[gslice(g), :] = psums[g]
            return dist + 1, any_group_continues(idxs, psums)

        lax.while_loop(lambda carry: carry[1] != 0, body, (jnp.int32(1), any_group_continues(idxs, psums)))

    o_ref[0] = acc_sc[...].astype(o_ref.dtype)


def _sb_attn(q, k, v, *, heads, rg=V7X_MXU_DIM, groups=4):
    b, t, hd_all = q.shape
    d = hd_all // heads
    rg = _pick_tile(t, rg)
    groups = _pick_tile(t // rg, groups)
    bq = rg * groups
    seq = lambda bi, hi, qi: (bi, 0, hi)
    blk = lambda bi, hi, qi: (bi, qi, hi)
    return pl.pallas_call(
        functools.partial(_sb_kernel, rg=rg, groups=groups, t_new=t, scale=d ** -0.5),
        grid=(b, heads, t // bq),
        in_specs=[pl.BlockSpec((1, bq, d), blk), pl.BlockSpec((1, t, d), seq), pl.BlockSpec((1, t, d), seq)],
        out_specs=pl.BlockSpec((1, bq, d), blk),
        out_shape=jax.ShapeDtypeStruct((b, t, hd_all), BF16),
        scratch_shapes=[pltpu.VMEM((rg + t, d), BF16), pltpu.VMEM((rg + t, d), BF16),
                        pltpu.VMEM((8, V7X_LANES), F32),
                        pltpu.VMEM((bq, 1), F32), pltpu.VMEM((bq, 1), F32),
                        pltpu.VMEM((bq, d), F32)],
        compiler_params=_cparams(("parallel", "parallel", "arbitrary")),
        name="sb_attn",
    )(q, k, v)


def _sb_decode_kernel(q_ref, kn_ref, vn_ref, kp_ref, vp_ref, o_ref, *, heads, bk, scale):
    t = q_ref.shape[1]
    d = q_ref.shape[2] // heads
    past = kp_ref.shape[1] // heads
    hs = range(heads)
    cols = lambda h: slice(h * d, (h + 1) * d)
    tiles = functools.partial(_sb_tiles, scale=scale)
    qs = [q_ref[0, :, cols(h)] for h in hs]
    psums, accs = tiles(qs, [kn_ref[0, :, cols(h)].astype(BF16) for h in hs],
                        [vn_ref[0, :, cols(h)].astype(BF16) for h in hs], _suffix_ones(t),
                        [jnp.zeros((t, 1), F32)] * heads, masked=True)
    u_past = _suffix_ones(bk)
    for i in range(past // bk - 1, -1, -1):
        head_rows = lambda h: pl.ds(i * bk * heads + h, bk, stride=heads)
        psums, contribs = tiles(qs, [kp_ref[0, head_rows(h), :].astype(BF16) for h in hs],
                                [vp_ref[0, head_rows(h), :].astype(BF16) for h in hs], u_past, psums,
                                masked=False)
        accs = [a + c for a, c in zip(accs, contribs)]
    for h in hs:
        o_ref[0, :, cols(h)] = accs[h].astype(o_ref.dtype)


def _sb_decode(q, k_new, v_new, k_cache, v_cache, *, bk=V7X_MXU_DIM):
    b, t, hd_all = q.shape
    _, p, heads, d = k_cache.shape
    bk = _pick_tile(p, bk)
    new = pl.BlockSpec((1, t, hd_all), lambda bi: (bi, 0, 0))
    old = pl.BlockSpec((1, p * heads, d), lambda bi: (bi, 0, 0))
    return pl.pallas_call(
        functools.partial(_sb_decode_kernel, heads=heads, bk=bk, scale=d ** -0.5),
        grid=(b,),
        in_specs=[new, new, new, old, old],
        out_specs=new,
        out_shape=jax.ShapeDtypeStruct((b, t, hd_all), BF16),
        compiler_params=_cparams(("parallel",)),
        name="sb_decode",
    )(q, k_new, v_new, k_cache.reshape(b, p * heads, d), v_cache.reshape(b, p * heads, d))


def _gla_kernel(*refs, chunk, n_chunks, heads, has_s0, dk_scale):
    if has_s0:
        q_ref, k_ref, v_ref, r_ref, ga_ref, wa_ref, ba_ref, ng_ref, s0_ref, o_ref, s_out_ref, s_sc = refs
    else:
        q_ref, k_ref, v_ref, r_ref, ga_ref, wa_ref, ba_ref, ng_ref, o_ref, s_out_ref, s_sc = refs
    tg = pl.program_id(1)

    @pl.when(tg == 0)
    def _():
        s_sc[...] = s0_ref[0] if has_s0 else jnp.zeros_like(s_sc)

    rows = lax.broadcasted_iota(jnp.int32, (chunk, chunk), 0)
    cols = lax.broadcasted_iota(jnp.int32, (chunk, chunk), 1)
    causal = cols <= rows
    tri = jnp.where(causal, 1.0, 0.0).astype(BF16)
    ones_cols = jnp.ones((chunk, V7X_LANES), BF16)
    _, dk, dv = s_sc.shape
    mid = chunk // 2 - 1
    nt_dims = (((1,), (1,)), ((), ()))
    tn_dims = (((0,), (0,)), ((), ()))
    hs = range(heads)
    kcols = lambda h: slice(h * dk, (h + 1) * dk)
    vcols = lambda h: slice(h * dv, (h + 1) * dv)

    def body(c, carry):
        sl = pl.ds(pl.multiple_of(c * chunk, chunk), chunk)
        ga = ga_ref[0, sl, :]
        gs = [_log_sigmoid(jnp.dot(ga, wa_ref[h], preferred_element_type=F32) + ba_ref[h]) / GLA_GATE_NORM
              for h in hs]
        g_split = [_split_hi_lo(g) for g in gs]
        bs = [jnp.dot(tri, hi, preferred_element_type=F32) + jnp.dot(tri, lo, preferred_element_type=F32)
              for hi, lo in g_split]
        b_last_cols = [lax.dot_general(hi, ones_cols, tn_dims, preferred_element_type=F32)
                       + lax.dot_general(lo, ones_cols, tn_dims, preferred_element_type=F32)
                       for hi, lo in g_split]
        qs = [q_ref[0, sl, kcols(h)].astype(F32) * dk_scale for h in hs]
        ks = [k_ref[0, sl, kcols(h)].astype(F32) for h in hs]
        vs = [v_ref[0, sl, vcols(h)] for h in hs]
        s_prev = [s_sc[h] for h in hs]
        q_in = [(qs[h] * jnp.exp(bs[h])).astype(BF16) for h in hs]
        q_m = [(qs[h] * jnp.exp(bs[h] - bs[h][mid:mid + 1, :])).astype(BF16) for h in hs]
        k_m = [(ks[h] * jnp.exp(bs[h][mid:mid + 1, :] - bs[h])).astype(BF16) for h in hs]
        k_st = [(ks[h] * jnp.exp(bs[h][chunk - 1:chunk, :] - bs[h])).astype(BF16) for h in hs]
        o_inter = [jnp.dot(q_in[h], s_prev[h].astype(BF16), preferred_element_type=F32) for h in hs]
        att = [jnp.where(causal, lax.dot_general(q_m[h], k_m[h], nt_dims, preferred_element_type=F32), 0.0)
               for h in hs]
        outs = [o_inter[h] + jnp.dot(att[h].astype(BF16), vs[h], preferred_element_type=F32) for h in hs]
        for h in hs:
            decay = jnp.concatenate([jnp.exp(b_last_cols[h])] * (dv // V7X_LANES), axis=1)
            s_sc[h] = decay * s_prev[h] + lax.dot_general(k_st[h], vs[h], tn_dims, preferred_element_type=F32)
        for h in hs:
            r = r_ref[0, sl, vcols(h)].astype(F32)
            o_ref[0, sl, vcols(h)] = (_rms(outs[h], ng_ref[...]) * (r * jax.nn.sigmoid(r))).astype(o_ref.dtype)
        return carry

    lax.fori_loop(0, n_chunks, body, 0)

    @pl.when(tg == pl.num_programs(1) - 1)
    def _():
        s_out_ref[0] = s_sc[...]


def _gla(q, k, v, r, ga, wa, ba, ng, s0, *, chunk, tg=512):
    b, t, _ = q.shape
    heads, _, dk = wa.shape
    dv = v.shape[-1] // heads
    chunk = min(chunk, t)
    tg = _pick_tile(t, tg)
    assert tg % chunk == 0
    has_s0 = s0 is not None
    tok = lambda bi, ti: (bi, ti, 0)
    in_specs = [
        pl.BlockSpec((1, tg, heads * dk), tok), pl.BlockSpec((1, tg, heads * dk), tok),
        pl.BlockSpec((1, tg, heads * dv), tok), pl.BlockSpec((1, tg, heads * dv), tok),
        pl.BlockSpec((1, tg, V7X_LANES), tok),
        _resident(wa.shape), _resident(ba.shape), _resident(ng.shape),
    ]
    args = [q, k, v, r, ga, wa, ba, ng]
    state_spec = pl.BlockSpec((1, heads, dk, dv), lambda bi, ti: (bi, 0, 0, 0))
    if has_s0:
        in_specs.append(state_spec)
        args.append(s0)
    return pl.pallas_call(
        functools.partial(_gla_kernel, chunk=chunk, n_chunks=tg // chunk, heads=heads, has_s0=has_s0,
                          dk_scale=dk ** -0.5),
        grid=(b, t // tg),
        in_specs=in_specs,
        out_specs=[pl.BlockSpec((1, tg, heads * dv), tok), state_spec],
        out_shape=[jax.ShapeDtypeStruct((b, t, heads * dv), BF16),
                   jax.ShapeDtypeStruct((b, heads, dk, dv), F32)],
        scratch_shapes=[pltpu.VMEM((heads, dk, dv), F32)],
        compiler_params=_cparams(("parallel", "arbitrary")),
        name="gla",
    )(*args)


def _mem_kernel(q_ref, k_ref, v_ref, o_ref, *, heads, scale):
    hd = q_ref.shape[-1] // heads
    for h in range(heads):
        cs = slice(h * hd, (h + 1) * hd)
        q = q_ref[0, :, cs]
        k = k_ref[0, :, cs].astype(BF16)
        v = v_ref[0, :, cs].astype(BF16)
        s = lax.dot_general(q, k, (((1,), (1,)), ((), ())), preferred_element_type=F32) * scale
        e = jnp.exp(s - jnp.max(s, axis=-1, keepdims=True))
        p = e / jnp.sum(e, axis=-1, keepdims=True)
        o_ref[0, :, cs] = jnp.dot(p.astype(BF16), v, preferred_element_type=F32).astype(o_ref.dtype)


def _mem_attn(q, mk, mv, *, heads, tq=1024):
    b, t, w = q.shape
    m = mk.shape[1]
    tq = _pick_tile(t, tq)
    mem = lambda bi, ti: (bi, 0, 0)
    return pl.pallas_call(
        functools.partial(_mem_kernel, heads=heads, scale=(w // heads) ** -0.5),
        grid=(b, t // tq),
        in_specs=[pl.BlockSpec((1, tq, w), lambda bi, ti: (bi, ti, 0)),
                  pl.BlockSpec((1, m, w), mem), pl.BlockSpec((1, m, w), mem)],
        out_specs=pl.BlockSpec((1, tq, w), lambda bi, ti: (bi, ti, 0)),
        out_shape=jax.ShapeDtypeStruct((b, t, w), BF16),
        compiler_params=_cparams(("parallel", "arbitrary")),
        name="mem_attn",
    )(q, mk, mv)


def _merge_kernel(h_ref, osb_ref, ogla_ref, omem_ref, gates_ref, wsb_ref, wgla_ref, wmem_ref, wout_ref,
                  post_g_ref, h2_ref):
    d = h_ref.shape[1]
    branches = ((osb_ref, wsb_ref), (ogla_ref, wgla_ref), (omem_ref, wmem_ref))
    merged = None
    for k, (o_ref, w_ref) in enumerate(branches):
        term = gates_ref[:, k * d:(k + 1) * d].astype(F32) * jnp.dot(o_ref[...], w_ref[...],
                                                                    preferred_element_type=F32)
        merged = term if merged is None else merged + term
    m = jnp.dot(merged.astype(BF16), wout_ref[...], preferred_element_type=F32)
    h2_ref[...] = h_ref[...] + _rms(m, post_g_ref[...])


def _merge(h, o_sb, o_gla, o_mem, gates, w_sb, w_gla, w_mem, w_out, post_g, *, tm=256):
    n, d = h.shape
    tm = _pick_tile(n, tm)
    row = lambda a: pl.BlockSpec((tm, a.shape[1]), lambda i: (i, 0))
    return pl.pallas_call(
        _merge_kernel,
        grid=(n // tm,),
        in_specs=[row(h), row(o_sb), row(o_gla), row(o_mem), row(gates),
                  _resident(w_sb.shape), _resident(w_gla.shape), _resident(w_mem.shape),
                  _resident(w_out.shape), _resident(post_g.shape)],
        out_specs=row(h),
        out_shape=jax.ShapeDtypeStruct((n, d), F32),
        compiler_params=_cparams(("parallel",)),
        name="merge",
    )(h, o_sb, o_gla, o_mem, gates, w_sb, w_gla, w_mem, w_out, post_g)


def _prep_weights(p, d):
    sb_w = d // 2
    gla_kw = d // 2
    gla_vw = d
    mem_w = d // 2
    w_in = p["w_in"]
    c = 0
    pieces = {}
    for name, width in (("sq", sb_w), ("sk", sb_w), ("sv", sb_w), ("gq", gla_kw), ("gk", gla_kw),
                        ("gv", gla_vw), ("gr", gla_vw), ("ga", GLA_LOW_RANK), ("mq", mem_w)):
        pieces[name] = w_in[:, c:c + width].astype(BF16)
        c += width
    assert c == w_in.shape[1]
    pad = V7X_LANES - GLA_LOW_RANK
    dk = gla_kw // GLA_HEADS
    w = dict(pieces)
    w["ga"] = jnp.pad(pieces["ga"], ((0, 0), (0, pad)))
    wa = jnp.pad(p["gla_w_a2"].astype(BF16), ((0, pad), (0, 0)))
    w["wa"] = wa.reshape(V7X_LANES, GLA_HEADS, dk).transpose(1, 0, 2)
    w["ba"] = p["gla_b_a2"].reshape(GLA_HEADS, 1, dk)
    for name in ("ffn1_w_gu", "ffn1_w_d", "ffn2_w_gu", "ffn2_w_d", "w_sb_br", "w_gla_br", "w_mem_br",
                 "w_gate", "w_out"):
        w[name] = p[name].astype(BF16)
    for name in ("ffn1_pre_g", "ffn1_post_g", "mix_pre_g", "mix_post_g", "ffn2_pre_g", "ffn2_post_g",
                 "gla_norm_g", "b_gate"):
        w[name] = p[name].reshape(1, -1)
    return w


def _layer(x, w, mem_k, mem_v, sb_past_k, sb_past_v, gla_s0):
    b, t, d = x.shape
    n = b * t
    sb_heads = (d // 2) // SB_HEAD_DIM
    h1, u = _ffn(x.reshape(n, d), w["ffn1_pre_g"], w["ffn1_w_gu"], w["ffn1_w_d"], w["ffn1_post_g"],
                 next_g=w["mix_pre_g"])
    sq, sk, sv, gq, gk, mq = _proj(u, [w[k] for k in ("sq", "sk", "sv", "gq", "gk", "mq")],
                                   [BF16, F32, F32, BF16, BF16, BF16])
    gv, gr, ga = _proj(u, [w["gv"], w["gr"], w["ga"]], [BF16, BF16, BF16])
    (gates,) = _proj(u, [w["w_gate"]], [BF16], biases=[w["b_gate"]], act="sigmoid")

    r3 = lambda a: a.reshape(b, t, -1)
    if sb_past_k is None:
        o_sb = _sb_attn(r3(sq), r3(sk), r3(sv), heads=sb_heads)
    else:
        o_sb = _sb_decode(r3(sq), r3(sk), r3(sv), sb_past_k, sb_past_v)
    o_gla, s_new = _gla(r3(gq), r3(gk), r3(gv), r3(gr), r3(ga), w["wa"], w["ba"], w["gla_norm_g"], gla_s0,
                        chunk=GLA_CHUNK)
    o_mem = _mem_attn(r3(mq), mem_k, mem_v, heads=MEM_HEADS)

    h2 = _merge(h1, o_sb.reshape(n, -1), o_gla.reshape(n, -1), o_mem.reshape(n, -1), gates,
                w["w_sb_br"], w["w_gla_br"], w["w_mem_br"], w["w_out"], w["mix_post_g"])
    y = _ffn(h2, w["ffn2_pre_g"], w["ffn2_w_gu"], w["ffn2_w_d"], w["ffn2_post_g"])
    return y.reshape(b, t, d), sk, sv, s_new


def kernel(x_prompt, x_sample, mem_prompt, cache_sb_k, cache_sb_v, state_gla, cache_mem_k, cache_mem_v, ffn1_pre_g, ffn1_w_gu, ffn1_w_d, ffn1_post_g, mix_pre_g, w_in, gla_w_a2, gla_b_a2, gla_norm_g, mem_norm_g, w_mem_kv, w_sb_br, w_gla_br, w_mem_br, w_gate, b_gate, w_out, mix_post_g, ffn2_pre_g, ffn2_w_gu, ffn2_w_d, ffn2_post_g):
    params = dict(ffn1_pre_g=ffn1_pre_g, ffn1_w_gu=ffn1_w_gu, ffn1_w_d=ffn1_w_d, ffn1_post_g=ffn1_post_g,
                  mix_pre_g=mix_pre_g, w_in=w_in, gla_w_a2=gla_w_a2, gla_b_a2=gla_b_a2, gla_norm_g=gla_norm_g,
                  w_sb_br=w_sb_br, w_gla_br=w_gla_br, w_mem_br=w_mem_br, w_gate=w_gate, b_gate=b_gate,
                  w_out=w_out, mix_post_g=mix_post_g, ffn2_pre_g=ffn2_pre_g, ffn2_w_gu=ffn2_w_gu,
                  ffn2_w_d=ffn2_w_d, ffn2_post_g=ffn2_post_g)
    depth = w_in.shape[0]
    bp, tp, d = x_prompt.shape
    bs, ts, _ = x_sample.shape
    m = mem_prompt.shape[1]
    mem_w = d // 2
    h_p, h_s = x_prompt, x_sample
    outs = [[] for _ in range(8)]
    for l in range(depth):
        w = _prep_weights({k: v[l] for k, v in params.items()}, d)
        w_mkv = w_mem_kv[l].astype(BF16)
        mk, mv = _proj(mem_prompt.reshape(bp * m, d), [w_mkv[:, :mem_w], w_mkv[:, mem_w:]], [F32, F32],
                       norm_g=mem_norm_g[l].reshape(1, d))
        mk = mk.reshape(bp, m, mem_w)
        mv = mv.reshape(bp, m, mem_w)
        h_p, k_p, v_p, s_p = _layer(h_p, w, mk, mv, None, None, None)
        h_s, k_s, v_s, s_s = _layer(h_s, w, cache_mem_k[l].reshape(bs, m, mem_w), cache_mem_v[l].reshape(bs, m, mem_w),
                                    cache_sb_k[l], cache_sb_v[l], state_gla[l])
        sb_heads = mem_w // SB_HEAD_DIM
        for lst, val in zip(outs, (k_p.reshape(bp, tp, sb_heads, SB_HEAD_DIM), v_p.reshape(bp, tp, sb_heads, SB_HEAD_DIM),
                                   s_p, mk.reshape(bp, m, MEM_HEADS, -1), mv.reshape(bp, m, MEM_HEADS, -1),
                                   k_s.reshape(bs, ts, sb_heads, SB_HEAD_DIM), v_s.reshape(bs, ts, sb_heads, SB_HEAD_DIM),
                                   s_s)):
            lst.append(val)
    return (h_p, h_s) + tuple(jnp.stack(o) for o in outs)
```

```python
import functools

import jax
import jax.numpy as jnp
from jax import lax
from jax.experimental import pallas as pl
from jax.experimental.pallas import tpu as pltpu

F32 = jnp.float32
BF16 = jnp.bfloat16

EPS = 1e-6
LOG2_E = 1.4426950408889634
V7X_LANES = 128
V7X_MXU_DIM = 256
V7X_VMEM_BYTES = 64 * 1024 * 1024
VMEM_LIMIT_BYTES = V7X_VMEM_BYTES - 4 * 1024 * 1024

SB_HEAD_DIM = 128
GLA_HEADS = 4
GLA_LOW_RANK = 16
GLA_GATE_NORM = 16.0
MEM_HEADS = 4
N_BRANCH = 3
GLA_CHUNK = 64
FFN_NORM_ROWS = 128
SB_LOG_WEIGHT_CUTOFF = -110.0


def _cparams(semantics):
    return pltpu.CompilerParams(dimension_semantics=semantics, vmem_limit_bytes=VMEM_LIMIT_BYTES)


def _rms(x, g):
    ms = jnp.mean(x * x, axis=-1, keepdims=True)
    return x * lax.rsqrt(ms + EPS) * g


def _log_sigmoid(x):
    return jnp.minimum(x, 0.0) - jnp.log(1.0 + jnp.exp(-jnp.abs(x)))


def _split_hi_lo(x):
    hi = x.astype(BF16)
    lo = (x - hi.astype(F32)).astype(BF16)
    return hi, lo


def _pick_tile(n, target):
    t = min(n, target)
    while n % t:
        t //= 2
    return t


def _ffn_kernel(x_ref, pre_g_ref, wg_ref, wu_ref, wd_ref, post_g_ref, *refs, emit_next):
    if emit_next:
        next_g_ref, h_ref, u_ref, xn_sc = refs
    else:
        h_ref, xn_sc = refs
    j = pl.program_id(1)
    tm = x_ref.shape[0]
    rc = min(tm, FFN_NORM_ROWS)

    def for_row_chunks(fn):
        def body(c, carry):
            fn(pl.ds(pl.multiple_of(c * rc, rc), rc))
            return carry
        lax.fori_loop(0, tm // rc, body, 0)

    @pl.when(j == 0)
    def _():
        def prologue(rows):
            xn_sc[rows, :] = _rms(x_ref[rows, :], pre_g_ref[...]).astype(BF16)
            h_ref[rows, :] = jnp.zeros((rc, h_ref.shape[1]), F32)
        for_row_chunks(prologue)

    xn = xn_sc[...]
    g = jnp.dot(xn, wg_ref[...], preferred_element_type=F32)
    u = jnp.dot(xn, wu_ref[...], preferred_element_type=F32)
    act = (g * jax.nn.sigmoid(g) * u).astype(BF16)
    h_ref[...] += jnp.dot(act, wd_ref[...], preferred_element_type=F32)

    @pl.when(j == pl.num_programs(1) - 1)
    def _():
        def epilogue(rows):
            h = x_ref[rows, :] + 0.5 * _rms(h_ref[rows, :], post_g_ref[...])
            h_ref[rows, :] = h
            if emit_next:
                u_ref[rows, :] = _rms(h, next_g_ref[...]).astype(BF16)
        for_row_chunks(epilogue)


def _ffn(x, pre_g, w_gu, w_d, post_g, next_g=None, *, tm=1024, tf=512):
    emit_next = next_g is not None
    n, d = x.shape
    d_ff = w_d.shape[0]
    tm = _pick_tile(n, tm)
    tf = max(t for t in range(V7X_LANES, tf + 1, V7X_LANES) if d_ff % t == 0)
    nf = d_ff // tf
    row = lambda i, j: (i, 0)
    out_shape = [jax.ShapeDtypeStruct((n, d), F32)]
    out_specs = [pl.BlockSpec((tm, d), row)]
    in_specs = [
        pl.BlockSpec((tm, d), row),
        _resident((1, d)),
        pl.BlockSpec((d, tf), lambda i, j: (0, j)),
        pl.BlockSpec((d, tf), lambda i, j: (0, nf + j)),
        pl.BlockSpec((tf, d), lambda i, j: (j, 0)),
        _resident((1, d)),
    ]
    args = [x, pre_g, w_gu, w_gu, w_d, post_g]
    if emit_next:
        in_specs.append(_resident((1, d)))
        args.append(next_g)
        out_shape.append(jax.ShapeDtypeStruct((n, d), BF16))
        out_specs.append(pl.BlockSpec((tm, d), row, pipeline_mode=pl.Buffered(1)))
    outs = pl.pallas_call(
        functools.partial(_ffn_kernel, emit_next=emit_next),
        grid=(n // tm, nf),
        in_specs=in_specs,
        out_specs=out_specs,
        out_shape=out_shape,
        scratch_shapes=[pltpu.VMEM((tm, d), BF16)],
        compiler_params=_cparams(("parallel", "arbitrary")),
        name="ffn",
    )(*args)
    return outs if emit_next else outs[0]


def _proj_kernel(*refs, n_out, has_bias, has_norm, act):
    it = iter(refs)
    x_ref = next(it)
    g_ref = next(it) if has_norm else None
    w_refs = [next(it) for _ in range(n_out)]
    b_refs = [next(it) for _ in range(n_out)] if has_bias else None
    o_refs = [next(it) for _ in range(n_out)]
    x = _rms(x_ref[...], g_ref[...]).astype(BF16) if has_norm else x_ref[...]
    for k in range(n_out):
        acc = jnp.dot(x, w_refs[k][...], preferred_element_type=F32)
        if has_bias:
            acc = acc + b_refs[k][...]
        if act == "sigmoid":
            acc = jax.nn.sigmoid(acc)
        o_refs[k][...] = acc.astype(o_refs[k].dtype)


def _resident(shape):
    return pl.BlockSpec(shape, lambda *_: (0,) * len(shape), pipeline_mode=pl.Buffered(1))


def _proj(x, ws, out_dtypes, *, biases=None, norm_g=None, act=None, tm=512):
    n, kdim = x.shape
    tm = _pick_tile(n, tm)
    n_out = len(ws)
    in_specs = [pl.BlockSpec((tm, kdim), lambda i: (i, 0))]
    args = [x]
    if norm_g is not None:
        in_specs.append(_resident((1, kdim)))
        args.append(norm_g)
    in_specs += [_resident(w.shape) for w in ws]
    args += list(ws)
    if biases is not None:
        in_specs += [_resident(b.shape) for b in biases]
        args += list(biases)
    return pl.pallas_call(
        functools.partial(_proj_kernel, n_out=n_out, has_bias=biases is not None,
                          has_norm=norm_g is not None, act=act),
        grid=(n // tm,),
        in_specs=in_specs,
        out_specs=[pl.BlockSpec((tm, w.shape[1]), lambda i: (i, 0)) for w in ws],
        out_shape=[jax.ShapeDtypeStruct((n, w.shape[1]), dt) for w, dt in zip(ws, out_dtypes)],
        compiler_params=_cparams(("parallel",)),
        name="proj",
    )(*args)


def _suffix_ones(n):
    r = lax.broadcasted_iota(jnp.int32, (n, n), 0)
    c = lax.broadcasted_iota(jnp.int32, (n, n), 1)
    return jnp.where(r >= c, 1.0, 0.0).astype(BF16)


def _sb_tiles(qs, ks, vs, u_mat, psums, *, scale, masked):
    n = len(qs)
    nt_dims = (((1,), (1,)), ((), ()))
    raw = [lax.dot_general(qs[g], ks[g], nt_dims, preferred_element_type=F32) for g in range(n)]
    zs = [r * scale for r in raw]
    sign = jnp.uint32(0x80000000)
    neg_abs = [lax.bitcast_convert_type(lax.bitcast_convert_type(r, jnp.uint32) | sign, F32) for r in raw]
    sps = [jnp.maximum(zs[g], 0.0) + jnp.log(1.0 + jnp.exp2(neg_abs[g] * (scale * LOG2_E))) for g in range(n)]
    if masked:
        rows = lax.broadcasted_iota(jnp.int32, zs[0].shape, 0)
        cols = lax.broadcasted_iota(jnp.int32, zs[0].shape, 1)
        mask = cols < rows
        sps = [jnp.where(mask, s, 0.0) for s in sps]
    splits = [_split_hi_lo(s) for s in sps]
    css = [jnp.dot(hi, u_mat, preferred_element_type=F32) + jnp.dot(lo, u_mat, preferred_element_type=F32)
           for hi, lo in splits]
    weights = [jnp.exp(zs[g] - css[g] - psums[g]) for g in range(n)]
    if masked:
        weights = [jnp.where(mask, a, 0.0) for a in weights]
    contribs = [jnp.dot(weights[g].astype(BF16), vs[g], preferred_element_type=F32) for g in range(n)]
    return [psums[g] + css[g][:, 0:1] for g in range(n)], contribs


def _sb_kernel(q_ref, kn_ref, vn_ref, o_ref, kb_sc, vb_sc, kmax_sc, zb_sc, r_sc, acc_sc, *,
               rg, groups, t_new, scale):
    qi = pl.program_id(2)
    sweep_left = t_new > rg

    @pl.when(qi == 0)
    def _():
        kb_sc[0:rg, :] = jnp.zeros((rg, kb_sc.shape[1]), BF16)
        vb_sc[0:rg, :] = jnp.zeros((rg, vb_sc.shape[1]), BF16)
        ch = min(t_new, 512)

        def conv(i, kmax2):
            src = pl.ds(pl.multiple_of(i * ch, ch), ch)
            dst = pl.ds(pl.multiple_of(rg + i * ch, rg), ch)
            kb = kn_ref[0, src, :].astype(BF16)
            kb_sc[dst, :] = kb
            vb_sc[dst, :] = vn_ref[0, src, :].astype(BF16)
            kf = kb.astype(F32)
            n2 = jnp.sum(kf * kf, axis=1, keepdims=True)
            return jnp.maximum(kmax2, jnp.max(n2, axis=0, keepdims=True))

        kmax2 = lax.fori_loop(0, t_new // ch, conv, jnp.zeros((1, 1), F32))
        kmax_sc[...] = jnp.broadcast_to(kmax2, kmax_sc.shape)

    u_mat = _suffix_ones(rg)
    tiles = functools.partial(_sb_tiles, scale=scale)
    gslice = lambda g: slice(g * rg, (g + 1) * rg)
    qs = [q_ref[0, gslice(g), :] for g in range(groups)]
    first_tile = qi * groups

    def tile_rows(idx):
        return pl.ds(pl.multiple_of((jnp.maximum(idx, -1) + 1) * rg, rg), rg)

    def any_group_continues(idxs, psums):
        go = None
        for g in range(groups):
            go_g = (idxs[g] >= 1) & (jnp.max(zb_sc[gslice(g), :] - psums[g]) > SB_LOG_WEIGHT_CUTOFF)
            go = go_g if go is None else go | go_g
        return go.astype(jnp.int32)

    idxs = [first_tile + g for g in range(groups)]
    psums, contribs = tiles(qs, [kb_sc[tile_rows(i), :] for i in idxs], [vb_sc[tile_rows(i), :] for i in idxs],
                            u_mat, [jnp.zeros((rg, 1), F32)] * groups, masked=True)
    for g in range(groups):
        acc_sc[gslice(g), :] = contribs[g]
        r_sc[gslice(g), :] = psums[g]

    if sweep_left:
        kmax = jnp.sqrt(kmax_sc[0:1, 0:1]) * (scale * 1.01)
        for g in range(groups):
            qf = qs[g].astype(F32)
            zb_sc[gslice(g), :] = jnp.sqrt(jnp.sum(qf * qf, axis=1, keepdims=True)) * kmax

        def body(carry):
            dist, _ = carry
            idxs = [first_tile + g - dist for g in range(groups)]
            psums, contribs = tiles(qs, [kb_sc[tile_rows(i), :] for i in idxs],
                                    [vb_sc[tile_rows(i), :] for i in idxs], u_mat,
                                    [r_sc[gslice(g), :] for g in range(groups)], masked=False)
            for g in range(groups):
                acc_sc[gslice(g), :] += contribs[g]
                r_sc[gslice(g), :] = psums[g]
            return dist + 1, any_group_continues(idxs, psums)

        lax.while_loop(lambda carry: carry[1] != 0, body, (jnp.int32(1), any_group_continues(idxs, psums)))

    o_ref[0] = acc_sc[...].astype(o_ref.dtype)


def _sb_attn(q, k, v, *, heads, rg=V7X_MXU_DIM, groups=4):
    b, t, hd_all = q.shape
    d = hd_all // heads
    rg = _pick_tile(t, rg)
    groups = _pick_tile(t // rg, groups)
    bq = rg * groups
    seq = lambda bi, hi, qi: (bi, 0, hi)
    blk = lambda bi, hi, qi: (bi, qi, hi)
    return pl.pallas_call(
        functools.partial(_sb_kernel, rg=rg, groups=groups, t_new=t, scale=d ** -0.5),
        grid=(b, heads, t // bq),
        in_specs=[pl.BlockSpec((1, bq, d), blk), pl.BlockSpec((1, t, d), seq), pl.BlockSpec((1, t, d), seq)],
        out_specs=pl.BlockSpec((1, bq, d), blk),
        out_shape=jax.ShapeDtypeStruct((b, t, hd_all), BF16),
        scratch_shapes=[pltpu.VMEM((rg + t, d), BF16), pltpu.VMEM((rg + t, d), BF16),
                        pltpu.VMEM((8, V7X_LANES), F32),
                        pltpu.VMEM((bq, 1), F32), pltpu.VMEM((bq, 1), F32),
                        pltpu.VMEM((bq, d), F32)],
        compiler_params=_cparams(("parallel", "parallel", "arbitrary")),
        name="sb_attn",
    )(q, k, v)


def _sb_decode_kernel(q_ref, kn_ref, vn_ref, kp_ref, vp_ref, o_ref, *, heads, bk, scale):
    t = q_ref.shape[1]
    d = q_ref.shape[2] // heads
    past = kp_ref.shape[1] // heads
    hs = range(heads)
    cols = lambda h: slice(h * d, (h + 1) * d)
    tiles = functools.partial(_sb_tiles, scale=scale)
    qs = [q_ref[0, :, cols(h)] for h in hs]
    psums, accs = tiles(qs, [kn_ref[0, :, cols(h)].astype(BF16) for h in hs],
                        [vn_ref[0, :, cols(h)].astype(BF16) for h in hs], _suffix_ones(t),
                        [jnp.zeros((t, 1), F32)] * heads, masked=True)
    u_past = _suffix_ones(bk)
    for i in range(past // bk - 1, -1, -1):
        head_rows = lambda h: pl.ds(i * bk * heads + h, bk, stride=heads)
        psums, contribs = tiles(qs, [kp_ref[0, head_rows(h), :].astype(BF16) for h in hs],
                                [vp_ref[0, head_rows(h), :].astype(BF16) for h in hs], u_past, psums,
                                masked=False)
        accs = [a + c for a, c in zip(accs, contribs)]
    for h in hs:
        o_ref[0, :, cols(h)] = accs[h].astype(o_ref.dtype)


def _sb_decode(q, k_new, v_new, k_cache, v_cache, *, bk=V7X_MXU_DIM):
    b, t, hd_all = q.shape
    _, p, heads, d = k_cache.shape
    bk = _pick_tile(p, bk)
    new = pl.BlockSpec((1, t, hd_all), lambda bi: (bi, 0, 0))
    old = pl.BlockSpec((1, p * heads, d), lambda bi: (bi, 0, 0))
    return pl.pallas_call(
        functools.partial(_sb_decode_kernel, heads=heads, bk=bk, scale=d ** -0.5),
        grid=(b,),
        in_specs=[new, new, new, old, old],
        out_specs=new,
        out_shape=jax.ShapeDtypeStruct((b, t, hd_all), BF16),
        compiler_params=_cparams(("parallel",)),
        name="sb_decode",
    )(q, k_new, v_new, k_cache.reshape(b, p * heads, d), v_cache.reshape(b, p * heads, d))


def _gla_kernel(*refs, chunk, n_chunks, heads, has_s0, dk_scale):
    if has_s0:
        q_ref, k_ref, v_ref, r_ref, ga_ref, wa_ref, ba_ref, ng_ref, s0_ref, o_ref, s_out_ref, s_sc = refs
    else:
        q_ref, k_ref, v_ref, r_ref, ga_ref, wa_ref, ba_ref, ng_ref, o_ref, s_out_ref, s_sc = refs
    tg = pl.program_id(1)

    @pl.when(tg == 0)
    def _():
        s_sc[...] = s0_ref[0] if has_s0 else jnp.zeros_like(s_sc)

    rows = lax.broadcasted_iota(jnp.int32, (chunk, chunk), 0)
    cols = lax.broadcasted_iota(jnp.int32, (chunk, chunk), 1)
    causal = cols <= rows
    tri = jnp.where(causal, 1.0, 0.0).astype(BF16)
    ones_cols = jnp.ones((chunk, V7X_LANES), BF16)
    _, dk, dv = s_sc.shape
    mid = chunk // 2 - 1
    nt_dims = (((1,), (1,)), ((), ()))
    tn_dims = (((0,), (0,)), ((), ()))
    hs = range(heads)
    kcols = lambda h: slice(h * dk, (h + 1) * dk)
    vcols = lambda h: slice(h * dv, (h + 1) * dv)

    def body(c, carry):
        sl = pl.ds(pl.multiple_of(c * chunk, chunk), chunk)
        ga = ga_ref[0, sl, :]
        gs = [_log_sigmoid(jnp.dot(ga, wa_ref[h], preferred_element_type=F32) + ba_ref[h]) / GLA_GATE_NORM
              for h in hs]
        g_split = [_split_hi_lo(g) for g in gs]
        bs = [jnp.dot(tri, hi, preferred_element_type=F32) + jnp.dot(tri, lo, preferred_element_type=F32)
              for hi, lo in g_split]
        b_last_cols = [lax.dot_general(hi, ones_cols, tn_dims, preferred_element_type=F32)
                       + lax.dot_general(lo, ones_cols, tn_dims, preferred_element_type=F32)
                       for hi, lo in g_split]
        qs = [q_ref[0, sl, kcols(h)].astype(F32) * dk_scale for h in hs]
        ks = [k_ref[0, sl, kcols(h)].astype(F32) for h in hs]
        vs = [v_ref[0, sl, vcols(h)] for h in hs]
        s_prev = [s_sc[h] for h in hs]
        q_in = [(qs[h] * jnp.exp(bs[h])).astype(BF16) for h in hs]
        q_m = [(qs[h] * jnp.exp(bs[h] - bs[h][mid:mid + 1, :])).astype(BF16) for h in hs]
        k_m = [(ks[h] * jnp.exp(bs[h][mid:mid + 1, :] - bs[h])).astype(BF16) for h in hs]
        k_st = [(ks[h] * jnp.exp(bs[h][chunk - 1:chunk, :] - bs[h])).astype(BF16) for h in hs]
        o_inter = [jnp.dot(q_in[h], s_prev[h].astype(BF16), preferred_element_type=F32) for h in hs]
        att = [jnp.where(causal, lax.dot_general(q_m[h], k_m[h], nt_dims, preferred_element_type=F32), 0.0)
               for h in hs]
        outs = [o_inter[h] + jnp.dot(att[h].astype(BF16), vs[h], preferred_element_type=F32) for h in hs]
        for h in hs:
            decay = jnp.concatenate([jnp.exp(b_last_cols[h])] * (dv // V7X_LANES), axis=1)
            s_sc[h] = decay * s_prev[h] + lax.dot_general(k_st[h], vs[h], tn_dims, preferred_element_type=F32)
        for h in hs:
            r = r_ref[0, sl, vcols(h)].astype(F32)
            o_ref[0, sl, vcols(h)] = (_rms(outs[h], ng_ref[...]) * (r * jax.nn.sigmoid(r))).astype(o_ref.dtype)
        return carry

    lax.fori_loop(0, n_chunks, body, 0)

    @pl.when(tg == pl.num_programs(1) - 1)
    def _():
        s_out_ref[0] = s_sc[...]


def _gla(q, k, v, r, ga, wa, ba, ng, s0, *, chunk, tg=512):
    b, t, _ = q.shape
    heads, _, dk = wa.shape
    dv = v.shape[-1] // heads
    chunk = min(chunk, t)
    tg = _pick_tile(t, tg)
    assert tg % chunk == 0
    has_s0 = s0 is not None
    tok = lambda bi, ti: (bi, ti, 0)
    in_specs = [
        pl.BlockSpec((1, tg, heads * dk), tok), pl.BlockSpec((1, tg, heads * dk), tok),
        pl.BlockSpec((1, tg, heads * dv), tok), pl.BlockSpec((1, tg, heads * dv), tok),
        pl.BlockSpec((1, tg, V7X_LANES), tok),
        _resident(wa.shape), _resident(ba.shape), _resident(ng.shape),
    ]
    args = [q, k, v, r, ga, wa, ba, ng]
    state_spec = pl.BlockSpec((1, heads, dk, dv), lambda bi, ti: (bi, 0, 0, 0))
    if has_s0:
        in_specs.append(state_spec)
        args.append(s0)
    return pl.pallas_call(
        functools.partial(_gla_kernel, chunk=chunk, n_chunks=tg // chunk, heads=heads, has_s0=has_s0,
                          dk_scale=dk ** -0.5),
        grid=(b, t // tg),
        in_specs=in_specs,
        out_specs=[pl.BlockSpec((1, tg, heads * dv), tok), state_spec],
        out_shape=[jax.ShapeDtypeStruct((b, t, heads * dv), BF16),
                   jax.ShapeDtypeStruct((b, heads, dk, dv), F32)],
        scratch_shapes=[pltpu.VMEM((heads, dk, dv), F32)],
        compiler_params=_cparams(("parallel", "arbitrary")),
        name="gla",
    )(*args)


def _mem_kernel(q_ref, k_ref, v_ref, o_ref, *, heads, scale):
    hd = q_ref.shape[-1] // heads
    for h in range(heads):
        cs = slice(h * hd, (h + 1) * hd)
        q = q_ref[0, :, cs]
        k = k_ref[0, :, cs].astype(BF16)
        v = v_ref[0, :, cs].astype(BF16)
        s = lax.dot_general(q, k, (((1,), (1,)), ((), ())), preferred_element_type=F32) * scale
        e = jnp.exp(s - jnp.max(s, axis=-1, keepdims=True))
        p = e / jnp.sum(e, axis=-1, keepdims=True)
        o_ref[0, :, cs] = jnp.dot(p.astype(BF16), v, preferred_element_type=F32).astype(o_ref.dtype)


def _mem_attn(q, mk, mv, *, heads, tq=1024):
    b, t, w = q.shape
    m = mk.shape[1]
    tq = _pick_tile(t, tq)
    mem = lambda bi, ti: (bi, 0, 0)
    return pl.pallas_call(
        functools.partial(_mem_kernel, heads=heads, scale=(w // heads) ** -0.5),
        grid=(b, t // tq),
        in_specs=[pl.BlockSpec((1, tq, w), lambda bi, ti: (bi, ti, 0)),
                  pl.BlockSpec((1, m, w), mem), pl.BlockSpec((1, m, w), mem)],
        out_specs=pl.BlockSpec((1, tq, w), lambda bi, ti: (bi, ti, 0)),
        out_shape=jax.ShapeDtypeStruct((b, t, w), BF16),
        compiler_params=_cparams(("parallel", "arbitrary")),
        name="mem_attn",
    )(q, mk, mv)


def _merge_kernel(h_ref, osb_ref, ogla_ref, omem_ref, gates_ref, wsb_ref, wgla_ref, wmem_ref, wout_ref,
                  post_g_ref, h2_ref):
    d = h_ref.shape[1]
    branches = ((osb_ref, wsb_ref), (ogla_ref, wgla_ref), (omem_ref, wmem_ref))
    merged = None
    for k, (o_ref, w_ref) in enumerate(branches):
        term = gates_ref[:, k * d:(k + 1) * d].astype(F32) * jnp.dot(o_ref[...], w_ref[...],
                                                                    preferred_element_type=F32)
        merged = term if merged is None else merged + term
    m = jnp.dot(merged.astype(BF16), wout_ref[...], preferred_element_type=F32)
    h2_ref[...] = h_ref[...] + _rms(m, post_g_ref[...])


def _merge(h, o_sb, o_gla, o_mem, gates, w_sb, w_gla, w_mem, w_out, post_g, *, tm=256):
    n, d = h.shape
    tm = _pick_tile(n, tm)
    row = lambda a: pl.BlockSpec((tm, a.shape[1]), lambda i: (i, 0))
    return pl.pallas_call(
        _merge_kernel,
        grid=(n // tm,),
        in_specs=[row(h), row(o_sb), row(o_gla), row(o_mem), row(gates),
                  _resident(w_sb.shape), _resident(w_gla.shape), _resident(w_mem.shape),
                  _resident(w_out.shape), _resident(post_g.shape)],
        out_specs=row(h),
        out_shape=jax.ShapeDtypeStruct((n, d), F32),
        compiler_params=_cparams(("parallel",)),
        name="merge",
    )(h, o_sb, o_gla, o_mem, gates, w_sb, w_gla, w_mem, w_out, post_g)


def _prep_weights(p, d):
    sb_w = d // 2
    gla_kw = d // 2
    gla_vw = d
    mem_w = d // 2
    w_in = p["w_in"]
    c = 0
    pieces = {}
    for name, width in (("sq", sb_w), ("sk", sb_w), ("sv", sb_w), ("gq", gla_kw), ("gk", gla_kw),
                        ("gv", gla_vw), ("gr", gla_vw), ("ga", GLA_LOW_RANK), ("mq", mem_w)):
        pieces[name] = w_in[:, c:c + width].astype(BF16)
        c += width
    assert c == w_in.shape[1]
    pad = V7X_LANES - GLA_LOW_RANK
    dk = gla_kw // GLA_HEADS
    w = dict(pieces)
    w["ga"] = jnp.pad(pieces["ga"], ((0, 0), (0, pad)))
    wa = jnp.pad(p["gla_w_a2"].astype(BF16), ((0, pad), (0, 0)))
    w["wa"] = wa.reshape(V7X_LANES, GLA_HEADS, dk).transpose(1, 0, 2)
    w["ba"] = p["gla_b_a2"].reshape(GLA_HEADS, 1, dk)
    for name in ("ffn1_w_gu", "ffn1_w_d", "ffn2_w_gu", "ffn2_w_d", "w_sb_br", "w_gla_br", "w_mem_br",
                 "w_gate", "w_out"):
        w[name] = p[name].astype(BF16)
    for name in ("ffn1_pre_g", "ffn1_post_g", "mix_pre_g", "mix_post_g", "ffn2_pre_g", "ffn2_post_g",
                 "gla_norm_g", "b_gate"):
        w[name] = p[name].reshape(1, -1)
    return w


def _layer(x, w, mem_k, mem_v, sb_past_k, sb_past_v, gla_s0):
    b, t, d = x.shape
    n = b * t
    sb_heads = (d // 2) // SB_HEAD_DIM
    h1, u = _ffn(x.reshape(n, d), w["ffn1_pre_g"], w["ffn1_w_gu"], w["ffn1_w_d"], w["ffn1_post_g"],
                 next_g=w["mix_pre_g"])
    sq, sk, sv, gq, gk, mq = _proj(u, [w[k] for k in ("sq", "sk", "sv", "gq", "gk", "mq")],
                                   [BF16, F32, F32, BF16, BF16, BF16])
    gv, gr, ga = _proj(u, [w["gv"], w["gr"], w["ga"]], [BF16, BF16, BF16])
    (gates,) = _proj(u, [w["w_gate"]], [BF16], biases=[w["b_gate"]], act="sigmoid")

    r3 = lambda a: a.reshape(b, t, -1)
    if sb_past_k is None:
        o_sb = _sb_attn(r3(sq), r3(sk), r3(sv), heads=sb_heads)
    else:
        o_sb = _sb_decode(r3(sq), r3(sk), r3(sv), sb_past_k, sb_past_v)
    o_gla, s_new = _gla(r3(gq), r3(gk), r3(gv), r3(gr), r3(ga), w["wa"], w["ba"], w["gla_norm_g"], gla_s0,
                        chunk=GLA_CHUNK)
    o_mem = _mem_attn(r3(mq), mem_k, mem_v, heads=MEM_HEADS)

    h2 = _merge(h1, o_sb.reshape(n, -1), o_gla.reshape(n, -1), o_mem.reshape(n, -1), gates,
                w["w_sb_br"], w["w_gla_br"], w["w_mem_br"], w["w_out"], w["mix_post_g"])
    y = _ffn(h2, w["ffn2_pre_g"], w["ffn2_w_gu"], w["ffn2_w_d"], w["ffn2_post_g"])
    return y.reshape(b, t, d), sk, sv, s_new


def kernel(x_prompt, x_sample, mem_prompt, cache_sb_k, cache_sb_v, state_gla, cache_mem_k, cache_mem_v, ffn1_pre_g, ffn1_w_gu, ffn1_w_d, ffn1_post_g, mix_pre_g, w_in, gla_w_a2, gla_b_a2, gla_norm_g, mem_norm_g, w_mem_kv, w_sb_br, w_gla_br, w_mem_br, w_gate, b_gate, w_out, mix_post_g, ffn2_pre_g, ffn2_w_gu, ffn2_w_d, ffn2_post_g):
    params = dict(ffn1_pre_g=ffn1_pre_g, ffn1_w_gu=ffn1_w_gu, ffn1_w_d=ffn1_w_d, ffn1_post_g=ffn1_post_g,
                  mix_pre_g=mix_pre_g, w_in=w_in, gla_w_a2=gla_w_a2, gla_b_a2=gla_b_a2, gla_norm_g=gla_norm_g,
                  w_sb_br=w_sb_br, w_gla_br=w_gla_br, w_mem_br=w_mem_br, w_gate=w_gate, b_gate=b_gate,
                  w_out=w_out, mix_post_g=mix_post_g, ffn2_pre_g=ffn2_pre_g, ffn2_w_gu=ffn2_w_gu,
                  ffn2_w_d=ffn2_w_d, ffn2_post_g=ffn2_post_g)
    depth = w_in.shape[0]
    bp, tp, d = x_prompt.shape
    bs, ts, _ = x_sample.shape
    m = mem_prompt.shape[1]
    mem_w = d // 2
    h_p, h_s = x_prompt, x_sample
    outs = [[] for _ in range(8)]
    for l in range(depth):
        w = _prep_weights({k: v[l] for k, v in params.items()}, d)
        w_mkv = w_mem_kv[l].astype(BF16)
        mk, mv = _proj(mem_prompt.reshape(bp * m, d), [w_mkv[:, :mem_w], w_mkv[:, mem_w:]], [F32, F32],
                       norm_g=mem_norm_g[l].reshape(1, d))
        mk = mk.reshape(bp, m, mem_w)
        mv = mv.reshape(bp, m, mem_w)
        h_p, k_p, v_p, s_p = _layer(h_p, w, mk, mv, None, None, None)
        h_s, k_s, v_s, s_s = _layer(h_s, w, cache_mem_k[l].reshape(bs, m, mem_w), cache_mem_v[l].reshape(bs, m, mem_w),
                                    cache_sb_k[l], cache_sb_v[l], state_gla[l])
        sb_heads = mem_w // SB_HEAD_DIM
        for lst, val in zip(outs, (k_p.reshape(bp, tp, sb_heads, SB_HEAD_DIM), v_p.reshape(bp, tp, sb_heads, SB_HEAD_DIM),
                                   s_p, mk.reshape(bp, m, MEM_HEADS, -1), mv.reshape(bp, m, MEM_HEADS, -1),
                                   k_s.reshape(bs, ts, sb_heads, SB_HEAD_DIM), v_s.reshape(bs, ts, sb_heads, SB_HEAD_DIM),
                                   s_s)):
            lst.append(val)
    return (h_p, h_s) + tuple(jnp.stack(o) for o in outs)
```

```python
import functools

import jax
import jax.numpy as jnp
from jax import lax
from jax.experimental import pallas as pl
from jax.experimental.pallas import tpu as pltpu

F32 = jnp.float32
BF16 = jnp.bfloat16

EPS = 1e-6
LOG2_E = 1.4426950408889634
V7X_LANES = 128
V7X_MXU_DIM = 256
V7X_VMEM_BYTES = 64 * 1024 * 1024
VMEM_LIMIT_BYTES = V7X_VMEM_BYTES - 4 * 1024 * 1024

SB_HEAD_DIM = 128
GLA_HEADS = 4
GLA_LOW_RANK = 16
GLA_GATE_NORM = 16.0
MEM_HEADS = 4
N_BRANCH = 3
GLA_CHUNK = 64
FFN_NORM_ROWS = 128
SB_LOG_WEIGHT_CUTOFF = -110.0


def _cparams(semantics):
    return pltpu.CompilerParams(dimension_semantics=semantics, vmem_limit_bytes=VMEM_LIMIT_BYTES)


def _rms(x, g):
    ms = jnp.mean(x * x, axis=-1, keepdims=True)
    return x * lax.rsqrt(ms + EPS) * g


def _log_sigmoid(x):
    return jnp.minimum(x, 0.0) - jnp.log(1.0 + jnp.exp(-jnp.abs(x)))


def _split_hi_lo(x):
    hi = x.astype(BF16)
    lo = (x - hi.astype(F32)).astype(BF16)
    return hi, lo


def _pick_tile(n, target):
    t = min(n, target)
    while n % t:
        t //= 2
    return t


def _ffn_kernel(x_ref, pre_g_ref, wg_ref, wu_ref, wd_ref, post_g_ref, *refs, emit_next):
    if emit_next:
        next_g_ref, h_ref, u_ref = refs
        xn_sc = u_ref
    else:
        h_ref, xn_sc = refs
    j = pl.program_id(1)
    tm = x_ref.shape[0]
    rc = min(tm, FFN_NORM_ROWS)

    def for_row_chunks(fn):
        def body(c, carry):
            fn(pl.ds(pl.multiple_of(c * rc, rc), rc))
            return carry
        lax.fori_loop(0, tm // rc, body, 0)

    @pl.when(j == 0)
    def _():
        def prologue(rows):
            xn_sc[rows, :] = _rms(x_ref[rows, :], pre_g_ref[...]).astype(BF16)
            h_ref[rows, :] = jnp.zeros((rc, h_ref.shape[1]), F32)
        for_row_chunks(prologue)

    for rows in (slice(0, tm // 2), slice(tm // 2, tm)):
        xn = xn_sc[rows, :]
        g = jnp.dot(xn, wg_ref[...], preferred_element_type=F32)
        u = jnp.dot(xn, wu_ref[...], preferred_element_type=F32)
        act = (g * jax.nn.sigmoid(g) * u).astype(BF16)
        h_ref[rows, :] += jnp.dot(act, wd_ref[...], preferred_element_type=F32)

    @pl.when(j == pl.num_programs(1) - 1)
    def _():
        def epilogue(rows):
            h = x_ref[rows, :] + 0.5 * _rms(h_ref[rows, :], post_g_ref[...])
            h_ref[rows, :] = h
            if emit_next:
                u_ref[rows, :] = _rms(h, next_g_ref[...]).astype(BF16)
        for_row_chunks(epilogue)


def _ffn(x, pre_g, w_gu, w_d, post_g, next_g=None, *, tm=1024, tf=512):
    emit_next = next_g is not None
    n, d = x.shape
    d_ff = w_d.shape[0]
    tm = _pick_tile(n, tm)
    tf = max(t for t in range(V7X_LANES, tf + 1, V7X_LANES) if d_ff % t == 0)
    nf = d_ff // tf
    row = lambda i, j: (i, 0)
    out_shape = [jax.ShapeDtypeStruct((n, d), F32)]
    out_specs = [pl.BlockSpec((tm, d), row)]
    in_specs = [
        pl.BlockSpec((tm, d), row),
        _resident((1, d)),
        pl.BlockSpec((d, tf), lambda i, j: (0, j)),
        pl.BlockSpec((d, tf), lambda i, j: (0, nf + j)),
        pl.BlockSpec((tf, d), lambda i, j: (j, 0)),
        _resident((1, d)),
    ]
    args = [x, pre_g, w_gu, w_gu, w_d, post_g]
    if emit_next:
        in_specs.append(_resident((1, d)))
        args.append(next_g)
        out_shape.append(jax.ShapeDtypeStruct((n, d), BF16))
        out_specs.append(pl.BlockSpec((tm, d), row))
    outs = pl.pallas_call(
        functools.partial(_ffn_kernel, emit_next=emit_next),
        grid=(n // tm, nf),
        in_specs=in_specs,
        out_specs=out_specs,
        out_shape=out_shape,
        scratch_shapes=[] if emit_next else [pltpu.VMEM((tm, d), BF16)],
        compiler_params=_cparams(("parallel", "arbitrary")),
        name="ffn",
    )(*args)
    return outs if emit_next else outs[0]


def _proj_kernel(*refs, n_out, has_bias, has_norm, act):
    it = iter(refs)
    x_ref = next(it)
    g_ref = next(it) if has_norm else None
    w_refs = [next(it) for _ in range(n_out)]
    b_refs = [next(it) for _ in range(n_out)] if has_bias else None
    o_refs = [next(it) for _ in range(n_out)]
    x = _rms(x_ref[...], g_ref[...]).astype(BF16) if has_norm else x_ref[...]
    for k in range(n_out):
        acc = jnp.dot(x, w_refs[k][...], preferred_element_type=F32)
        if has_bias:
            acc = acc + b_refs[k][...]
        if act == "sigmoid":
            acc = jax.nn.sigmoid(acc)
        o_refs[k][...] = acc.astype(o_refs[k].dtype)


def _resident(shape):
    return pl.BlockSpec(shape, lambda *_: (0,) * len(shape), pipeline_mode=pl.Buffered(1))


def _proj(x, ws, out_dtypes, *, biases=None, norm_g=None, act=None, tm=512):
    n, kdim = x.shape
    tm = _pick_tile(n, tm)
    n_out = len(ws)
    in_specs = [pl.BlockSpec((tm, kdim), lambda i: (i, 0))]
    args = [x]
    if norm_g is not None:
        in_specs.append(_resident((1, kdim)))
        args.append(norm_g)
    in_specs += [_resident(w.shape) for w in ws]
    args += list(ws)
    if biases is not None:
        in_specs += [_resident(b.shape) for b in biases]
        args += list(biases)
    return pl.pallas_call(
        functools.partial(_proj_kernel, n_out=n_out, has_bias=biases is not None,
                          has_norm=norm_g is not None, act=act),
        grid=(n // tm,),
        in_specs=in_specs,
        out_specs=[pl.BlockSpec((tm, w.shape[1]), lambda i: (i, 0)) for w in ws],
        out_shape=[jax.ShapeDtypeStruct((n, w.shape[1]), dt) for w, dt in zip(ws, out_dtypes)],
        compiler_params=_cparams(("parallel",)),
        name="proj",
    )(*args)


def _suffix_ones(n):
    r = lax.broadcasted_iota(jnp.int32, (n, n), 0)
    c = lax.broadcasted_iota(jnp.int32, (n, n), 1)
    return jnp.where(r >= c, 1.0, 0.0).astype(BF16)


def _sb_tiles(qs, ks, vs, u_mat, psums, *, scale, masked):
    n = len(qs)
    nt_dims = (((1,), (1,)), ((), ()))
    raw = [lax.dot_general(qs[g], ks[g], nt_dims, preferred_element_type=F32) for g in range(n)]
    zs = [r * scale for r in raw]
    sign = jnp.uint32(0x80000000)
    neg_abs = [lax.bitcast_convert_type(lax.bitcast_convert_type(r, jnp.uint32) | sign, F32) for r in raw]
    sps = [jnp.maximum(zs[g], 0.0) + jnp.log(1.0 + jnp.exp2(neg_abs[g] * (scale * LOG2_E))) for g in range(n)]
    if masked:
        rows = lax.broadcasted_iota(jnp.int32, zs[0].shape, 0)
        cols = lax.broadcasted_iota(jnp.int32, zs[0].shape, 1)
        mask = cols < rows
        sps = [jnp.where(mask, s, 0.0) for s in sps]
    splits = [_split_hi_lo(s) for s in sps]
    css = [jnp.dot(hi, u_mat, preferred_element_type=F32) + jnp.dot(lo, u_mat, preferred_element_type=F32)
           for hi, lo in splits]
    weights = [jnp.exp(zs[g] - css[g] - psums[g]) for g in range(n)]
    if masked:
        weights = [jnp.where(mask, a, 0.0) for a in weights]
    contribs = [jnp.dot(weights[g].astype(BF16), vs[g], preferred_element_type=F32) for g in range(n)]
    return [psums[g] + css[g][:, 0:1] for g in range(n)], contribs


def _sb_kernel(q_ref, kn_ref, vn_ref, o_ref, kb_sc, vb_sc, kmax_sc, zb_sc, r_sc, acc_sc, *,
               rg, groups, t_new, scale):
    qi = pl.program_id(2)
    sweep_left = t_new > rg

    @pl.when(qi == 0)
    def _():
        kb_sc[0:rg, :] = jnp.zeros((rg, kb_sc.shape[1]), BF16)
        vb_sc[0:rg, :] = jnp.zeros((rg, vb_sc.shape[1]), BF16)
        ch = min(t_new, 512)

        def conv(i, kmax2):
            src = pl.ds(pl.multiple_of(i * ch, ch), ch)
            dst = pl.ds(pl.multiple_of(rg + i * ch, rg), ch)
            kb = kn_ref[0, src, :].astype(BF16)
            kb_sc[dst, :] = kb
            vb_sc[dst, :] = vn_ref[0, src, :].astype(BF16)
            kf = kb.astype(F32)
            n2 = jnp.sum(kf * kf, axis=1, keepdims=True)
            return jnp.maximum(kmax2, jnp.max(n2, axis=0, keepdims=True))

        kmax2 = lax.fori_loop(0, t_new // ch, conv, jnp.zeros((1, 1), F32))
        kmax_sc[...] = jnp.broadcast_to(kmax2, kmax_sc.shape)

    u_mat = _suffix_ones(rg)
    tiles = functools.partial(_sb_tiles, scale=scale)
    gslice = lambda g: slice(g * rg, (g + 1) * rg)
    qs = [q_ref[0, gslice(g), :] for g in range(groups)]
    first_tile = qi * groups

    def tile_rows(idx):
        return pl.ds(pl.multiple_of((jnp.maximum(idx, -1) + 1) * rg, rg), rg)

    def any_group_continues(idxs, psums):
        go = None
        for g in range(groups):
            go_g = (idxs[g] >= 1) & (jnp.max(zb_sc[gslice(g), :] - psums[g]) > SB_LOG_WEIGHT_CUTOFF)
            go = go_g if go is None else go | go_g
        return go.astype(jnp.int32)

    idxs = [first_tile + g for g in range(groups)]
    psums, contribs = tiles(qs, [kb_sc[tile_rows(i), :] for i in idxs], [vb_sc[tile_rows(i), :] for i in idxs],
                            u_mat, [jnp.zeros((rg, 1), F32)] * groups, masked=True)
    for g in range(groups):
        acc_sc[gslice(g), :] = contribs[g]
        r_sc[gslice(g), :] = psums[g]

    if sweep_left:
        kmax = jnp.sqrt(kmax_sc[0:1, 0:1]) * (scale * 1.01)
        for g in range(groups):
            qf = qs[g].astype(F32)
            zb_sc[gslice(g), :] = jnp.sqrt(jnp.sum(qf * qf, axis=1, keepdims=True)) * kmax

        def body(carry):
            dist, _ = carry
            idxs = [first_tile + g - dist for g in range(groups)]
            psums, contribs = tiles(qs, [kb_sc[tile_rows(i), :] for i in idxs],
                                    [vb_sc[tile_rows(i), :] for i in idxs], u_mat,
                                    [r_sc[gslice(g), :] for g in range(groups)], masked=False)
            for g in range(groups):
                acc_sc[gslice(g), :] += contribs[g]
                r_sc[gslice(g), :] = psums[g]
            return dist + 1, any_group_continues(idxs, psums)

        lax.while_loop(lambda carry: carry[1] != 0, body, (jnp.int32(1), any_group_continues(idxs, psums)))

    o_ref[0] = acc_sc[...].astype(o_ref.dtype)


def _sb_attn(q, k, v, *, heads, rg=V7X_MXU_DIM, groups=4):
    b, t, hd_all = q.shape
    d = hd_all // heads
    rg = _pick_tile(t, rg)
    groups = _pick_tile(t // rg, groups)
    bq = rg * groups
    seq = lambda bi, hi, qi: (bi, 0, hi)
    blk = lambda bi, hi, qi: (bi, qi, hi)
    return pl.pallas_call(
        functools.partial(_sb_kernel, rg=rg, groups=groups, t_new=t, scale=d ** -0.5),
        grid=(b, heads, t // bq),
        in_specs=[pl.BlockSpec((1, bq, d), blk), pl.BlockSpec((1, t, d), seq), pl.BlockSpec((1, t, d), seq)],
        out_specs=pl.BlockSpec((1, bq, d), blk),
        out_shape=jax.ShapeDtypeStruct((b, t, hd_all), BF16),
        scratch_shapes=[pltpu.VMEM((rg + t, d), BF16), pltpu.VMEM((rg + t, d), BF16),
                        pltpu.VMEM((8, V7X_LANES), F32),
                        pltpu.VMEM((bq, 1), F32), pltpu.VMEM((bq, 1), F32),
                        pltpu.VMEM((bq, d), F32)],
        compiler_params=_cparams(("parallel", "parallel", "arbitrary")),
        name="sb_attn",
    )(q, k, v)


def _sb_decode_kernel(q_ref, kn_ref, vn_ref, kp_ref, vp_ref, o_ref, *, heads, bk, scale):
    t = q_ref.shape[1]
    d = q_ref.shape[2] // heads
    past = kp_ref.shape[1] // heads
    hs = range(heads)
    cols = lambda h: slice(h * d, (h + 1) * d)
    tiles = functools.partial(_sb_tiles, scale=scale)
    qs = [q_ref[0, :, cols(h)] for h in hs]
    psums, accs = tiles(qs, [kn_ref[0, :, cols(h)].astype(BF16) for h in hs],
                        [vn_ref[0, :, cols(h)].astype(BF16) for h in hs], _suffix_ones(t),
                        [jnp.zeros((t, 1), F32)] * heads, masked=True)
    u_past = _suffix_ones(bk)
    for i in range(past // bk - 1, -1, -1):
        head_rows = lambda h: pl.ds(i * bk * heads + h, bk, stride=heads)
        psums, contribs = tiles(qs, [kp_ref[0, head_rows(h), :].astype(BF16) for h in hs],
                                [vp_ref[0, head_rows(h), :].astype(BF16) for h in hs], u_past, psums,
                                masked=False)
        accs = [a + c for a, c in zip(accs, contribs)]
    for h in hs:
        o_ref[0, :, cols(h)] = accs[h].astype(o_ref.dtype)


def _sb_decode(q, k_new, v_new, k_cache, v_cache, *, bk=V7X_MXU_DIM):
    b, t, hd_all = q.shape
    _, p, heads, d = k_cache.shape
    bk = _pick_tile(p, bk)
    new = pl.BlockSpec((1, t, hd_all), lambda bi: (bi, 0, 0))
    old = pl.BlockSpec((1, p * heads, d), lambda bi: (bi, 0, 0))
    return pl.pallas_call(
        functools.partial(_sb_decode_kernel, heads=heads, bk=bk, scale=d ** -0.5),
        grid=(b,),
        in_specs=[new, new, new, old, old],
        out_specs=new,
        out_shape=jax.ShapeDtypeStruct((b, t, hd_all), BF16),
        compiler_params=_cparams(("parallel",)),
        name="sb_decode",
    )(q, k_new, v_new, k_cache.reshape(b, p * heads, d), v_cache.reshape(b, p * heads, d))


def _gla_kernel(*refs, chunk, n_chunks, heads, has_s0, dk_scale):
    if has_s0:
        q_ref, k_ref, v_ref, r_ref, ga_ref, wa_ref, ba_ref, ng_ref, s0_ref, o_ref, s_out_ref, s_sc = refs
    else:
        q_ref, k_ref, v_ref, r_ref, ga_ref, wa_ref, ba_ref, ng_ref, o_ref, s_out_ref, s_sc = refs
    tg = pl.program_id(1)

    @pl.when(tg == 0)
    def _():
        s_sc[...] = s0_ref[0] if has_s0 else jnp.zeros_like(s_sc)

    rows = lax.broadcasted_iota(jnp.int32, (chunk, chunk), 0)
    cols = lax.broadcasted_iota(jnp.int32, (chunk, chunk), 1)
    causal = cols <= rows
    tri = jnp.where(causal, 1.0, 0.0).astype(BF16)
    ones_cols = jnp.ones((chunk, V7X_LANES), BF16)
    _, dk, dv = s_sc.shape
    mid = chunk // 2 - 1
    nt_dims = (((1,), (1,)), ((), ()))
    tn_dims = (((0,), (0,)), ((), ()))
    hs = range(heads)
    kcols = lambda h: slice(h * dk, (h + 1) * dk)
    vcols = lambda h: slice(h * dv, (h + 1) * dv)

    def body(c, carry):
        sl = pl.ds(pl.multiple_of(c * chunk, chunk), chunk)
        ga = ga_ref[0, sl, :]
        gs = [_log_sigmoid(jnp.dot(ga, wa_ref[h], preferred_element_type=F32) + ba_ref[h]) / GLA_GATE_NORM
              for h in hs]
        g_split = [_split_hi_lo(g) for g in gs]
        bs = [jnp.dot(tri, hi, preferred_element_type=F32) + jnp.dot(tri, lo, preferred_element_type=F32)
              for hi, lo in g_split]
        b_last_cols = [lax.dot_general(hi, ones_cols, tn_dims, preferred_element_type=F32)
                       + lax.dot_general(lo, ones_cols, tn_dims, preferred_element_type=F32)
                       for hi, lo in g_split]
        qs = [q_ref[0, sl, kcols(h)].astype(F32) * dk_scale for h in hs]
        ks = [k_ref[0, sl, kcols(h)].astype(F32) for h in hs]
        vs = [v_ref[0, sl, vcols(h)] for h in hs]
        s_prev = [s_sc[h] for h in hs]
        q_in = [(qs[h] * jnp.exp(bs[h])).astype(BF16) for h in hs]
        q_m = [(qs[h] * jnp.exp(bs[h] - bs[h][mid:mid + 1, :])).astype(BF16) for h in hs]
        k_m = [(ks[h] * jnp.exp(bs[h][mid:mid + 1, :] - bs[h])).astype(BF16) for h in hs]
        k_st = [(ks[h] * jnp.exp(bs[h][chunk - 1:chunk, :] - bs[h])).astype(BF16) for h in hs]
        o_inter = [jnp.dot(q_in[h], s_prev[h].astype(BF16), preferred_element_type=F32) for h in hs]
        att = [jnp.where(causal, lax.dot_general(q_m[h], k_m[h], nt_dims, preferred_element_type=F32), 0.0)
               for h in hs]
        outs = [o_inter[h] + jnp.dot(att[h].astype(BF16), vs[h], preferred_element_type=F32) for h in hs]
        for h in hs:
            decay = jnp.concatenate([jnp.exp(b_last_cols[h])] * (dv // V7X_LANES), axis=1)
            s_sc[h] = decay * s_prev[h] + lax.dot_general(k_st[h], vs[h], tn_dims, preferred_element_type=F32)
        for h in hs:
            r = r_ref[0, sl, vcols(h)].astype(F32)
            o_ref[0, sl, vcols(h)] = (_rms(outs[h], ng_ref[...]) * (r * jax.nn.sigmoid(r))).astype(o_ref.dtype)
        return carry

    lax.fori_loop(0, n_chunks, body, 0)

    @pl.when(tg == pl.num_programs(1) - 1)
    def _():
        s_out_ref[0] = s_sc[...]


def _gla(q, k, v, r, ga, wa, ba, ng, s0, *, chunk, tg=512):
    b, t, _ = q.shape
    heads, _, dk = wa.shape
    dv = v.shape[-1] // heads
    chunk = min(chunk, t)
    tg = _pick_tile(t, tg)
    assert tg % chunk == 0
    has_s0 = s0 is not None
    tok = lambda bi, ti: (bi, ti, 0)
    in_specs = [
        pl.BlockSpec((1, tg, heads * dk), tok), pl.BlockSpec((1, tg, heads * dk), tok),
        pl.BlockSpec((1, tg, heads * dv), tok), pl.BlockSpec((1, tg, heads * dv), tok),
        pl.BlockSpec((1, tg, V7X_LANES), tok),
        _resident(wa.shape), _resident(ba.shape), _resident(ng.shape),
    ]
    args = [q, k, v, r, ga, wa, ba, ng]
    state_spec = pl.BlockSpec((1, heads, dk, dv), lambda bi, ti: (bi, 0, 0, 0))
    if has_s0:
        in_specs.append(state_spec)
        args.append(s0)
    return pl.pallas_call(
        functools.partial(_gla_kernel, chunk=chunk, n_chunks=tg // chunk, heads=heads, has_s0=has_s0,
                          dk_scale=dk ** -0.5),
        grid=(b, t // tg),
        in_specs=in_specs,
        out_specs=[pl.BlockSpec((1, tg, heads * dv), tok), state_spec],
        out_shape=[jax.ShapeDtypeStruct((b, t, heads * dv), BF16),
                   jax.ShapeDtypeStruct((b, heads, dk, dv), F32)],
        scratch_shapes=[pltpu.VMEM((heads, dk, dv), F32)],
        compiler_params=_cparams(("parallel", "arbitrary")),
        name="gla",
    )(*args)


def _mem_kernel(q_ref, k_ref, v_ref, o_ref, *, heads, scale):
    hd = q_ref.shape[-1] // heads
    per_head = len(k_ref.shape) == 4
    for h in range(heads):
        cs = slice(h * hd, (h + 1) * hd)
        q = q_ref[0, :, cs]
        k = (k_ref[0, :, h, :] if per_head else k_ref[0, :, cs]).astype(BF16)
        v = (v_ref[0, :, h, :] if per_head else v_ref[0, :, cs]).astype(BF16)
        s = lax.dot_general(q, k, (((1,), (1,)), ((), ())), preferred_element_type=F32) * scale
        e = jnp.exp(s - jnp.max(s, axis=-1, keepdims=True))
        p = e / jnp.sum(e, axis=-1, keepdims=True)
        o_ref[0, :, cs] = jnp.dot(p.astype(BF16), v, preferred_element_type=F32).astype(o_ref.dtype)


def _mem_attn(q, mk, mv, *, heads, tq=1024):
    b, t, w = q.shape
    tq = _pick_tile(t, tq)
    mem_block = (1,) + mk.shape[1:]
    mem = pl.BlockSpec(mem_block, lambda bi, ti: (bi,) + (0,) * (len(mem_block) - 1))
    return pl.pallas_call(
        functools.partial(_mem_kernel, heads=heads, scale=(w // heads) ** -0.5),
        grid=(b, t // tq),
        in_specs=[pl.BlockSpec((1, tq, w), lambda bi, ti: (bi, ti, 0)), mem, mem],
        out_specs=pl.BlockSpec((1, tq, w), lambda bi, ti: (bi, ti, 0)),
        out_shape=jax.ShapeDtypeStruct((b, t, w), BF16),
        compiler_params=_cparams(("parallel", "arbitrary")),
        name="mem_attn",
    )(q, mk, mv)


def _merge_kernel(h_ref, osb_ref, ogla_ref, omem_ref, gates_ref, wsb_ref, wgla_ref, wmem_ref, wout_ref,
                  post_g_ref, h2_ref):
    d = h_ref.shape[1]
    branches = ((osb_ref, wsb_ref), (ogla_ref, wgla_ref), (omem_ref, wmem_ref))
    merged = None
    for k, (o_ref, w_ref) in enumerate(branches):
        term = gates_ref[:, k * d:(k + 1) * d].astype(F32) * jnp.dot(o_ref[...], w_ref[...],
                                                                    preferred_element_type=F32)
        merged = term if merged is None else merged + term
    m = jnp.dot(merged.astype(BF16), wout_ref[...], preferred_element_type=F32)
    h2_ref[...] = h_ref[...] + _rms(m, post_g_ref[...])


def _merge(h, o_sb, o_gla, o_mem, gates, w_sb, w_gla, w_mem, w_out, post_g, *, tm=256):
    n, d = h.shape
    tm = _pick_tile(n, tm)
    row = lambda a: pl.BlockSpec((tm, a.shape[1]), lambda i: (i, 0))
    return pl.pallas_call(
        _merge_kernel,
        grid=(n // tm,),
        in_specs=[row(h), row(o_sb), row(o_gla), row(o_mem), row(gates),
                  _resident(w_sb.shape), _resident(w_gla.shape), _resident(w_mem.shape),
                  _resident(w_out.shape), _resident(post_g.shape)],
        out_specs=row(h),
        out_shape=jax.ShapeDtypeStruct((n, d), F32),
        compiler_params=_cparams(("parallel",)),
        name="merge",
    )(h, o_sb, o_gla, o_mem, gates, w_sb, w_gla, w_mem, w_out, post_g)


def _prep_weights(p, d):
    sb_w = d // 2
    gla_kw = d // 2
    gla_vw = d
    mem_w = d // 2
    w_in = p["w_in"]
    c = 0
    pieces = {}
    for name, width in (("sq", sb_w), ("sk", sb_w), ("sv", sb_w), ("gq", gla_kw), ("gk", gla_kw),
                        ("gv", gla_vw), ("gr", gla_vw), ("ga", GLA_LOW_RANK), ("mq", mem_w)):
        pieces[name] = w_in[:, c:c + width].astype(BF16)
        c += width
    assert c == w_in.shape[1]
    pad = V7X_LANES - GLA_LOW_RANK
    dk = gla_kw // GLA_HEADS
    w = dict(pieces)
    w["ga"] = jnp.pad(pieces["ga"], ((0, 0), (0, pad)))
    wa = jnp.pad(p["gla_w_a2"].astype(BF16), ((0, pad), (0, 0)))
    w["wa"] = wa.reshape(V7X_LANES, GLA_HEADS, dk).transpose(1, 0, 2)
    w["ba"] = p["gla_b_a2"].reshape(GLA_HEADS, 1, dk)
    for name in ("ffn1_w_gu", "ffn1_w_d", "ffn2_w_gu", "ffn2_w_d", "w_sb_br", "w_gla_br", "w_mem_br",
                 "w_gate", "w_out"):
        w[name] = p[name].astype(BF16)
    for name in ("ffn1_pre_g", "ffn1_post_g", "mix_pre_g", "mix_post_g", "ffn2_pre_g", "ffn2_post_g",
                 "gla_norm_g", "b_gate"):
        w[name] = p[name].reshape(1, -1)
    return w


def _layer(x, w, mem_k, mem_v, sb_past_k, sb_past_v, gla_s0):
    b, t, d = x.shape
    n = b * t
    sb_heads = (d // 2) // SB_HEAD_DIM
    h1, u = _ffn(x.reshape(n, d), w["ffn1_pre_g"], w["ffn1_w_gu"], w["ffn1_w_d"], w["ffn1_post_g"],
                 next_g=w["mix_pre_g"])
    sq, sk, sv, gq, gk, mq = _proj(u, [w[k] for k in ("sq", "sk", "sv", "gq", "gk", "mq")],
                                   [BF16, F32, F32, BF16, BF16, BF16])
    gv, gr, ga = _proj(u, [w["gv"], w["gr"], w["ga"]], [BF16, BF16, BF16])
    (gates,) = _proj(u, [w["w_gate"]], [BF16], biases=[w["b_gate"]], act="sigmoid")

    r3 = lambda a: a.reshape(b, t, -1)
    if sb_past_k is None:
        o_sb = _sb_attn(r3(sq), r3(sk), r3(sv), heads=sb_heads)
    else:
        o_sb = _sb_decode(r3(sq), r3(sk), r3(sv), sb_past_k, sb_past_v)
    o_gla, s_new = _gla(r3(gq), r3(gk), r3(gv), r3(gr), r3(ga), w["wa"], w["ba"], w["gla_norm_g"], gla_s0,
                        chunk=GLA_CHUNK)
    o_mem = _mem_attn(r3(mq), mem_k, mem_v, heads=MEM_HEADS)

    h2 = _merge(h1, o_sb.reshape(n, -1), o_gla.reshape(n, -1), o_mem.reshape(n, -1), gates,
                w["w_sb_br"], w["w_gla_br"], w["w_mem_br"], w["w_out"], w["mix_post_g"])
    y = _ffn(h2, w["ffn2_pre_g"], w["ffn2_w_gu"], w["ffn2_w_d"], w["ffn2_post_g"])
    return y.reshape(b, t, d), sk, sv, s_new


def kernel(x_prompt, x_sample, mem_prompt, cache_sb_k, cache_sb_v, state_gla, cache_mem_k, cache_mem_v, ffn1_pre_g, ffn1_w_gu, ffn1_w_d, ffn1_post_g, mix_pre_g, w_in, gla_w_a2, gla_b_a2, gla_norm_g, mem_norm_g, w_mem_kv, w_sb_br, w_gla_br, w_mem_br, w_gate, b_gate, w_out, mix_post_g, ffn2_pre_g, ffn2_w_gu, ffn2_w_d, ffn2_post_g):
    params = dict(ffn1_pre_g=ffn1_pre_g, ffn1_w_gu=ffn1_w_gu, ffn1_w_d=ffn1_w_d, ffn1_post_g=ffn1_post_g,
                  mix_pre_g=mix_pre_g, w_in=w_in, gla_w_a2=gla_w_a2, gla_b_a2=gla_b_a2, gla_norm_g=gla_norm_g,
                  w_sb_br=w_sb_br, w_gla_br=w_gla_br, w_mem_br=w_mem_br, w_gate=w_gate, b_gate=b_gate,
                  w_out=w_out, mix_post_g=mix_post_g, ffn2_pre_g=ffn2_pre_g, ffn2_w_gu=ffn2_w_gu,
                  ffn2_w_d=ffn2_w_d, ffn2_post_g=ffn2_post_g)
    depth = w_in.shape[0]
    bp, tp, d = x_prompt.shape
    bs, ts, _ = x_sample.shape
    m = mem_prompt.shape[1]
    mem_w = d // 2
    h_p, h_s = x_prompt, x_sample
    outs = [[] for _ in range(8)]
    for l in range(depth):
        w = _prep_weights({k: v[l] for k, v in params.items()}, d)
        w_mkv = w_mem_kv[l].astype(BF16)
        mk, mv = _proj(mem_prompt.reshape(bp * m, d), [w_mkv[:, :mem_w], w_mkv[:, mem_w:]], [F32, F32],
                       norm_g=mem_norm_g[l].reshape(1, d))
        mk = mk.reshape(bp, m, mem_w)
        mv = mv.reshape(bp, m, mem_w)
        h_p, k_p, v_p, s_p = _layer(h_p, w, mk, mv, None, None, None)
        h_s, k_s, v_s, s_s = _layer(h_s, w, cache_mem_k[l], cache_mem_v[l], cache_sb_k[l], cache_sb_v[l],
                                    state_gla[l])
        sb_heads = mem_w // SB_HEAD_DIM
        for lst, val in zip(outs, (k_p.reshape(bp, tp, sb_heads, SB_HEAD_DIM), v_p.reshape(bp, tp, sb_heads, SB_HEAD_DIM),
                                   s_p, mk.reshape(bp, m, MEM_HEADS, -1), mv.reshape(bp, m, MEM_HEADS, -1),
                                   k_s.reshape(bs, ts, sb_heads, SB_HEAD_DIM), v_s.reshape(bs, ts, sb_heads, SB_HEAD_DIM),
                                   s_s)):
            lst.append(val)
    return (h_p, h_s) + tuple(jnp.stack(o) for o in outs)
```

```python
import functools

import jax
import jax.numpy as jnp
from jax import lax
from jax.experimental import pallas as pl
from jax.experimental.pallas import tpu as pltpu

F32 = jnp.float32
BF16 = jnp.bfloat16

EPS = 1e-6
LOG2_E = 1.4426950408889634
V7X_LANES = 128
V7X_MXU_DIM = 256
V7X_VMEM_BYTES = 64 * 1024 * 1024
VMEM_LIMIT_BYTES = V7X_VMEM_BYTES - 4 * 1024 * 1024

SB_HEAD_DIM = 128
GLA_HEADS = 4
GLA_LOW_RANK = 16
GLA_GATE_NORM = 16.0
MEM_HEADS = 4
N_BRANCH = 3
GLA_CHUNK = 64
FFN_NORM_ROWS = 128
SB_LOG_WEIGHT_CUTOFF = -110.0


def _cparams(semantics):
    return pltpu.CompilerParams(dimension_semantics=semantics, vmem_limit_bytes=VMEM_LIMIT_BYTES)


def _rms(x, g):
    ms = jnp.mean(x * x, axis=-1, keepdims=True)
    return x * lax.rsqrt(ms + EPS) * g


def _log_sigmoid(x):
    return jnp.minimum(x, 0.0) - jnp.log(1.0 + jnp.exp(-jnp.abs(x)))


def _split_hi_lo(x):
    hi = x.astype(BF16)
    lo = (x - hi.astype(F32)).astype(BF16)
    return hi, lo


def _pick_tile(n, target):
    t = min(n, target)
    while n % t:
        t //= 2
    return t


def _ffn_kernel(x_ref, pre_g_ref, wg_ref, wu_ref, wd_ref, post_g_ref, *refs, emit_next):
    if emit_next:
        next_g_ref, h_ref, u_ref = refs
        xn_sc = u_ref
    else:
        h_ref, xn_sc = refs
    j = pl.program_id(1)
    tm = x_ref.shape[0]
    rc = min(tm, FFN_NORM_ROWS)

    def for_row_chunks(fn):
        def body(c, carry):
            fn(pl.ds(pl.multiple_of(c * rc, rc), rc))
            return carry
        lax.fori_loop(0, tm // rc, body, 0)

    @pl.when(j == 0)
    def _():
        def prologue(rows):
            xn_sc[rows, :] = _rms(x_ref[rows, :], pre_g_ref[...]).astype(BF16)
            h_ref[rows, :] = jnp.zeros((rc, h_ref.shape[1]), F32)
        for_row_chunks(prologue)

    for rows in (slice(0, tm // 2), slice(tm // 2, tm)):
        xn = xn_sc[rows, :]
        g = jnp.dot(xn, wg_ref[...], preferred_element_type=F32)
        u = jnp.dot(xn, wu_ref[...], preferred_element_type=F32)
        act = (g * jax.nn.sigmoid(g) * u).astype(BF16)
        h_ref[rows, :] += jnp.dot(act, wd_ref[...], preferred_element_type=F32)

    @pl.when(j == pl.num_programs(1) - 1)
    def _():
        def epilogue(rows):
            h = x_ref[rows, :] + 0.5 * _rms(h_ref[rows, :], post_g_ref[...])
            h_ref[rows, :] = h
            if emit_next:
                u_ref[rows, :] = _rms(h, next_g_ref[...]).astype(BF16)
        for_row_chunks(epilogue)


def _ffn(x, pre_g, w_gu, w_d, post_g, next_g=None, *, tm=1024, tf=512):
    emit_next = next_g is not None
    n, d = x.shape
    d_ff = w_d.shape[0]
    tm = _pick_tile(n, tm)
    tf = max(t for t in range(V7X_LANES, tf + 1, V7X_LANES) if d_ff % t == 0)
    nf = d_ff // tf
    row = lambda i, j: (i, 0)
    out_shape = [jax.ShapeDtypeStruct((n, d), F32)]
    out_specs = [pl.BlockSpec((tm, d), row)]
    in_specs = [
        pl.BlockSpec((tm, d), row),
        _resident((1, d)),
        pl.BlockSpec((d, tf), lambda i, j: (0, j)),
        pl.BlockSpec((d, tf), lambda i, j: (0, nf + j)),
        pl.BlockSpec((tf, d), lambda i, j: (j, 0)),
        _resident((1, d)),
    ]
    args = [x, pre_g, w_gu, w_gu, w_d, post_g]
    if emit_next:
        in_specs.append(_resident((1, d)))
        args.append(next_g)
        out_shape.append(jax.ShapeDtypeStruct((n, d), BF16))
        out_specs.append(pl.BlockSpec((tm, d), row))
    outs = pl.pallas_call(
        functools.partial(_ffn_kernel, emit_next=emit_next),
        grid=(n // tm, nf),
        in_specs=in_specs,
        out_specs=out_specs,
        out_shape=out_shape,
        scratch_shapes=[] if emit_next else [pltpu.VMEM((tm, d), BF16)],
        compiler_params=_cparams(("parallel", "arbitrary")),
        name="ffn",
    )(*args)
    return outs if emit_next else outs[0]


def _proj_kernel(*refs, n_out, has_bias, has_norm, act):
    it = iter(refs)
    x_ref = next(it)
    g_ref = next(it) if has_norm else None
    w_refs = [next(it) for _ in range(n_out)]
    b_refs = [next(it) for _ in range(n_out)] if has_bias else None
    o_refs = [next(it) for _ in range(n_out)]
    x = _rms(x_ref[...], g_ref[...]).astype(BF16) if has_norm else x_ref[...]
    for k in range(n_out):
        acc = jnp.dot(x, w_refs[k][...], preferred_element_type=F32)
        if has_bias:
            acc = acc + b_refs[k][...]
        if act == "sigmoid":
            acc = jax.nn.sigmoid(acc)
        o_refs[k][...] = acc.astype(o_refs[k].dtype)


def _resident(shape):
    return pl.BlockSpec(shape, lambda *_: (0,) * len(shape), pipeline_mode=pl.Buffered(1))


def _proj(x, ws, out_dtypes, *, biases=None, norm_g=None, act=None, tm=512):
    n, kdim = x.shape
    tm = _pick_tile(n, tm)
    n_out = len(ws)
    in_specs = [pl.BlockSpec((tm, kdim), lambda i: (i, 0))]
    args = [x]
    if norm_g is not None:
        in_specs.append(_resident((1, kdim)))
        args.append(norm_g)
    in_specs += [_resident(w.shape) for w in ws]
    args += list(ws)
    if biases is not None:
        in_specs += [_resident(b.shape) for b in biases]
        args += list(biases)
    return pl.pallas_call(
        functools.partial(_proj_kernel, n_out=n_out, has_bias=biases is not None,
                          has_norm=norm_g is not None, act=act),
        grid=(n // tm,),
        in_specs=in_specs,
        out_specs=[pl.BlockSpec((tm, w.shape[1]), lambda i: (i, 0)) for w in ws],
        out_shape=[jax.ShapeDtypeStruct((n, w.shape[1]), dt) for w, dt in zip(ws, out_dtypes)],
        compiler_params=_cparams(("parallel",)),
        name="proj",
    )(*args)


def _suffix_ones(n):
    r = lax.broadcasted_iota(jnp.int32, (n, n), 0)
    c = lax.broadcasted_iota(jnp.int32, (n, n), 1)
    return jnp.where(r >= c, 1.0, 0.0).astype(BF16)


def _sb_tiles(qs, ks, vs, u_mat, psums, *, scale, masked):
    n = len(qs)
    nt_dims = (((1,), (1,)), ((), ()))
    raw = [lax.dot_general(qs[g], ks[g], nt_dims, preferred_element_type=F32) for g in range(n)]
    zs = [r * scale for r in raw]
    sign = jnp.uint32(0x80000000)
    neg_abs = [lax.bitcast_convert_type(lax.bitcast_convert_type(r, jnp.uint32) | sign, F32) for r in raw]
    sps = [jnp.maximum(zs[g], 0.0) + jnp.log(1.0 + jnp.exp2(neg_abs[g] * (scale * LOG2_E))) for g in range(n)]
    if masked:
        rows = lax.broadcasted_iota(jnp.int32, zs[0].shape, 0)
        cols = lax.broadcasted_iota(jnp.int32, zs[0].shape, 1)
        mask = cols < rows
        sps = [jnp.where(mask, s, 0.0) for s in sps]
    splits = [_split_hi_lo(s) for s in sps]
    css = [jnp.dot(hi, u_mat, preferred_element_type=F32) + jnp.dot(lo, u_mat, preferred_element_type=F32)
           for hi, lo in splits]
    weights = [jnp.exp(zs[g] - css[g] - psums[g]) for g in range(n)]
    if masked:
        weights = [jnp.where(mask, a, 0.0) for a in weights]
    contribs = [jnp.dot(weights[g].astype(BF16), vs[g], preferred_element_type=F32) for g in range(n)]
    return [psums[g] + css[g][:, 0:1] for g in range(n)], contribs


def _sb_kernel(q_ref, kn_ref, vn_ref, o_ref, kb_sc, vb_sc, kmax_sc, zb_sc, r_sc, acc_sc, *,
               rg, groups, t_new, scale):
    qi = pl.program_id(2)
    sweep_left = t_new > rg

    @pl.when(qi == 0)
    def _():
        kb_sc[0:rg, :] = jnp.zeros((rg, kb_sc.shape[1]), BF16)
        vb_sc[0:rg, :] = jnp.zeros((rg, vb_sc.shape[1]), BF16)
        ch = min(t_new, 512)

        def conv(i, kmax2):
            src = pl.ds(pl.multiple_of(i * ch, ch), ch)
            dst = pl.ds(pl.multiple_of(rg + i * ch, rg), ch)
            kb = kn_ref[0, src, :].astype(BF16)
            kb_sc[dst, :] = kb
            vb_sc[dst, :] = vn_ref[0, src, :].astype(BF16)
            kf = kb.astype(F32)
            n2 = jnp.sum(kf * kf, axis=1, keepdims=True)
            return jnp.maximum(kmax2, jnp.max(n2, axis=0, keepdims=True))

        kmax2 = lax.fori_loop(0, t_new // ch, conv, jnp.zeros((1, 1), F32))
        kmax_sc[...] = jnp.broadcast_to(kmax2, kmax_sc.shape)

    u_mat = _suffix_ones(rg)
    tiles = functools.partial(_sb_tiles, scale=scale)
    gslice = lambda g: slice(g * rg, (g + 1) * rg)
    qs = [q_ref[0, gslice(g), :] for g in range(groups)]
    first_tile = qi * groups

    def tile_rows(idx):
        return pl.ds(pl.multiple_of((jnp.maximum(idx, -1) + 1) * rg, rg), rg)

    def any_group_continues(idxs, psums):
        go = None
        for g in range(groups):
            go_g = (idxs[g] >= 1) & (jnp.max(zb_sc[gslice(g), :] - psums[g]) > SB_LOG_WEIGHT_CUTOFF)
            go = go_g if go is None else go | go_g
        return go.astype(jnp.int32)

    idxs = [first_tile + g for g in range(groups)]
    psums, contribs = tiles(qs, [kb_sc[tile_rows(i), :] for i in idxs], [vb_sc[tile_rows(i), :] for i in idxs],
                            u_mat, [jnp.zeros((rg, 1), F32)] * groups, masked=True)
    for g in range(groups):
        acc_sc[gslice(g), :] = contribs[g]
        r_sc[gslice(g), :] = psums[g]

    if sweep_left:
        kmax = jnp.sqrt(kmax_sc[0:1, 0:1]) * (scale * 1.01)
        for g in range(groups):
            qf = qs[g].astype(F32)
            zb_sc[gslice(g), :] = jnp.sqrt(jnp.sum(qf * qf, axis=1, keepdims=True)) * kmax

        def body(carry):
            dist, _ = carry
            idxs = [first_tile + g - dist for g in range(groups)]
            psums, contribs = tiles(qs, [kb_sc[tile_rows(i), :] for i in idxs],
                                    [vb_sc[tile_rows(i), :] for i in idxs], u_mat,
                                    [r_sc[gslice(g), :] for g in range(groups)], masked=False)
            for g in range(groups):
                acc_sc[gslice(g), :] += contribs[g]
                r_sc[gslice(g), :] = psums[g]
            return dist + 1, any_group_continues(idxs, psums)

        lax.while_loop(lambda carry: carry[1] != 0, body, (jnp.int32(1), any_group_continues(idxs, psums)))

    o_ref[0] = acc_sc[...].astype(o_ref.dtype)


def _sb_attn(q, k, v, *, heads, rg=V7X_MXU_DIM, groups=4):
    b, t, hd_all = q.shape
    d = hd_all // heads
    rg = _pick_tile(t, rg)
    groups = _pick_tile(t // rg, groups)
    bq = rg * groups
    seq = lambda bi, hi, qi: (bi, 0, hi)
    blk = lambda bi, hi, qi: (bi, qi, hi)
    return pl.pallas_call(
        functools.partial(_sb_kernel, rg=rg, groups=groups, t_new=t, scale=d ** -0.5),
        grid=(b, heads, t // bq),
        in_specs=[pl.BlockSpec((1, bq, d), blk), pl.BlockSpec((1, t, d), seq), pl.BlockSpec((1, t, d), seq)],
        out_specs=pl.BlockSpec((1, bq, d), blk),
        out_shape=jax.ShapeDtypeStruct((b, t, hd_all), BF16),
        scratch_shapes=[pltpu.VMEM((rg + t, d), BF16), pltpu.VMEM((rg + t, d), BF16),
                        pltpu.VMEM((8, V7X_LANES), F32),
                        pltpu.VMEM((bq, 1), F32), pltpu.VMEM((bq, 1), F32),
                        pltpu.VMEM((bq, d), F32)],
        compiler_params=_cparams(("parallel", "parallel", "arbitrary")),
        name="sb_attn",
    )(q, k, v)


def _sb_decode_kernel(q_ref, kn_ref, vn_ref, kp_ref, vp_ref, o_ref, *, heads, bk, scale):
    t = q_ref.shape[1]
    d = q_ref.shape[2] // heads
    past = kp_ref.shape[1] // heads
    hs = range(heads)
    cols = lambda h: slice(h * d, (h + 1) * d)
    tiles = functools.partial(_sb_tiles, scale=scale)
    qs = [q_ref[0, :, cols(h)] for h in hs]
    psums, accs = tiles(qs, [kn_ref[0, :, cols(h)].astype(BF16) for h in hs],
                        [vn_ref[0, :, cols(h)].astype(BF16) for h in hs], _suffix_ones(t),
                        [jnp.zeros((t, 1), F32)] * heads, masked=True)
    u_past = _suffix_ones(bk)
    for i in range(past // bk - 1, -1, -1):
        head_rows = lambda h: pl.ds(i * bk * heads + h, bk, stride=heads)
        psums, contribs = tiles(qs, [kp_ref[0, head_rows(h), :].astype(BF16) for h in hs],
                                [vp_ref[0, head_rows(h), :].astype(BF16) for h in hs], u_past, psums,
                                masked=False)
        accs = [a + c for a, c in zip(accs, contribs)]
    for h in hs:
        o_ref[0, :, cols(h)] = accs[h].astype(o_ref.dtype)


def _sb_decode(q, k_new, v_new, k_cache, v_cache, *, bk=V7X_MXU_DIM):
    b, t, hd_all = q.shape
    _, p, heads, d = k_cache.shape
    bk = _pick_tile(p, bk)
    new = pl.BlockSpec((1, t, hd_all), lambda bi: (bi, 0, 0))
    old = pl.BlockSpec((1, p * heads, d), lambda bi: (bi, 0, 0))
    return pl.pallas_call(
        functools.partial(_sb_decode_kernel, heads=heads, bk=bk, scale=d ** -0.5),
        grid=(b,),
        in_specs=[new, new, new, old, old],
        out_specs=new,
        out_shape=jax.ShapeDtypeStruct((b, t, hd_all), BF16),
        compiler_params=_cparams(("parallel",)),
        name="sb_decode",
    )(q, k_new, v_new, k_cache.reshape(b, p * heads, d), v_cache.reshape(b, p * heads, d))


def _gla_kernel(*refs, chunk, n_chunks, heads, has_s0, dk_scale):
    if has_s0:
        q_ref, k_ref, v_ref, r_ref, ga_ref, wa_ref, ba_ref, ng_ref, s0_ref, o_ref, s_out_ref, s_sc = refs
    else:
        q_ref, k_ref, v_ref, r_ref, ga_ref, wa_ref, ba_ref, ng_ref, o_ref, s_out_ref, s_sc = refs
    tg = pl.program_id(1)

    @pl.when(tg == 0)
    def _():
        s_sc[...] = s0_ref[...] if has_s0 else jnp.zeros_like(s_sc)

    rows = lax.broadcasted_iota(jnp.int32, (chunk, chunk), 0)
    cols = lax.broadcasted_iota(jnp.int32, (chunk, chunk), 1)
    causal = cols <= rows
    tri = jnp.where(causal, 1.0, 0.0).astype(BF16)
    ones_cols = jnp.ones((chunk, V7X_LANES), BF16)
    nb, _, dk, dv = s_sc.shape
    mid = chunk // 2 - 1
    nt_dims = (((1,), (1,)), ((), ()))
    tn_dims = (((0,), (0,)), ((), ()))
    chains = [(bb, h) for bb in range(nb) for h in range(heads)]
    cs = range(len(chains))
    kcols = lambda h: slice(h * dk, (h + 1) * dk)
    vcols = lambda h: slice(h * dv, (h + 1) * dv)

    def body(c, carry):
        sl = pl.ds(pl.multiple_of(c * chunk, chunk), chunk)
        gs = [_log_sigmoid(jnp.dot(ga_ref[bb, sl, :], wa_ref[h], preferred_element_type=F32) + ba_ref[h])
              / GLA_GATE_NORM for bb, h in chains]
        g_split = [_split_hi_lo(g) for g in gs]
        bs = [jnp.dot(tri, hi, preferred_element_type=F32) + jnp.dot(tri, lo, preferred_element_type=F32)
              for hi, lo in g_split]
        b_last_cols = [lax.dot_general(hi, ones_cols, tn_dims, preferred_element_type=F32)
                       + lax.dot_general(lo, ones_cols, tn_dims, preferred_element_type=F32)
                       for hi, lo in g_split]
        qs = [q_ref[bb, sl, kcols(h)].astype(F32) * dk_scale for bb, h in chains]
        ks = [k_ref[bb, sl, kcols(h)].astype(F32) for bb, h in chains]
        vs = [v_ref[bb, sl, vcols(h)] for bb, h in chains]
        s_prev = [s_sc[bb, h] for bb, h in chains]
        q_in = [(qs[i] * jnp.exp(bs[i])).astype(BF16) for i in cs]
        q_m = [(qs[i] * jnp.exp(bs[i] - bs[i][mid:mid + 1, :])).astype(BF16) for i in cs]
        k_m = [(ks[i] * jnp.exp(bs[i][mid:mid + 1, :] - bs[i])).astype(BF16) for i in cs]
        k_st = [(ks[i] * jnp.exp(bs[i][chunk - 1:chunk, :] - bs[i])).astype(BF16) for i in cs]
        o_inter = [jnp.dot(q_in[i], s_prev[i].astype(BF16), preferred_element_type=F32) for i in cs]
        att = [jnp.where(causal, lax.dot_general(q_m[i], k_m[i], nt_dims, preferred_element_type=F32), 0.0)
               for i in cs]
        outs = [o_inter[i] + jnp.dot(att[i].astype(BF16), vs[i], preferred_element_type=F32) for i in cs]
        for i, (bb, h) in enumerate(chains):
            decay = jnp.concatenate([jnp.exp(b_last_cols[i])] * (dv // V7X_LANES), axis=1)
            s_sc[bb, h] = decay * s_prev[i] + lax.dot_general(k_st[i], vs[i], tn_dims,
                                                             preferred_element_type=F32)
        for i, (bb, h) in enumerate(chains):
            r = r_ref[bb, sl, vcols(h)].astype(F32)
            o_ref[bb, sl, vcols(h)] = (_rms(outs[i], ng_ref[...]) * (r * jax.nn.sigmoid(r))).astype(o_ref.dtype)
        return carry

    lax.fori_loop(0, n_chunks, body, 0)

    @pl.when(tg == pl.num_programs(1) - 1)
    def _():
        s_out_ref[...] = s_sc[...]


def _gla(q, k, v, r, ga, wa, ba, ng, s0, *, chunk, tg=512, nb=2):
    b, t, _ = q.shape
    heads, _, dk = wa.shape
    dv = v.shape[-1] // heads
    chunk = min(chunk, t)
    tg = _pick_tile(t, tg)
    nb = _pick_tile(b, nb)
    assert tg % chunk == 0
    has_s0 = s0 is not None
    tok = lambda bi, ti: (bi, ti, 0)
    in_specs = [
        pl.BlockSpec((nb, tg, heads * dk), tok), pl.BlockSpec((nb, tg, heads * dk), tok),
        pl.BlockSpec((nb, tg, heads * dv), tok), pl.BlockSpec((nb, tg, heads * dv), tok),
        pl.BlockSpec((nb, tg, V7X_LANES), tok),
        _resident(wa.shape), _resident(ba.shape), _resident(ng.shape),
    ]
    args = [q, k, v, r, ga, wa, ba, ng]
    state_spec = pl.BlockSpec((nb, heads, dk, dv), lambda bi, ti: (bi, 0, 0, 0))
    if has_s0:
        in_specs.append(state_spec)
        args.append(s0)
    return pl.pallas_call(
        functools.partial(_gla_kernel, chunk=chunk, n_chunks=tg // chunk, heads=heads, has_s0=has_s0,
                          dk_scale=dk ** -0.5),
        grid=(b // nb, t // tg),
        in_specs=in_specs,
        out_specs=[pl.BlockSpec((nb, tg, heads * dv), tok), state_spec],
        out_shape=[jax.ShapeDtypeStruct((b, t, heads * dv), BF16),
                   jax.ShapeDtypeStruct((b, heads, dk, dv), F32)],
        scratch_shapes=[pltpu.VMEM((nb, heads, dk, dv), F32)],
        compiler_params=_cparams(("parallel", "arbitrary")),
        name="gla",
    )(*args)


def _mem_kernel(q_ref, k_ref, v_ref, o_ref, *, heads, scale):
    hd = q_ref.shape[-1] // heads
    per_head = len(k_ref.shape) == 4
    for h in range(heads):
        cs = slice(h * hd, (h + 1) * hd)
        q = q_ref[0, :, cs]
        k = (k_ref[0, :, h, :] if per_head else k_ref[0, :, cs]).astype(BF16)
        v = (v_ref[0, :, h, :] if per_head else v_ref[0, :, cs]).astype(BF16)
        s = lax.dot_general(q, k, (((1,), (1,)), ((), ())), preferred_element_type=F32) * scale
        e = jnp.exp(s - jnp.max(s, axis=-1, keepdims=True))
        p = e / jnp.sum(e, axis=-1, keepdims=True)
        o_ref[0, :, cs] = jnp.dot(p.astype(BF16), v, preferred_element_type=F32).astype(o_ref.dtype)


def _mem_attn(q, mk, mv, *, heads, tq=1024):
    b, t, w = q.shape
    tq = _pick_tile(t, tq)
    mem_block = (1,) + mk.shape[1:]
    mem = pl.BlockSpec(mem_block, lambda bi, ti: (bi,) + (0,) * (len(mem_block) - 1))
    return pl.pallas_call(
        functools.partial(_mem_kernel, heads=heads, scale=(w // heads) ** -0.5),
        grid=(b, t // tq),
        in_specs=[pl.BlockSpec((1, tq, w), lambda bi, ti: (bi, ti, 0)), mem, mem],
        out_specs=pl.BlockSpec((1, tq, w), lambda bi, ti: (bi, ti, 0)),
        out_shape=jax.ShapeDtypeStruct((b, t, w), BF16),
        compiler_params=_cparams(("parallel", "arbitrary")),
        name="mem_attn",
    )(q, mk, mv)


def _merge_kernel(h_ref, osb_ref, ogla_ref, omem_ref, gates_ref, wsb_ref, wgla_ref, wmem_ref, wout_ref,
                  post_g_ref, h2_ref):
    d = h_ref.shape[1]
    branches = ((osb_ref, wsb_ref), (ogla_ref, wgla_ref), (omem_ref, wmem_ref))
    merged = None
    for k, (o_ref, w_ref) in enumerate(branches):
        term = gates_ref[:, k * d:(k + 1) * d].astype(F32) * jnp.dot(o_ref[...], w_ref[...],
                                                                    preferred_element_type=F32)
        merged = term if merged is None else merged + term
    m = jnp.dot(merged.astype(BF16), wout_ref[...], preferred_element_type=F32)
    h2_ref[...] = h_ref[...] + _rms(m, post_g_ref[...])


def _merge(h, o_sb, o_gla, o_mem, gates, w_sb, w_gla, w_mem, w_out, post_g, *, tm=256):
    n, d = h.shape
    tm = _pick_tile(n, tm)
    row = lambda a: pl.BlockSpec((tm, a.shape[1]), lambda i: (i, 0))
    return pl.pallas_call(
        _merge_kernel,
        grid=(n // tm,),
        in_specs=[row(h), row(o_sb), row(o_gla), row(o_mem), row(gates),
                  _resident(w_sb.shape), _resident(w_gla.shape), _resident(w_mem.shape),
                  _resident(w_out.shape), _resident(post_g.shape)],
        out_specs=row(h),
        out_shape=jax.ShapeDtypeStruct((n, d), F32),
        compiler_params=_cparams(("parallel",)),
        name="merge",
    )(h, o_sb, o_gla, o_mem, gates, w_sb, w_gla, w_mem, w_out, post_g)


def _prep_weights(p, d):
    sb_w = d // 2
    gla_kw = d // 2
    gla_vw = d
    mem_w = d // 2
    w_in = p["w_in"]
    c = 0
    pieces = {}
    for name, width in (("sq", sb_w), ("sk", sb_w), ("sv", sb_w), ("gq", gla_kw), ("gk", gla_kw),
                        ("gv", gla_vw), ("gr", gla_vw), ("ga", GLA_LOW_RANK), ("mq", mem_w)):
        pieces[name] = w_in[:, c:c + width].astype(BF16)
        c += width
    assert c == w_in.shape[1]
    pad = V7X_LANES - GLA_LOW_RANK
    dk = gla_kw // GLA_HEADS
    w = dict(pieces)
    w["ga"] = jnp.pad(pieces["ga"], ((0, 0), (0, pad)))
    wa = jnp.pad(p["gla_w_a2"].astype(BF16), ((0, pad), (0, 0)))
    w["wa"] = wa.reshape(V7X_LANES, GLA_HEADS, dk).transpose(1, 0, 2)
    w["ba"] = p["gla_b_a2"].reshape(GLA_HEADS, 1, dk)
    for name in ("ffn1_w_gu", "ffn1_w_d", "ffn2_w_gu", "ffn2_w_d", "w_sb_br", "w_gla_br", "w_mem_br",
                 "w_gate", "w_out"):
        w[name] = p[name].astype(BF16)
    for name in ("ffn1_pre_g", "ffn1_post_g", "mix_pre_g", "mix_post_g", "ffn2_pre_g", "ffn2_post_g",
                 "gla_norm_g", "b_gate"):
        w[name] = p[name].reshape(1, -1)
    return w


def _layer(x, w, mem_k, mem_v, sb_past_k, sb_past_v, gla_s0):
    b, t, d = x.shape
    n = b * t
    sb_heads = (d // 2) // SB_HEAD_DIM
    h1, u = _ffn(x.reshape(n, d), w["ffn1_pre_g"], w["ffn1_w_gu"], w["ffn1_w_d"], w["ffn1_post_g"],
                 next_g=w["mix_pre_g"])
    sq, sk, sv, gq, gk, mq = _proj(u, [w[k] for k in ("sq", "sk", "sv", "gq", "gk", "mq")],
                                   [BF16, F32, F32, BF16, BF16, BF16])
    gv, gr, ga = _proj(u, [w["gv"], w["gr"], w["ga"]], [BF16, BF16, BF16])
    (gates,) = _proj(u, [w["w_gate"]], [BF16], biases=[w["b_gate"]], act="sigmoid")

    r3 = lambda a: a.reshape(b, t, -1)
    if sb_past_k is None:
        o_sb = _sb_attn(r3(sq), r3(sk), r3(sv), heads=sb_heads)
    else:
        o_sb = _sb_decode(r3(sq), r3(sk), r3(sv), sb_past_k, sb_past_v)
    o_gla, s_new = _gla(r3(gq), r3(gk), r3(gv), r3(gr), r3(ga), w["wa"], w["ba"], w["gla_norm_g"], gla_s0,
                        chunk=GLA_CHUNK)
    o_mem = _mem_attn(r3(mq), mem_k, mem_v, heads=MEM_HEADS)

    h2 = _merge(h1, o_sb.reshape(n, -1), o_gla.reshape(n, -1), o_mem.reshape(n, -1), gates,
                w["w_sb_br"], w["w_gla_br"], w["w_mem_br"], w["w_out"], w["mix_post_g"])
    y = _ffn(h2, w["ffn2_pre_g"], w["ffn2_w_gu"], w["ffn2_w_d"], w["ffn2_post_g"])
    return y.reshape(b, t, d), sk, sv, s_new


def kernel(x_prompt, x_sample, mem_prompt, cache_sb_k, cache_sb_v, state_gla, cache_mem_k, cache_mem_v, ffn1_pre_g, ffn1_w_gu, ffn1_w_d, ffn1_post_g, mix_pre_g, w_in, gla_w_a2, gla_b_a2, gla_norm_g, mem_norm_g, w_mem_kv, w_sb_br, w_gla_br, w_mem_br, w_gate, b_gate, w_out, mix_post_g, ffn2_pre_g, ffn2_w_gu, ffn2_w_d, ffn2_post_g):
    params = dict(ffn1_pre_g=ffn1_pre_g, ffn1_w_gu=ffn1_w_gu, ffn1_w_d=ffn1_w_d, ffn1_post_g=ffn1_post_g,
                  mix_pre_g=mix_pre_g, w_in=w_in, gla_w_a2=gla_w_a2, gla_b_a2=gla_b_a2, gla_norm_g=gla_norm_g,
                  w_sb_br=w_sb_br, w_gla_br=w_gla_br, w_mem_br=w_mem_br, w_gate=w_gate, b_gate=b_gate,
                  w_out=w_out, mix_post_g=mix_post_g, ffn2_pre_g=ffn2_pre_g, ffn2_w_gu=ffn2_w_gu,
                  ffn2_w_d=ffn2_w_d, ffn2_post_g=ffn2_post_g)
    depth = w_in.shape[0]
    bp, tp, d = x_prompt.shape
    bs, ts, _ = x_sample.shape
    m = mem_prompt.shape[1]
    mem_w = d // 2
    h_p, h_s = x_prompt, x_sample
    outs = [[] for _ in range(8)]
    for l in range(depth):
        w = _prep_weights({k: v[l] for k, v in params.items()}, d)
        w_mkv = w_mem_kv[l].astype(BF16)
        mk, mv = _proj(mem_prompt.reshape(bp * m, d), [w_mkv[:, :mem_w], w_mkv[:, mem_w:]], [F32, F32],
                       norm_g=mem_norm_g[l].reshape(1, d))
        mk = mk.reshape(bp, m, mem_w)
        mv = mv.reshape(bp, m, mem_w)
        h_p, k_p, v_p, s_p = _layer(h_p, w, mk, mv, None, None, None)
        h_s, k_s, v_s, s_s = _layer(h_s, w, cache_mem_k[l], cache_mem_v[l], cache_sb_k[l], cache_sb_v[l],
                                    state_gla[l])
        sb_heads = mem_w // SB_HEAD_DIM
        for lst, val in zip(outs, (k_p.reshape(bp, tp, sb_heads, SB_HEAD_DIM), v_p.reshape(bp, tp, sb_heads, SB_HEAD_DIM),
                                   s_p, mk.reshape(bp, m, MEM_HEADS, -1), mv.reshape(bp, m, MEM_HEADS, -1),
                                   k_s.reshape(bs, ts, sb_heads, SB_HEAD_DIM), v_s.reshape(bs, ts, sb_heads, SB_HEAD_DIM),
                                   s_s)):
            lst.append(val)
    return (h_p, h_s) + tuple(jnp.stack(o) for o in outs)
```

```python
import functools

import jax
import jax.numpy as jnp
from jax import lax
from jax.experimental import pallas as pl
from jax.experimental.pallas import tpu as pltpu

F32 = jnp.float32
BF16 = jnp.bfloat16

EPS = 1e-6
LOG2_E = 1.4426950408889634
V7X_LANES = 128
V7X_MXU_DIM = 256
V7X_VMEM_BYTES = 64 * 1024 * 1024
VMEM_LIMIT_BYTES = V7X_VMEM_BYTES - 4 * 1024 * 1024

SB_HEAD_DIM = 128
GLA_HEADS = 4
GLA_LOW_RANK = 16
GLA_GATE_NORM = 16.0
MEM_HEADS = 4
N_BRANCH = 3
GLA_CHUNK = 64
FFN_NORM_ROWS = 128
SB_LOG_WEIGHT_CUTOFF = -110.0


def _cparams(semantics):
    return pltpu.CompilerParams(dimension_semantics=semantics, vmem_limit_bytes=VMEM_LIMIT_BYTES)


def _rms(x, g):
    ms = jnp.mean(x * x, axis=-1, keepdims=True)
    return x * lax.rsqrt(ms + EPS) * g


def _log_sigmoid(x):
    return jnp.minimum(x, 0.0) - jnp.log(1.0 + jnp.exp(-jnp.abs(x)))


def _split_hi_lo(x):
    hi = x.astype(BF16)
    lo = (x - hi.astype(F32)).astype(BF16)
    return hi, lo


def _pick_tile(n, target):
    t = min(n, target)
    while n % t:
        t //= 2
    return t


def _ffn_kernel(x_ref, pre_g_ref, wg_ref, wu_ref, wd_ref, post_g_ref, *refs, emit_next):
    if emit_next:
        next_g_ref, h_ref, u_ref = refs
        xn_sc = u_ref
    else:
        h_ref, xn_sc = refs
    j = pl.program_id(1)
    tm = x_ref.shape[0]
    rc = min(tm, FFN_NORM_ROWS)

    def for_row_chunks(fn):
        def body(c, carry):
            fn(pl.ds(pl.multiple_of(c * rc, rc), rc))
            return carry
        lax.fori_loop(0, tm // rc, body, 0)

    @pl.when(j == 0)
    def _():
        def prologue(rows):
            xn_sc[rows, :] = _rms(x_ref[rows, :], pre_g_ref[...]).astype(BF16)
            h_ref[rows, :] = jnp.zeros((rc, h_ref.shape[1]), F32)
        for_row_chunks(prologue)

    for rows in (slice(0, tm // 2), slice(tm // 2, tm)):
        xn = xn_sc[rows, :]
        g = jnp.dot(xn, wg_ref[...], preferred_element_type=F32)
        u = jnp.dot(xn, wu_ref[...], preferred_element_type=F32)
        act = (g * jax.nn.sigmoid(g) * u).astype(BF16)
        h_ref[rows, :] += jnp.dot(act, wd_ref[...], preferred_element_type=F32)

    @pl.when(j == pl.num_programs(1) - 1)
    def _():
        def epilogue(rows):
            h = x_ref[rows, :] + 0.5 * _rms(h_ref[rows, :], post_g_ref[...])
            h_ref[rows, :] = h
            if emit_next:
                u_ref[rows, :] = _rms(h, next_g_ref[...]).astype(BF16)
        for_row_chunks(epilogue)


def _ffn(x, pre_g, w_gu, w_d, post_g, next_g=None, *, tm=1024, tf=512):
    emit_next = next_g is not None
    n, d = x.shape
    d_ff = w_d.shape[0]
    tm = _pick_tile(n, tm)
    tf = max(t for t in range(V7X_LANES, tf + 1, V7X_LANES) if d_ff % t == 0)
    nf = d_ff // tf
    row = lambda i, j: (i, 0)
    out_shape = [jax.ShapeDtypeStruct((n, d), F32)]
    out_specs = [pl.BlockSpec((tm, d), row)]
    in_specs = [
        pl.BlockSpec((tm, d), row),
        _resident((1, d)),
        pl.BlockSpec((d, tf), lambda i, j: (0, j)),
        pl.BlockSpec((d, tf), lambda i, j: (0, nf + j)),
        pl.BlockSpec((tf, d), lambda i, j: (j, 0)),
        _resident((1, d)),
    ]
    args = [x, pre_g, w_gu, w_gu, w_d, post_g]
    if emit_next:
        in_specs.append(_resident((1, d)))
        args.append(next_g)
        out_shape.append(jax.ShapeDtypeStruct((n, d), BF16))
        out_specs.append(pl.BlockSpec((tm, d), row))
    outs = pl.pallas_call(
        functools.partial(_ffn_kernel, emit_next=emit_next),
        grid=(n // tm, nf),
        in_specs=in_specs,
        out_specs=out_specs,
        out_shape=out_shape,
        scratch_shapes=[] if emit_next else [pltpu.VMEM((tm, d), BF16)],
        compiler_params=_cparams(("parallel", "arbitrary")),
        name="ffn",
    )(*args)
    return outs if emit_next else outs[0]


def _proj_kernel(*refs, n_out, has_bias, has_norm, act):
    it = iter(refs)
    x_ref = next(it)
    g_ref = next(it) if has_norm else None
    w_refs = [next(it) for _ in range(n_out)]
    b_refs = [next(it) for _ in range(n_out)] if has_bias else None
    o_refs = [next(it) for _ in range(n_out)]
    x = _rms(x_ref[...], g_ref[...]).astype(BF16) if has_norm else x_ref[...]
    for k in range(n_out):
        acc = jnp.dot(x, w_refs[k][...], preferred_element_type=F32)
        if has_bias:
            acc = acc + b_refs[k][...]
        if act == "sigmoid":
            acc = jax.nn.sigmoid(acc)
        o_refs[k][...] = acc.astype(o_refs[k].dtype)


def _resident(shape):
    return pl.BlockSpec(shape, lambda *_: (0,) * len(shape), pipeline_mode=pl.Buffered(1))


def _proj(x, ws, out_dtypes, *, biases=None, norm_g=None, act=None, tm=512):
    n, kdim = x.shape
    tm = _pick_tile(n, tm)
    n_out = len(ws)
    in_specs = [pl.BlockSpec((tm, kdim), lambda i: (i, 0))]
    args = [x]
    if norm_g is not None:
        in_specs.append(_resident((1, kdim)))
        args.append(norm_g)
    in_specs += [_resident(w.shape) for w in ws]
    args += list(ws)
    if biases is not None:
        in_specs += [_resident(b.shape) for b in biases]
        args += list(biases)
    return pl.pallas_call(
        functools.partial(_proj_kernel, n_out=n_out, has_bias=biases is not None,
                          has_norm=norm_g is not None, act=act),
        grid=(n // tm,),
        in_specs=in_specs,
        out_specs=[pl.BlockSpec((tm, w.shape[1]), lambda i: (i, 0)) for w in ws],
        out_shape=[jax.ShapeDtypeStruct((n, w.shape[1]), dt) for w, dt in zip(ws, out_dtypes)],
        compiler_params=_cparams(("parallel",)),
        name="proj",
    )(*args)


def _suffix_ones(n):
    r = lax.broadcasted_iota(jnp.int32, (n, n), 0)
    c = lax.broadcasted_iota(jnp.int32, (n, n), 1)
    return jnp.where(r >= c, 1.0, 0.0).astype(BF16)


def _sb_tiles(qs, ks, vs, u_mat, psums, *, scale, masked):
    n = len(qs)
    nt_dims = (((1,), (1,)), ((), ()))
    raw = [lax.dot_general(qs[g], ks[g], nt_dims, preferred_element_type=F32) for g in range(n)]
    zs = [r * scale for r in raw]
    sign = jnp.uint32(0x80000000)
    neg_abs = [lax.bitcast_convert_type(lax.bitcast_convert_type(r, jnp.uint32) | sign, F32) for r in raw]
    sps = [jnp.maximum(zs[g], 0.0) + jnp.log(1.0 + jnp.exp2(neg_abs[g] * (scale * LOG2_E))) for g in range(n)]
    if masked:
        rows = lax.broadcasted_iota(jnp.int32, zs[0].shape, 0)
        cols = lax.broadcasted_iota(jnp.int32, zs[0].shape, 1)
        mask = cols < rows
        sps = [jnp.where(mask, s, 0.0) for s in sps]
    splits = [_split_hi_lo(s) for s in sps]
    css = [jnp.dot(hi, u_mat, preferred_element_type=F32) + jnp.dot(lo, u_mat, preferred_element_type=F32)
           for hi, lo in splits]
    weights = [jnp.exp(zs[g] - css[g] - psums[g]) for g in range(n)]
    if masked:
        weights = [jnp.where(mask, a, 0.0) for a in weights]
    contribs = [jnp.dot(weights[g].astype(BF16), vs[g], preferred_element_type=F32) for g in range(n)]
    return [psums[g] + css[g][:, 0:1] for g in range(n)], contribs


def _sb_kernel(q_ref, kn_ref, vn_ref, o_ref, kb_sc, vb_sc, kmax_sc, zb_sc, r_sc, acc_sc, *,
               rg, groups, t_new, scale):
    qi = pl.program_id(2)
    sweep_left = t_new > rg

    @pl.when(qi == 0)
    def _():
        kb_sc[0:rg, :] = jnp.zeros((rg, kb_sc.shape[1]), BF16)
        vb_sc[0:rg, :] = jnp.zeros((rg, vb_sc.shape[1]), BF16)
        ch = min(t_new, 512)

        def conv(i, kmax2):
            src = pl.ds(pl.multiple_of(i * ch, ch), ch)
            dst = pl.ds(pl.multiple_of(rg + i * ch, rg), ch)
            kb = kn_ref[0, src, :].astype(BF16)
            kb_sc[dst, :] = kb
            vb_sc[dst, :] = vn_ref[0, src, :].astype(BF16)
            kf = kb.astype(F32)
            n2 = jnp.sum(kf * kf, axis=1, keepdims=True)
            return jnp.maximum(kmax2, jnp.max(n2, axis=0, keepdims=True))

        kmax2 = lax.fori_loop(0, t_new // ch, conv, jnp.zeros((1, 1), F32))
        kmax_sc[...] = jnp.broadcast_to(kmax2, kmax_sc.shape)

    u_mat = _suffix_ones(rg)
    tiles = functools.partial(_sb_tiles, scale=scale)
    gslice = lambda g: slice(g * rg, (g + 1) * rg)
    qs = [q_ref[0, gslice(g), :] for g in range(groups)]
    first_tile = qi * groups

    def tile_rows(idx):
        return pl.ds(pl.multiple_of((jnp.maximum(idx, -1) + 1) * rg, rg), rg)

    def any_group_continues(idxs, psums):
        go = None
        for g in range(groups):
            go_g = (idxs[g] >= 1) & (jnp.max(zb_sc[gslice(g), :] - psums[g]) > SB_LOG_WEIGHT_CUTOFF)
            go = go_g if go is None else go | go_g
        return go.astype(jnp.int32)

    idxs = [first_tile + g for g in range(groups)]
    psums, contribs = tiles(qs, [kb_sc[tile_rows(i), :] for i in idxs], [vb_sc[tile_rows(i), :] for i in idxs],
                            u_mat, [jnp.zeros((rg, 1), F32)] * groups, masked=True)
    for g in range(groups):
        acc_sc[gslice(g), :] = contribs[g]
        r_sc[gslice(g), :] = psums[g]

    if sweep_left:
        kmax = jnp.sqrt(kmax_sc[0:1, 0:1]) * (scale * 1.01)
        for g in range(groups):
            qf = qs[g].astype(F32)
            zb_sc[gslice(g), :] = jnp.sqrt(jnp.sum(qf * qf, axis=1, keepdims=True)) * kmax

        def body(carry):
            dist, _ = carry
            idxs = [first_tile + g - dist for g in range(groups)]
            psums, contribs = tiles(qs, [kb_sc[tile_rows(i), :] for i in idxs],
                                    [vb_sc[tile_rows(i), :] for i in idxs], u_mat,
                                    [r_sc[gslice(g), :] for g in range(groups)], masked=False)
            for g in range(groups):
                acc_sc[gslice(g), :] += contribs[g]
                r_sc[gslice(g), :] = psums[g]
            return dist + 1, any_group_continues(idxs, psums)

        lax.while_loop(lambda carry: carry[1] != 0, body, (jnp.int32(1), any_group_continues(idxs, psums)))

    o_ref[0] = acc_sc[...].astype(o_ref.dtype)


def _sb_attn(q, k, v, *, heads, rg=V7X_MXU_DIM, groups=8):
    b, t, hd_all = q.shape
    d = hd_all // heads
    rg = _pick_tile(t, rg)
    groups = _pick_tile(t // rg, groups)
    bq = rg * groups
    seq = lambda bi, hi, qi: (bi, 0, hi)
    blk = lambda bi, hi, qi: (bi, qi, hi)
    return pl.pallas_call(
        functools.partial(_sb_kernel, rg=rg, groups=groups, t_new=t, scale=d ** -0.5),
        grid=(b, heads, t // bq),
        in_specs=[pl.BlockSpec((1, bq, d), blk), pl.BlockSpec((1, t, d), seq), pl.BlockSpec((1, t, d), seq)],
        out_specs=pl.BlockSpec((1, bq, d), blk),
        out_shape=jax.ShapeDtypeStruct((b, t, hd_all), BF16),
        scratch_shapes=[pltpu.VMEM((rg + t, d), BF16), pltpu.VMEM((rg + t, d), BF16),
                        pltpu.VMEM((8, V7X_LANES), F32),
                        pltpu.VMEM((bq, 1), F32), pltpu.VMEM((bq, 1), F32),
                        pltpu.VMEM((bq, d), F32)],
        compiler_params=_cparams(("parallel", "parallel", "arbitrary")),
        name="sb_attn",
    )(q, k, v)


def _sb_decode_kernel(q_ref, kn_ref, vn_ref, kp_ref, vp_ref, o_ref, *, heads, bk, scale):
    t = q_ref.shape[1]
    d = q_ref.shape[2] // heads
    past = kp_ref.shape[1] // heads
    hs = range(heads)
    cols = lambda h: slice(h * d, (h + 1) * d)
    tiles = functools.partial(_sb_tiles, scale=scale)
    qs = [q_ref[0, :, cols(h)] for h in hs]
    psums, accs = tiles(qs, [kn_ref[0, :, cols(h)].astype(BF16) for h in hs],
                        [vn_ref[0, :, cols(h)].astype(BF16) for h in hs], _suffix_ones(t),
                        [jnp.zeros((t, 1), F32)] * heads, masked=True)
    u_past = _suffix_ones(bk)
    for i in range(past // bk - 1, -1, -1):
        head_rows = lambda h: pl.ds(i * bk * heads + h, bk, stride=heads)
        psums, contribs = tiles(qs, [kp_ref[0, head_rows(h), :].astype(BF16) for h in hs],
                                [vp_ref[0, head_rows(h), :].astype(BF16) for h in hs], u_past, psums,
                                masked=False)
        accs = [a + c for a, c in zip(accs, contribs)]
    for h in hs:
        o_ref[0, :, cols(h)] = accs[h].astype(o_ref.dtype)


def _sb_decode(q, k_new, v_new, k_cache, v_cache, *, bk=V7X_MXU_DIM):
    b, t, hd_all = q.shape
    _, p, heads, d = k_cache.shape
    bk = _pick_tile(p, bk)
    new = pl.BlockSpec((1, t, hd_all), lambda bi: (bi, 0, 0))
    old = pl.BlockSpec((1, p * heads, d), lambda bi: (bi, 0, 0))
    return pl.pallas_call(
        functools.partial(_sb_decode_kernel, heads=heads, bk=bk, scale=d ** -0.5),
        grid=(b,),
        in_specs=[new, new, new, old, old],
        out_specs=new,
        out_shape=jax.ShapeDtypeStruct((b, t, hd_all), BF16),
        compiler_params=_cparams(("parallel",)),
        name="sb_decode",
    )(q, k_new, v_new, k_cache.reshape(b, p * heads, d), v_cache.reshape(b, p * heads, d))


def _gla_kernel(*refs, chunk, n_chunks, heads, has_s0, dk_scale):
    if has_s0:
        q_ref, k_ref, v_ref, r_ref, ga_ref, wa_ref, ba_ref, ng_ref, s0_ref, o_ref, s_out_ref, s_sc = refs
    else:
        q_ref, k_ref, v_ref, r_ref, ga_ref, wa_ref, ba_ref, ng_ref, o_ref, s_out_ref, s_sc = refs
    tg = pl.program_id(1)

    @pl.when(tg == 0)
    def _():
        s_sc[...] = s0_ref[...] if has_s0 else jnp.zeros_like(s_sc)

    rows = lax.broadcasted_iota(jnp.int32, (chunk, chunk), 0)
    cols = lax.broadcasted_iota(jnp.int32, (chunk, chunk), 1)
    causal = cols <= rows
    tri = jnp.where(causal, 1.0, 0.0).astype(BF16)
    ones_cols = jnp.ones((chunk, V7X_LANES), BF16)
    nb, _, dk, dv = s_sc.shape
    mid = chunk // 2 - 1
    nt_dims = (((1,), (1,)), ((), ()))
    tn_dims = (((0,), (0,)), ((), ()))
    chains = [(bb, h) for bb in range(nb) for h in range(heads)]
    cs = range(len(chains))
    kcols = lambda h: slice(h * dk, (h + 1) * dk)
    vcols = lambda h: slice(h * dv, (h + 1) * dv)

    def body(c, carry):
        sl = pl.ds(pl.multiple_of(c * chunk, chunk), chunk)
        gs = [_log_sigmoid(jnp.dot(ga_ref[bb, sl, :], wa_ref[h], preferred_element_type=F32) + ba_ref[h])
              / GLA_GATE_NORM for bb, h in chains]
        g_split = [_split_hi_lo(g) for g in gs]
        bs = [jnp.dot(tri, hi, preferred_element_type=F32) + jnp.dot(tri, lo, preferred_element_type=F32)
              for hi, lo in g_split]
        b_last_cols = [lax.dot_general(hi, ones_cols, tn_dims, preferred_element_type=F32)
                       + lax.dot_general(lo, ones_cols, tn_dims, preferred_element_type=F32)
                       for hi, lo in g_split]
        qs = [q_ref[bb, sl, kcols(h)].astype(F32) * dk_scale for bb, h in chains]
        ks = [k_ref[bb, sl, kcols(h)].astype(F32) for bb, h in chains]
        vs = [v_ref[bb, sl, vcols(h)] for bb, h in chains]
        s_prev = [s_sc[bb, h] for bb, h in chains]
        q_in = [(qs[i] * jnp.exp(bs[i])).astype(BF16) for i in cs]
        q_m = [(qs[i] * jnp.exp(bs[i] - bs[i][mid:mid + 1, :])).astype(BF16) for i in cs]
        k_m = [(ks[i] * jnp.exp(bs[i][mid:mid + 1, :] - bs[i])).astype(BF16) for i in cs]
        k_st = [(ks[i] * jnp.exp(bs[i][chunk - 1:chunk, :] - bs[i])).astype(BF16) for i in cs]
        o_inter = [jnp.dot(q_in[i], s_prev[i].astype(BF16), preferred_element_type=F32) for i in cs]
        att = [jnp.where(causal, lax.dot_general(q_m[i], k_m[i], nt_dims, preferred_element_type=F32), 0.0)
               for i in cs]
        outs = [o_inter[i] + jnp.dot(att[i].astype(BF16), vs[i], preferred_element_type=F32) for i in cs]
        for i, (bb, h) in enumerate(chains):
            decay = jnp.concatenate([jnp.exp(b_last_cols[i])] * (dv // V7X_LANES), axis=1)
            s_sc[bb, h] = decay * s_prev[i] + lax.dot_general(k_st[i], vs[i], tn_dims,
                                                             preferred_element_type=F32)
        for i, (bb, h) in enumerate(chains):
            r = r_ref[bb, sl, vcols(h)].astype(F32)
            o_ref[bb, sl, vcols(h)] = (_rms(outs[i], ng_ref[...]) * (r * jax.nn.sigmoid(r))).astype(o_ref.dtype)
        return carry

    lax.fori_loop(0, n_chunks, body, 0)

    @pl.when(tg == pl.num_programs(1) - 1)
    def _():
        s_out_ref[...] = s_sc[...]


def _gla(q, k, v, r, ga, wa, ba, ng, s0, *, chunk, tg=512, nb=2):
    b, t, _ = q.shape
    heads, _, dk = wa.shape
    dv = v.shape[-1] // heads
    chunk = min(chunk, t)
    tg = _pick_tile(t, tg)
    nb = _pick_tile(b, nb)
    assert tg % chunk == 0
    has_s0 = s0 is not None
    tok = lambda bi, ti: (bi, ti, 0)
    in_specs = [
        pl.BlockSpec((nb, tg, heads * dk), tok), pl.BlockSpec((nb, tg, heads * dk), tok),
        pl.BlockSpec((nb, tg, heads * dv), tok), pl.BlockSpec((nb, tg, heads * dv), tok),
        pl.BlockSpec((nb, tg, V7X_LANES), tok),
        _resident(wa.shape), _resident(ba.shape), _resident(ng.shape),
    ]
    args = [q, k, v, r, ga, wa, ba, ng]
    state_spec = pl.BlockSpec((nb, heads, dk, dv), lambda bi, ti: (bi, 0, 0, 0))
    if has_s0:
        in_specs.append(state_spec)
        args.append(s0)
    return pl.pallas_call(
        functools.partial(_gla_kernel, chunk=chunk, n_chunks=tg // chunk, heads=heads, has_s0=has_s0,
                          dk_scale=dk ** -0.5),
        grid=(b // nb, t // tg),
        in_specs=in_specs,
        out_specs=[pl.BlockSpec((nb, tg, heads * dv), tok), state_spec],
        out_shape=[jax.ShapeDtypeStruct((b, t, heads * dv), BF16),
                   jax.ShapeDtypeStruct((b, heads, dk, dv), F32)],
        scratch_shapes=[pltpu.VMEM((nb, heads, dk, dv), F32)],
        compiler_params=_cparams(("parallel", "arbitrary")),
        name="gla",
    )(*args)


def _mem_kernel(q_ref, k_ref, v_ref, o_ref, *, heads, scale):
    hd = q_ref.shape[-1] // heads
    per_head = len(k_ref.shape) == 4
    for h in range(heads):
        cs = slice(h * hd, (h + 1) * hd)
        q = q_ref[0, :, cs]
        k = (k_ref[0, :, h, :] if per_head else k_ref[0, :, cs]).astype(BF16)
        v = (v_ref[0, :, h, :] if per_head else v_ref[0, :, cs]).astype(BF16)
        s = lax.dot_general(q, k, (((1,), (1,)), ((), ())), preferred_element_type=F32) * scale
        e = jnp.exp(s - jnp.max(s, axis=-1, keepdims=True))
        p = e / jnp.sum(e, axis=-1, keepdims=True)
        o_ref[0, :, cs] = jnp.dot(p.astype(BF16), v, preferred_element_type=F32).astype(o_ref.dtype)


def _mem_attn(q, mk, mv, *, heads, tq=1024):
    b, t, w = q.shape
    tq = _pick_tile(t, tq)
    mem_block = (1,) + mk.shape[1:]
    mem = pl.BlockSpec(mem_block, lambda bi, ti: (bi,) + (0,) * (len(mem_block) - 1))
    return pl.pallas_call(
        functools.partial(_mem_kernel, heads=heads, scale=(w // heads) ** -0.5),
        grid=(b, t // tq),
        in_specs=[pl.BlockSpec((1, tq, w), lambda bi, ti: (bi, ti, 0)), mem, mem],
        out_specs=pl.BlockSpec((1, tq, w), lambda bi, ti: (bi, ti, 0)),
        out_shape=jax.ShapeDtypeStruct((b, t, w), BF16),
        compiler_params=_cparams(("parallel", "arbitrary")),
        name="mem_attn",
    )(q, mk, mv)


def _merge_kernel(h_ref, osb_ref, ogla_ref, omem_ref, gates_ref, wsb_ref, wgla_ref, wmem_ref, wout_ref,
                  post_g_ref, h2_ref):
    d = h_ref.shape[1]
    branches = ((osb_ref, wsb_ref), (ogla_ref, wgla_ref), (omem_ref, wmem_ref))
    merged = None
    for k, (o_ref, w_ref) in enumerate(branches):
        term = gates_ref[:, k * d:(k + 1) * d].astype(F32) * jnp.dot(o_ref[...], w_ref[...],
                                                                    preferred_element_type=F32)
        merged = term if merged is None else merged + term
    m = jnp.dot(merged.astype(BF16), wout_ref[...], preferred_element_type=F32)
    h2_ref[...] = h_ref[...] + _rms(m, post_g_ref[...])


def _merge(h, o_sb, o_gla, o_mem, gates, w_sb, w_gla, w_mem, w_out, post_g, *, tm=256):
    n, d = h.shape
    tm = _pick_tile(n, tm)
    row = lambda a: pl.BlockSpec((tm, a.shape[1]), lambda i: (i, 0))
    return pl.pallas_call(
        _merge_kernel,
        grid=(n // tm,),
        in_specs=[row(h), row(o_sb), row(o_gla), row(o_mem), row(gates),
                  _resident(w_sb.shape), _resident(w_gla.shape), _resident(w_mem.shape),
                  _resident(w_out.shape), _resident(post_g.shape)],
        out_specs=row(h),
        out_shape=jax.ShapeDtypeStruct((n, d), F32),
        compiler_params=_cparams(("parallel",)),
        name="merge",
    )(h, o_sb, o_gla, o_mem, gates, w_sb, w_gla, w_mem, w_out, post_g)


def _prep_weights(p, d):
    sb_w = d // 2
    gla_kw = d // 2
    gla_vw = d
    mem_w = d // 2
    w_in = p["w_in"]
    c = 0
    pieces = {}
    for name, width in (("sq", sb_w), ("sk", sb_w), ("sv", sb_w), ("gq", gla_kw), ("gk", gla_kw),
                        ("gv", gla_vw), ("gr", gla_vw), ("ga", GLA_LOW_RANK), ("mq", mem_w)):
        pieces[name] = w_in[:, c:c + width].astype(BF16)
        c += width
    assert c == w_in.shape[1]
    pad = V7X_LANES - GLA_LOW_RANK
    dk = gla_kw // GLA_HEADS
    w = dict(pieces)
    w["ga"] = jnp.pad(pieces["ga"], ((0, 0), (0, pad)))
    wa = jnp.pad(p["gla_w_a2"].astype(BF16), ((0, pad), (0, 0)))
    w["wa"] = wa.reshape(V7X_LANES, GLA_HEADS, dk).transpose(1, 0, 2)
    w["ba"] = p["gla_b_a2"].reshape(GLA_HEADS, 1, dk)
    for name in ("ffn1_w_gu", "ffn1_w_d", "ffn2_w_gu", "ffn2_w_d", "w_sb_br", "w_gla_br", "w_mem_br",
                 "w_gate", "w_out"):
        w[name] = p[name].astype(BF16)
    for name in ("ffn1_pre_g", "ffn1_post_g", "mix_pre_g", "mix_post_g", "ffn2_pre_g", "ffn2_post_g",
                 "gla_norm_g", "b_gate"):
        w[name] = p[name].reshape(1, -1)
    return w


def _layer(x, w, mem_k, mem_v, sb_past_k, sb_past_v, gla_s0):
    b, t, d = x.shape
    n = b * t
    sb_heads = (d // 2) // SB_HEAD_DIM
    h1, u = _ffn(x.reshape(n, d), w["ffn1_pre_g"], w["ffn1_w_gu"], w["ffn1_w_d"], w["ffn1_post_g"],
                 next_g=w["mix_pre_g"])
    sq, sk, sv, gq, gk, mq = _proj(u, [w[k] for k in ("sq", "sk", "sv", "gq", "gk", "mq")],
                                   [BF16, F32, F32, BF16, BF16, BF16])
    gv, gr, ga = _proj(u, [w["gv"], w["gr"], w["ga"]], [BF16, BF16, BF16])
    (gates,) = _proj(u, [w["w_gate"]], [BF16], biases=[w["b_gate"]], act="sigmoid")

    r3 = lambda a: a.reshape(b, t, -1)
    if sb_past_k is None:
        o_sb = _sb_attn(r3(sq), r3(sk), r3(sv), heads=sb_heads)
    else:
        o_sb = _sb_decode(r3(sq), r3(sk), r3(sv), sb_past_k, sb_past_v)
    o_gla, s_new = _gla(r3(gq), r3(gk), r3(gv), r3(gr), r3(ga), w["wa"], w["ba"], w["gla_norm_g"], gla_s0,
                        chunk=GLA_CHUNK)
    o_mem = _mem_attn(r3(mq), mem_k, mem_v, heads=MEM_HEADS)

    h2 = _merge(h1, o_sb.reshape(n, -1), o_gla.reshape(n, -1), o_mem.reshape(n, -1), gates,
                w["w_sb_br"], w["w_gla_br"], w["w_mem_br"], w["w_out"], w["mix_post_g"])
    y = _ffn(h2, w["ffn2_pre_g"], w["ffn2_w_gu"], w["ffn2_w_d"], w["ffn2_post_g"])
    return y.reshape(b, t, d), sk, sv, s_new


def kernel(x_prompt, x_sample, mem_prompt, cache_sb_k, cache_sb_v, state_gla, cache_mem_k, cache_mem_v, ffn1_pre_g, ffn1_w_gu, ffn1_w_d, ffn1_post_g, mix_pre_g, w_in, gla_w_a2, gla_b_a2, gla_norm_g, mem_norm_g, w_mem_kv, w_sb_br, w_gla_br, w_mem_br, w_gate, b_gate, w_out, mix_post_g, ffn2_pre_g, ffn2_w_gu, ffn2_w_d, ffn2_post_g):
    params = dict(ffn1_pre_g=ffn1_pre_g, ffn1_w_gu=ffn1_w_gu, ffn1_w_d=ffn1_w_d, ffn1_post_g=ffn1_post_g,
                  mix_pre_g=mix_pre_g, w_in=w_in, gla_w_a2=gla_w_a2, gla_b_a2=gla_b_a2, gla_norm_g=gla_norm_g,
                  w_sb_br=w_sb_br, w_gla_br=w_gla_br, w_mem_br=w_mem_br, w_gate=w_gate, b_gate=b_gate,
                  w_out=w_out, mix_post_g=mix_post_g, ffn2_pre_g=ffn2_pre_g, ffn2_w_gu=ffn2_w_gu,
                  ffn2_w_d=ffn2_w_d, ffn2_post_g=ffn2_post_g)
    depth = w_in.shape[0]
    bp, tp, d = x_prompt.shape
    bs, ts, _ = x_sample.shape
    m = mem_prompt.shape[1]
    mem_w = d // 2
    h_p, h_s = x_prompt, x_sample
    outs = [[] for _ in range(8)]
    for l in range(depth):
        w = _prep_weights({k: v[l] for k, v in params.items()}, d)
        w_mkv = w_mem_kv[l].astype(BF16)
        mk, mv = _proj(mem_prompt.reshape(bp * m, d), [w_mkv[:, :mem_w], w_mkv[:, mem_w:]], [F32, F32],
                       norm_g=mem_norm_g[l].reshape(1, d))
        mk = mk.reshape(bp, m, mem_w)
        mv = mv.reshape(bp, m, mem_w)
        h_p, k_p, v_p, s_p = _layer(h_p, w, mk, mv, None, None, None)
        h_s, k_s, v_s, s_s = _layer(h_s, w, cache_mem_k[l], cache_mem_v[l], cache_sb_k[l], cache_sb_v[l],
                                    state_gla[l])
        sb_heads = mem_w // SB_HEAD_DIM
        for lst, val in zip(outs, (k_p.reshape(bp, tp, sb_heads, SB_HEAD_DIM), v_p.reshape(bp, tp, sb_heads, SB_HEAD_DIM),
                                   s_p, mk.reshape(bp, m, MEM_HEADS, -1), mv.reshape(bp, m, MEM_HEADS, -1),
                                   k_s.reshape(bs, ts, sb_heads, SB_HEAD_DIM), v_s.reshape(bs, ts, sb_heads, SB_HEAD_DIM),
                                   s_s)):
            lst.append(val)
    return (h_p, h_s) + tuple(jnp.stack(o) for o in outs)
```

```python
import functools

import jax
import jax.numpy as jnp
from jax import lax
from jax.experimental import pallas as pl
from jax.experimental.pallas import tpu as pltpu

F32 = jnp.float32
BF16 = jnp.bfloat16

EPS = 1e-6
LOG2_E = 1.4426950408889634
V7X_LANES = 128
V7X_MXU_DIM = 256
V7X_VMEM_BYTES = 64 * 1024 * 1024
VMEM_LIMIT_BYTES = V7X_VMEM_BYTES - 4 * 1024 * 1024

SB_HEAD_DIM = 128
GLA_HEADS = 4
GLA_LOW_RANK = 16
GLA_GATE_NORM = 16.0
MEM_HEADS = 4
GLA_CHUNK = 64
FFN_NORM_ROWS = 128

FFN_TOKEN_TILE = 1024
FFN_FF_TILE = 512
PROJ_TOKEN_TILE = 512
MERGE_TOKEN_TILE = 256
MEM_QUERY_TILE = 1024
GLA_TOKEN_GROUP = 512
GLA_BATCH_ROWS = 2
SB_ROW_GROUPS = 8
SB_CONVERT_ROWS = 512
SB_LOG_WEIGHT_CUTOFF = -110.0
SB_BOUND_SLACK = 1.01


def _cparams(semantics):
    return pltpu.CompilerParams(dimension_semantics=semantics, vmem_limit_bytes=VMEM_LIMIT_BYTES)


def _rms(x, g):
    ms = jnp.mean(x * x, axis=-1, keepdims=True)
    return x * lax.rsqrt(ms + EPS) * g


def _log_sigmoid(x):
    return jnp.minimum(x, 0.0) - jnp.log(1.0 + jnp.exp(-jnp.abs(x)))


def _split_hi_lo(x):
    hi = x.astype(BF16)
    lo = (x - hi.astype(F32)).astype(BF16)
    return hi, lo


def _pick_tile(n, target):
    t = min(n, target)
    while n % t:
        t //= 2
    return t


def _ffn_kernel(x_ref, pre_g_ref, wg_ref, wu_ref, wd_ref, post_g_ref, *refs, emit_next):
    if emit_next:
        next_g_ref, h_ref, u_ref = refs
        xn_sc = u_ref
    else:
        h_ref, xn_sc = refs
    j = pl.program_id(1)
    tm = x_ref.shape[0]
    rc = min(tm, FFN_NORM_ROWS)

    def for_row_chunks(fn):
        def body(c, carry):
            fn(pl.ds(pl.multiple_of(c * rc, rc), rc))
            return carry
        lax.fori_loop(0, tm // rc, body, 0)

    @pl.when(j == 0)
    def _():
        def prologue(rows):
            xn_sc[rows, :] = _rms(x_ref[rows, :], pre_g_ref[...]).astype(BF16)
            h_ref[rows, :] = jnp.zeros((rc, h_ref.shape[1]), F32)
        for_row_chunks(prologue)

    for rows in (slice(0, tm // 2), slice(tm // 2, tm)):
        xn = xn_sc[rows, :]
        g = jnp.dot(xn, wg_ref[...], preferred_element_type=F32)
        u = jnp.dot(xn, wu_ref[...], preferred_element_type=F32)
        act = (g * jax.nn.sigmoid(g) * u).astype(BF16)
        h_ref[rows, :] += jnp.dot(act, wd_ref[...], preferred_element_type=F32)

    @pl.when(j == pl.num_programs(1) - 1)
    def _():
        def epilogue(rows):
            h = x_ref[rows, :] + 0.5 * _rms(h_ref[rows, :], post_g_ref[...])
            h_ref[rows, :] = h
            if emit_next:
                u_ref[rows, :] = _rms(h, next_g_ref[...]).astype(BF16)
        for_row_chunks(epilogue)


def _ffn(x, pre_g, w_gu, w_d, post_g, next_g=None, *, tm=FFN_TOKEN_TILE, tf=FFN_FF_TILE):
    emit_next = next_g is not None
    n, d = x.shape
    d_ff = w_d.shape[0]
    tm = _pick_tile(n, tm)
    tf = max(t for t in range(V7X_LANES, tf + 1, V7X_LANES) if d_ff % t == 0)
    nf = d_ff // tf
    row = lambda i, j: (i, 0)
    out_shape = [jax.ShapeDtypeStruct((n, d), F32)]
    out_specs = [pl.BlockSpec((tm, d), row)]
    in_specs = [
        pl.BlockSpec((tm, d), row),
        _resident((1, d)),
        pl.BlockSpec((d, tf), lambda i, j: (0, j)),
        pl.BlockSpec((d, tf), lambda i, j: (0, nf + j)),
        pl.BlockSpec((tf, d), lambda i, j: (j, 0)),
        _resident((1, d)),
    ]
    args = [x, pre_g, w_gu, w_gu, w_d, post_g]
    if emit_next:
        in_specs.append(_resident((1, d)))
        args.append(next_g)
        out_shape.append(jax.ShapeDtypeStruct((n, d), BF16))
        out_specs.append(pl.BlockSpec((tm, d), row))
    outs = pl.pallas_call(
        functools.partial(_ffn_kernel, emit_next=emit_next),
        grid=(n // tm, nf),
        in_specs=in_specs,
        out_specs=out_specs,
        out_shape=out_shape,
        scratch_shapes=[] if emit_next else [pltpu.VMEM((tm, d), BF16)],
        compiler_params=_cparams(("parallel", "arbitrary")),
        name="ffn",
    )(*args)
    return outs if emit_next else outs[0]


def _proj_kernel(*refs, n_out, has_bias, has_norm, act):
    it = iter(refs)
    x_ref = next(it)
    g_ref = next(it) if has_norm else None
    w_refs = [next(it) for _ in range(n_out)]
    b_refs = [next(it) for _ in range(n_out)] if has_bias else None
    o_refs = [next(it) for _ in range(n_out)]
    x = _rms(x_ref[...], g_ref[...]).astype(BF16) if has_norm else x_ref[...]
    for k in range(n_out):
        acc = jnp.dot(x, w_refs[k][...], preferred_element_type=F32)
        if has_bias:
            acc = acc + b_refs[k][...]
        if act == "sigmoid":
            acc = jax.nn.sigmoid(acc)
        o_refs[k][...] = acc.astype(o_refs[k].dtype)


def _resident(shape):
    return pl.BlockSpec(shape, lambda *_: (0,) * len(shape), pipeline_mode=pl.Buffered(1))


def _proj(x, ws, out_dtypes, *, biases=None, norm_g=None, act=None, tm=PROJ_TOKEN_TILE):
    n, kdim = x.shape
    tm = _pick_tile(n, tm)
    n_out = len(ws)
    in_specs = [pl.BlockSpec((tm, kdim), lambda i: (i, 0))]
    args = [x]
    if norm_g is not None:
        in_specs.append(_resident((1, kdim)))
        args.append(norm_g)
    in_specs += [_resident(w.shape) for w in ws]
    args += list(ws)
    if biases is not None:
        in_specs += [_resident(b.shape) for b in biases]
        args += list(biases)
    return pl.pallas_call(
        functools.partial(_proj_kernel, n_out=n_out, has_bias=biases is not None,
                          has_norm=norm_g is not None, act=act),
        grid=(n // tm,),
        in_specs=in_specs,
        out_specs=[pl.BlockSpec((tm, w.shape[1]), lambda i: (i, 0)) for w in ws],
        out_shape=[jax.ShapeDtypeStruct((n, w.shape[1]), dt) for w, dt in zip(ws, out_dtypes)],
        compiler_params=_cparams(("parallel",)),
        name="proj",
    )(*args)


def _suffix_ones(n):
    r = lax.broadcasted_iota(jnp.int32, (n, n), 0)
    c = lax.broadcasted_iota(jnp.int32, (n, n), 1)
    return jnp.where(r >= c, 1.0, 0.0).astype(BF16)


def _sb_tiles(qs, ks, vs, u_mat, psums, *, scale, masked):
    n = len(qs)
    nt_dims = (((1,), (1,)), ((), ()))
    raw = [lax.dot_general(qs[g], ks[g], nt_dims, preferred_element_type=F32) for g in range(n)]
    zs = [r * scale for r in raw]
    sign = jnp.uint32(0x80000000)
    neg_abs = [lax.bitcast_convert_type(lax.bitcast_convert_type(r, jnp.uint32) | sign, F32) for r in raw]
    sps = [jnp.maximum(zs[g], 0.0) + jnp.log(1.0 + jnp.exp2(neg_abs[g] * (scale * LOG2_E))) for g in range(n)]
    if masked:
        rows = lax.broadcasted_iota(jnp.int32, zs[0].shape, 0)
        cols = lax.broadcasted_iota(jnp.int32, zs[0].shape, 1)
        mask = cols < rows
        sps = [jnp.where(mask, s, 0.0) for s in sps]
    splits = [_split_hi_lo(s) for s in sps]
    css = [jnp.dot(hi, u_mat, preferred_element_type=F32) + jnp.dot(lo, u_mat, preferred_element_type=F32)
           for hi, lo in splits]
    weights = [jnp.exp(zs[g] - css[g] - psums[g]) for g in range(n)]
    if masked:
        weights = [jnp.where(mask, a, 0.0) for a in weights]
    contribs = [jnp.dot(weights[g].astype(BF16), vs[g], preferred_element_type=F32) for g in range(n)]
    return [psums[g] + css[g][:, 0:1] for g in range(n)], contribs


def _sb_kernel(q_ref, kn_ref, vn_ref, o_ref, kb_sc, vb_sc, kmax_sc, zb_sc, r_sc, acc_sc, *,
               rg, groups, t_new, scale):
    qi = pl.program_id(2)
    sweep_left = t_new > rg

    @pl.when(qi == 0)
    def _():
        kb_sc[0:rg, :] = jnp.zeros((rg, kb_sc.shape[1]), BF16)
        vb_sc[0:rg, :] = jnp.zeros((rg, vb_sc.shape[1]), BF16)
        ch = min(t_new, SB_CONVERT_ROWS)

        def conv(i, kmax2):
            src = pl.ds(pl.multiple_of(i * ch, ch), ch)
            dst = pl.ds(pl.multiple_of(rg + i * ch, rg), ch)
            kb = kn_ref[0, src, :].astype(BF16)
            kb_sc[dst, :] = kb
            vb_sc[dst, :] = vn_ref[0, src, :].astype(BF16)
            kf = kb.astype(F32)
            n2 = jnp.sum(kf * kf, axis=1, keepdims=True)
            return jnp.maximum(kmax2, jnp.max(n2, axis=0, keepdims=True))

        kmax2 = lax.fori_loop(0, t_new // ch, conv, jnp.zeros((1, 1), F32))
        kmax_sc[...] = jnp.broadcast_to(kmax2, kmax_sc.shape)

    u_mat = _suffix_ones(rg)
    tiles = functools.partial(_sb_tiles, scale=scale)
    gslice = lambda g: slice(g * rg, (g + 1) * rg)
    qs = [q_ref[0, gslice(g), :] for g in range(groups)]
    first_tile = qi * groups

    def tile_rows(idx):
        return pl.ds(pl.multiple_of((jnp.maximum(idx, -1) + 1) * rg, rg), rg)

    def any_group_continues(idxs, psums):
        go = None
        for g in range(groups):
            go_g = (idxs[g] >= 1) & (jnp.max(zb_sc[gslice(g), :] - psums[g]) > SB_LOG_WEIGHT_CUTOFF)
            go = go_g if go is None else go | go_g
        return go.astype(jnp.int32)

    idxs = [first_tile + g for g in range(groups)]
    psums, contribs = tiles(qs, [kb_sc[tile_rows(i), :] for i in idxs], [vb_sc[tile_rows(i), :] for i in idxs],
                            u_mat, [jnp.zeros((rg, 1), F32)] * groups, masked=True)
    for g in range(groups):
        acc_sc[gslice(g), :] = contribs[g]
        r_sc[gslice(g), :] = psums[g]

    if sweep_left:
        kmax = jnp.sqrt(kmax_sc[0:1, 0:1]) * (scale * SB_BOUND_SLACK)
        for g in range(groups):
            qf = qs[g].astype(F32)
            zb_sc[gslice(g), :] = jnp.sqrt(jnp.sum(qf * qf, axis=1, keepdims=True)) * kmax

        def body(carry):
            dist, _ = carry
            idxs = [first_tile + g - dist for g in range(groups)]
            psums, contribs = tiles(qs, [kb_sc[tile_rows(i), :] for i in idxs],
                                    [vb_sc[tile_rows(i), :] for i in idxs], u_mat,
                                    [r_sc[gslice(g), :] for g in range(groups)], masked=False)
            for g in range(groups):
                acc_sc[gslice(g), :] += contribs[g]
                r_sc[gslice(g), :] = psums[g]
            return dist + 1, any_group_continues(idxs, psums)

        lax.while_loop(lambda carry: carry[1] != 0, body, (jnp.int32(1), any_group_continues(idxs, psums)))

    o_ref[0] = acc_sc[...].astype(o_ref.dtype)


def _sb_attn(q, k, v, *, heads, rg=V7X_MXU_DIM, groups=SB_ROW_GROUPS):
    b, t, hd_all = q.shape
    d = hd_all // heads
    rg = _pick_tile(t, rg)
    groups = _pick_tile(t // rg, groups)
    bq = rg * groups
    seq = lambda bi, hi, qi: (bi, 0, hi)
    blk = lambda bi, hi, qi: (bi, qi, hi)
    return pl.pallas_call(
        functools.partial(_sb_kernel, rg=rg, groups=groups, t_new=t, scale=d ** -0.5),
        grid=(b, heads, t // bq),
        in_specs=[pl.BlockSpec((1, bq, d), blk), pl.BlockSpec((1, t, d), seq), pl.BlockSpec((1, t, d), seq)],
        out_specs=pl.BlockSpec((1, bq, d), blk),
        out_shape=jax.ShapeDtypeStruct((b, t, hd_all), BF16),
        scratch_shapes=[pltpu.VMEM((rg + t, d), BF16), pltpu.VMEM((rg + t, d), BF16),
                        pltpu.VMEM((8, V7X_LANES), F32),
                        pltpu.VMEM((bq, 1), F32), pltpu.VMEM((bq, 1), F32),
                        pltpu.VMEM((bq, d), F32)],
        compiler_params=_cparams(("parallel", "parallel", "arbitrary")),
        name="sb_attn",
    )(q, k, v)


def _sb_decode_kernel(q_ref, kn_ref, vn_ref, kp_ref, vp_ref, o_ref, *, heads, bk, scale):
    t = q_ref.shape[1]
    d = q_ref.shape[2] // heads
    past = kp_ref.shape[1] // heads
    hs = range(heads)
    cols = lambda h: slice(h * d, (h + 1) * d)
    tiles = functools.partial(_sb_tiles, scale=scale)
    qs = [q_ref[0, :, cols(h)] for h in hs]
    psums, accs = tiles(qs, [kn_ref[0, :, cols(h)].astype(BF16) for h in hs],
                        [vn_ref[0, :, cols(h)].astype(BF16) for h in hs], _suffix_ones(t),
                        [jnp.zeros((t, 1), F32)] * heads, masked=True)
    u_past = _suffix_ones(bk)
    for i in range(past // bk - 1, -1, -1):
        head_rows = lambda h: pl.ds(i * bk * heads + h, bk, stride=heads)
        psums, contribs = tiles(qs, [kp_ref[0, head_rows(h), :].astype(BF16) for h in hs],
                                [vp_ref[0, head_rows(h), :].astype(BF16) for h in hs], u_past, psums,
                                masked=False)
        accs = [a + c for a, c in zip(accs, contribs)]
    for h in hs:
        o_ref[0, :, cols(h)] = accs[h].astype(o_ref.dtype)


def _sb_decode(q, k_new, v_new, k_cache, v_cache, *, bk=V7X_MXU_DIM):
    b, t, hd_all = q.shape
    _, p, heads, d = k_cache.shape
    bk = _pick_tile(p, bk)
    new = pl.BlockSpec((1, t, hd_all), lambda bi: (bi, 0, 0))
    old = pl.BlockSpec((1, p * heads, d), lambda bi: (bi, 0, 0))
    return pl.pallas_call(
        functools.partial(_sb_decode_kernel, heads=heads, bk=bk, scale=d ** -0.5),
        grid=(b,),
        in_specs=[new, new, new, old, old],
        out_specs=new,
        out_shape=jax.ShapeDtypeStruct((b, t, hd_all), BF16),
        compiler_params=_cparams(("parallel",)),
        name="sb_decode",
    )(q, k_new, v_new, k_cache.reshape(b, p * heads, d), v_cache.reshape(b, p * heads, d))


def _gla_kernel(*refs, chunk, n_chunks, heads, has_s0, dk_scale):
    if has_s0:
        q_ref, k_ref, v_ref, r_ref, ga_ref, wa_ref, ba_ref, ng_ref, s0_ref, o_ref, s_out_ref, s_sc = refs
    else:
        q_ref, k_ref, v_ref, r_ref, ga_ref, wa_ref, ba_ref, ng_ref, o_ref, s_out_ref, s_sc = refs
    tg = pl.program_id(1)

    @pl.when(tg == 0)
    def _():
        s_sc[...] = s0_ref[...] if has_s0 else jnp.zeros_like(s_sc)

    rows = lax.broadcasted_iota(jnp.int32, (chunk, chunk), 0)
    cols = lax.broadcasted_iota(jnp.int32, (chunk, chunk), 1)
    causal = cols <= rows
    tri = jnp.where(causal, 1.0, 0.0).astype(BF16)
    ones_cols = jnp.ones((chunk, V7X_LANES), BF16)
    nb, _, dk, dv = s_sc.shape
    mid = chunk // 2 - 1
    nt_dims = (((1,), (1,)), ((), ()))
    tn_dims = (((0,), (0,)), ((), ()))
    chains = [(bb, h) for bb in range(nb) for h in range(heads)]
    cs = range(len(chains))
    kcols = lambda h: slice(h * dk, (h + 1) * dk)
    vcols = lambda h: slice(h * dv, (h + 1) * dv)

    def body(c, carry):
        sl = pl.ds(pl.multiple_of(c * chunk, chunk), chunk)
        gs = [_log_sigmoid(jnp.dot(ga_ref[bb, sl, :], wa_ref[h], preferred_element_type=F32) + ba_ref[h])
              / GLA_GATE_NORM for bb, h in chains]
        g_split = [_split_hi_lo(g) for g in gs]
        bs = [jnp.dot(tri, hi, preferred_element_type=F32) + jnp.dot(tri, lo, preferred_element_type=F32)
              for hi, lo in g_split]
        b_last_cols = [lax.dot_general(hi, ones_cols, tn_dims, preferred_element_type=F32)
                       + lax.dot_general(lo, ones_cols, tn_dims, preferred_element_type=F32)
                       for hi, lo in g_split]
        qs = [q_ref[bb, sl, kcols(h)].astype(F32) * dk_scale for bb, h in chains]
        ks = [k_ref[bb, sl, kcols(h)].astype(F32) for bb, h in chains]
        vs = [v_ref[bb, sl, vcols(h)] for bb, h in chains]
        s_prev = [s_sc[bb, h] for bb, h in chains]
        q_in = [(qs[i] * jnp.exp(bs[i])).astype(BF16) for i in cs]
        q_m = [(qs[i] * jnp.exp(bs[i] - bs[i][mid:mid + 1, :])).astype(BF16) for i in cs]
        k_m = [(ks[i] * jnp.exp(bs[i][mid:mid + 1, :] - bs[i])).astype(BF16) for i in cs]
        k_st = [(ks[i] * jnp.exp(bs[i][chunk - 1:chunk, :] - bs[i])).astype(BF16) for i in cs]
        o_inter = [jnp.dot(q_in[i], s_prev[i].astype(BF16), preferred_element_type=F32) for i in cs]
        att = [jnp.where(causal, lax.dot_general(q_m[i], k_m[i], nt_dims, preferred_element_type=F32), 0.0)
               for i in cs]
        outs = [o_inter[i] + jnp.dot(att[i].astype(BF16), vs[i], preferred_element_type=F32) for i in cs]
        for i, (bb, h) in enumerate(chains):
            decay = jnp.concatenate([jnp.exp(b_last_cols[i])] * (dv // V7X_LANES), axis=1)
            s_sc[bb, h] = decay * s_prev[i] + lax.dot_general(k_st[i], vs[i], tn_dims,
                                                             preferred_element_type=F32)
        for i, (bb, h) in enumerate(chains):
            r = r_ref[bb, sl, vcols(h)].astype(F32)
            o_ref[bb, sl, vcols(h)] = (_rms(outs[i], ng_ref[...]) * (r * jax.nn.sigmoid(r))).astype(o_ref.dtype)
        return carry

    lax.fori_loop(0, n_chunks, body, 0)

    @pl.when(tg == pl.num_programs(1) - 1)
    def _():
        s_out_ref[...] = s_sc[...]


def _gla(q, k, v, r, ga, wa, ba, ng, s0, *, chunk, tg=GLA_TOKEN_GROUP, nb=GLA_BATCH_ROWS):
    b, t, _ = q.shape
    heads, _, dk = wa.shape
    dv = v.shape[-1] // heads
    chunk = min(chunk, t)
    tg = _pick_tile(t, tg)
    nb = _pick_tile(b, nb)
    assert tg % chunk == 0
    has_s0 = s0 is not None
    tok = lambda bi, ti: (bi, ti, 0)
    in_specs = [
        pl.BlockSpec((nb, tg, heads * dk), tok), pl.BlockSpec((nb, tg, heads * dk), tok),
        pl.BlockSpec((nb, tg, heads * dv), tok), pl.BlockSpec((nb, tg, heads * dv), tok),
        pl.BlockSpec((nb, tg, V7X_LANES), tok),
        _resident(wa.shape), _resident(ba.shape), _resident(ng.shape),
    ]
    args = [q, k, v, r, ga, wa, ba, ng]
    state_spec = pl.BlockSpec((nb, heads, dk, dv), lambda bi, ti: (bi, 0, 0, 0))
    if has_s0:
        in_specs.append(state_spec)
        args.append(s0)
    return pl.pallas_call(
        functools.partial(_gla_kernel, chunk=chunk, n_chunks=tg // chunk, heads=heads, has_s0=has_s0,
                          dk_scale=dk ** -0.5),
        grid=(b // nb, t // tg),
        in_specs=in_specs,
        out_specs=[pl.BlockSpec((nb, tg, heads * dv), tok), state_spec],
        out_shape=[jax.ShapeDtypeStruct((b, t, heads * dv), BF16),
                   jax.ShapeDtypeStruct((b, heads, dk, dv), F32)],
        scratch_shapes=[pltpu.VMEM((nb, heads, dk, dv), F32)],
        compiler_params=_cparams(("parallel", "arbitrary")),
        name="gla",
    )(*args)


def _mem_kernel(q_ref, k_ref, v_ref, o_ref, *, heads, scale):
    hd = q_ref.shape[-1] // heads
    per_head = len(k_ref.shape) == 4
    for h in range(heads):
        cs = slice(h * hd, (h + 1) * hd)
        q = q_ref[0, :, cs]
        k = (k_ref[0, :, h, :] if per_head else k_ref[0, :, cs]).astype(BF16)
        v = (v_ref[0, :, h, :] if per_head else v_ref[0, :, cs]).astype(BF16)
        s = lax.dot_general(q, k, (((1,), (1,)), ((), ())), preferred_element_type=F32) * scale
        e = jnp.exp(s - jnp.max(s, axis=-1, keepdims=True))
        p = e / jnp.sum(e, axis=-1, keepdims=True)
        o_ref[0, :, cs] = jnp.dot(p.astype(BF16), v, preferred_element_type=F32).astype(o_ref.dtype)


def _mem_attn(q, mk, mv, *, heads, tq=MEM_QUERY_TILE):
    b, t, w = q.shape
    tq = _pick_tile(t, tq)
    mem_block = (1,) + mk.shape[1:]
    mem = pl.BlockSpec(mem_block, lambda bi, ti: (bi,) + (0,) * (len(mem_block) - 1))
    return pl.pallas_call(
        functools.partial(_mem_kernel, heads=heads, scale=(w // heads) ** -0.5),
        grid=(b, t // tq),
        in_specs=[pl.BlockSpec((1, tq, w), lambda bi, ti: (bi, ti, 0)), mem, mem],
        out_specs=pl.BlockSpec((1, tq, w), lambda bi, ti: (bi, ti, 0)),
        out_shape=jax.ShapeDtypeStruct((b, t, w), BF16),
        compiler_params=_cparams(("parallel", "arbitrary")),
        name="mem_attn",
    )(q, mk, mv)


def _merge_kernel(h_ref, osb_ref, ogla_ref, omem_ref, gates_ref, wsb_ref, wgla_ref, wmem_ref, wout_ref,
                  post_g_ref, h2_ref):
    d = h_ref.shape[1]
    branches = ((osb_ref, wsb_ref), (ogla_ref, wgla_ref), (omem_ref, wmem_ref))
    merged = None
    for k, (o_ref, w_ref) in enumerate(branches):
        term = gates_ref[:, k * d:(k + 1) * d].astype(F32) * jnp.dot(o_ref[...], w_ref[...],
                                                                    preferred_element_type=F32)
        merged = term if merged is None else merged + term
    m = jnp.dot(merged.astype(BF16), wout_ref[...], preferred_element_type=F32)
    h2_ref[...] = h_ref[...] + _rms(m, post_g_ref[...])


def _merge(h, o_sb, o_gla, o_mem, gates, w_sb, w_gla, w_mem, w_out, post_g, *, tm=MERGE_TOKEN_TILE):
    n, d = h.shape
    tm = _pick_tile(n, tm)
    row = lambda a: pl.BlockSpec((tm, a.shape[1]), lambda i: (i, 0))
    return pl.pallas_call(
        _merge_kernel,
        grid=(n // tm,),
        in_specs=[row(h), row(o_sb), row(o_gla), row(o_mem), row(gates),
                  _resident(w_sb.shape), _resident(w_gla.shape), _resident(w_mem.shape),
                  _resident(w_out.shape), _resident(post_g.shape)],
        out_specs=row(h),
        out_shape=jax.ShapeDtypeStruct((n, d), F32),
        compiler_params=_cparams(("parallel",)),
        name="merge",
    )(h, o_sb, o_gla, o_mem, gates, w_sb, w_gla, w_mem, w_out, post_g)


def _prep_weights(p, d):
    sb_w = d // 2
    gla_kw = d // 2
    gla_vw = d
    mem_w = d // 2
    w_in = p["w_in"]
    c = 0
    pieces = {}
    for name, width in (("sq", sb_w), ("sk", sb_w), ("sv", sb_w), ("gq", gla_kw), ("gk", gla_kw),
                        ("gv", gla_vw), ("gr", gla_vw), ("ga", GLA_LOW_RANK), ("mq", mem_w)):
        pieces[name] = w_in[:, c:c + width].astype(BF16)
        c += width
    assert c == w_in.shape[1]
    pad = V7X_LANES - GLA_LOW_RANK
    dk = gla_kw // GLA_HEADS
    w = dict(pieces)
    w["ga"] = jnp.pad(pieces["ga"], ((0, 0), (0, pad)))
    wa = jnp.pad(p["gla_w_a2"].astype(BF16), ((0, pad), (0, 0)))
    w["wa"] = wa.reshape(V7X_LANES, GLA_HEADS, dk).transpose(1, 0, 2)
    w["ba"] = p["gla_b_a2"].reshape(GLA_HEADS, 1, dk)
    for name in ("ffn1_w_gu", "ffn1_w_d", "ffn2_w_gu", "ffn2_w_d", "w_sb_br", "w_gla_br", "w_mem_br",
                 "w_gate", "w_out"):
        w[name] = p[name].astype(BF16)
    for name in ("ffn1_pre_g", "ffn1_post_g", "mix_pre_g", "mix_post_g", "ffn2_pre_g", "ffn2_post_g",
                 "gla_norm_g", "b_gate"):
        w[name] = p[name].reshape(1, -1)
    return w


def _layer(x, w, mem_k, mem_v, sb_past_k, sb_past_v, gla_s0):
    b, t, d = x.shape
    n = b * t
    sb_heads = (d // 2) // SB_HEAD_DIM
    h1, u = _ffn(x.reshape(n, d), w["ffn1_pre_g"], w["ffn1_w_gu"], w["ffn1_w_d"], w["ffn1_post_g"],
                 next_g=w["mix_pre_g"])
    sq, sk, sv, gq, gk, mq = _proj(u, [w[k] for k in ("sq", "sk", "sv", "gq", "gk", "mq")],
                                   [BF16, F32, F32, BF16, BF16, BF16])
    gv, gr, ga = _proj(u, [w["gv"], w["gr"], w["ga"]], [BF16, BF16, BF16])
    (gates,) = _proj(u, [w["w_gate"]], [BF16], biases=[w["b_gate"]], act="sigmoid")

    r3 = lambda a: a.reshape(b, t, -1)
    if sb_past_k is None:
        o_sb = _sb_attn(r3(sq), r3(sk), r3(sv), heads=sb_heads)
    else:
        o_sb = _sb_decode(r3(sq), r3(sk), r3(sv), sb_past_k, sb_past_v)
    o_gla, s_new = _gla(r3(gq), r3(gk), r3(gv), r3(gr), r3(ga), w["wa"], w["ba"], w["gla_norm_g"], gla_s0,
                        chunk=GLA_CHUNK)
    o_mem = _mem_attn(r3(mq), mem_k, mem_v, heads=MEM_HEADS)

    h2 = _merge(h1, o_sb.reshape(n, -1), o_gla.reshape(n, -1), o_mem.reshape(n, -1), gates,
                w["w_sb_br"], w["w_gla_br"], w["w_mem_br"], w["w_out"], w["mix_post_g"])
    y = _ffn(h2, w["ffn2_pre_g"], w["ffn2_w_gu"], w["ffn2_w_d"], w["ffn2_post_g"])
    return y.reshape(b, t, d), sk, sv, s_new


def kernel(x_prompt, x_sample, mem_prompt, cache_sb_k, cache_sb_v, state_gla, cache_mem_k, cache_mem_v, ffn1_pre_g, ffn1_w_gu, ffn1_w_d, ffn1_post_g, mix_pre_g, w_in, gla_w_a2, gla_b_a2, gla_norm_g, mem_norm_g, w_mem_kv, w_sb_br, w_gla_br, w_mem_br, w_gate, b_gate, w_out, mix_post_g, ffn2_pre_g, ffn2_w_gu, ffn2_w_d, ffn2_post_g):
    params = dict(ffn1_pre_g=ffn1_pre_g, ffn1_w_gu=ffn1_w_gu, ffn1_w_d=ffn1_w_d, ffn1_post_g=ffn1_post_g,
                  mix_pre_g=mix_pre_g, w_in=w_in, gla_w_a2=gla_w_a2, gla_b_a2=gla_b_a2, gla_norm_g=gla_norm_g,
                  w_sb_br=w_sb_br, w_gla_br=w_gla_br, w_mem_br=w_mem_br, w_gate=w_gate, b_gate=b_gate,
                  w_out=w_out, mix_post_g=mix_post_g, ffn2_pre_g=ffn2_pre_g, ffn2_w_gu=ffn2_w_gu,
                  ffn2_w_d=ffn2_w_d, ffn2_post_g=ffn2_post_g)
    depth = w_in.shape[0]
    bp, tp, d = x_prompt.shape
    bs, ts, _ = x_sample.shape
    m = mem_prompt.shape[1]
    mem_w = d // 2
    h_p, h_s = x_prompt, x_sample
    outs = [[] for _ in range(8)]
    for l in range(depth):
        w = _prep_weights({k: v[l] for k, v in params.items()}, d)
        w_mkv = w_mem_kv[l].astype(BF16)
        mk, mv = _proj(mem_prompt.reshape(bp * m, d), [w_mkv[:, :mem_w], w_mkv[:, mem_w:]], [F32, F32],
                       norm_g=mem_norm_g[l].reshape(1, d))
        mk = mk.reshape(bp, m, mem_w)
        mv = mv.reshape(bp, m, mem_w)
        h_p, k_p, v_p, s_p = _layer(h_p, w, mk, mv, None, None, None)
        h_s, k_s, v_s, s_s = _layer(h_s, w, cache_mem_k[l], cache_mem_v[l], cache_sb_k[l], cache_sb_v[l],
                                    state_gla[l])
        sb_heads = mem_w // SB_HEAD_DIM
        for lst, val in zip(outs, (k_p.reshape(bp, tp, sb_heads, SB_HEAD_DIM), v_p.reshape(bp, tp, sb_heads, SB_HEAD_DIM),
                                   s_p, mk.reshape(bp, m, MEM_HEADS, -1), mv.reshape(bp, m, MEM_HEADS, -1),
                                   k_s.reshape(bs, ts, sb_heads, SB_HEAD_DIM), v_s.reshape(bs, ts, sb_heads, SB_HEAD_DIM),
                                   s_s)):
            lst.append(val)
    return (h_p, h_s) + tuple(jnp.stack(o) for o in outs)
```

```python
import functools

import jax
import jax.numpy as jnp
from jax import lax
from jax.experimental import pallas as pl
from jax.experimental.pallas import tpu as pltpu

F32 = jnp.float32
BF16 = jnp.bfloat16

EPS = 1e-6
LOG2_E = 1.4426950408889634
V7X_LANES = 128
V7X_MXU_DIM = 256
V7X_VMEM_BYTES = 64 * 1024 * 1024
VMEM_LIMIT_BYTES = V7X_VMEM_BYTES - 4 * 1024 * 1024

SB_HEAD_DIM = 128
GLA_HEADS = 4
GLA_LOW_RANK = 16
GLA_GATE_NORM = 16.0
MEM_HEADS = 4
GLA_CHUNK = 64
FFN_NORM_ROWS = 128

FFN_TOKEN_TILE = 1024
FFN_FF_TILE = 512
PROJ_TOKEN_TILE = 512
MERGE_TOKEN_TILE = 256
MEM_QUERY_TILE = 1024
GLA_TOKEN_GROUP = 256
GLA_BATCH_ROWS = 4
SB_ROW_GROUPS = 16
SB_CONVERT_ROWS = 512
SB_LOG_WEIGHT_CUTOFF = -110.0
SB_BOUND_SLACK = 1.01


def _cparams(semantics):
    return pltpu.CompilerParams(dimension_semantics=semantics, vmem_limit_bytes=VMEM_LIMIT_BYTES)


def _rms(x, g):
    ms = jnp.mean(x * x, axis=-1, keepdims=True)
    return x * lax.rsqrt(ms + EPS) * g


def _log_sigmoid(x):
    return jnp.minimum(x, 0.0) - jnp.log(1.0 + jnp.exp(-jnp.abs(x)))


def _split_hi_lo(x):
    hi = x.astype(BF16)
    lo = (x - hi.astype(F32)).astype(BF16)
    return hi, lo


def _pick_tile(n, target):
    t = min(n, target)
    while n % t:
        t //= 2
    return t


def _ffn_kernel(x_ref, pre_g_ref, wg_ref, wu_ref, wd_ref, post_g_ref, *refs, emit_next):
    if emit_next:
        next_g_ref, h_ref, u_ref = refs
        xn_sc = u_ref
    else:
        h_ref, xn_sc = refs
    j = pl.program_id(1)
    tm = x_ref.shape[0]
    rc = min(tm, FFN_NORM_ROWS)

    def for_row_chunks(fn):
        def body(c, carry):
            fn(pl.ds(pl.multiple_of(c * rc, rc), rc))
            return carry
        lax.fori_loop(0, tm // rc, body, 0)

    @pl.when(j == 0)
    def _():
        def prologue(rows):
            xn_sc[rows, :] = _rms(x_ref[rows, :], pre_g_ref[...]).astype(BF16)
            h_ref[rows, :] = jnp.zeros((rc, h_ref.shape[1]), F32)
        for_row_chunks(prologue)

    for rows in (slice(0, tm // 2), slice(tm // 2, tm)):
        xn = xn_sc[rows, :]
        g = jnp.dot(xn, wg_ref[...], preferred_element_type=F32)
        u = jnp.dot(xn, wu_ref[...], preferred_element_type=F32)
        act = (g * jax.nn.sigmoid(g) * u).astype(BF16)
        h_ref[rows, :] += jnp.dot(act, wd_ref[...], preferred_element_type=F32)

    @pl.when(j == pl.num_programs(1) - 1)
    def _():
        def epilogue(rows):
            h = x_ref[rows, :] + 0.5 * _rms(h_ref[rows, :], post_g_ref[...])
            h_ref[rows, :] = h
            if emit_next:
                u_ref[rows, :] = _rms(h, next_g_ref[...]).astype(BF16)
        for_row_chunks(epilogue)


def _ffn(x, pre_g, w_gu, w_d, post_g, next_g=None, *, tm=FFN_TOKEN_TILE, tf=FFN_FF_TILE):
    emit_next = next_g is not None
    n, d = x.shape
    d_ff = w_d.shape[0]
    tm = _pick_tile(n, tm)
    tf = max(t for t in range(V7X_LANES, tf + 1, V7X_LANES) if d_ff % t == 0)
    nf = d_ff // tf
    row = lambda i, j: (i, 0)
    out_shape = [jax.ShapeDtypeStruct((n, d), F32)]
    out_specs = [pl.BlockSpec((tm, d), row)]
    in_specs = [
        pl.BlockSpec((tm, d), row),
        _resident((1, d)),
        pl.BlockSpec((d, tf), lambda i, j: (0, j)),
        pl.BlockSpec((d, tf), lambda i, j: (0, nf + j)),
        pl.BlockSpec((tf, d), lambda i, j: (j, 0)),
        _resident((1, d)),
    ]
    args = [x, pre_g, w_gu, w_gu, w_d, post_g]
    if emit_next:
        in_specs.append(_resident((1, d)))
        args.append(next_g)
        out_shape.append(jax.ShapeDtypeStruct((n, d), BF16))
        out_specs.append(pl.BlockSpec((tm, d), row))
    outs = pl.pallas_call(
        functools.partial(_ffn_kernel, emit_next=emit_next),
        grid=(n // tm, nf),
        in_specs=in_specs,
        out_specs=out_specs,
        out_shape=out_shape,
        scratch_shapes=[] if emit_next else [pltpu.VMEM((tm, d), BF16)],
        compiler_params=_cparams(("parallel", "arbitrary")),
        name="ffn",
    )(*args)
    return outs if emit_next else outs[0]


def _proj_kernel(*refs, n_out, has_bias, has_norm, act):
    it = iter(refs)
    x_ref = next(it)
    g_ref = next(it) if has_norm else None
    w_refs = [next(it) for _ in range(n_out)]
    b_refs = [next(it) for _ in range(n_out)] if has_bias else None
    o_refs = [next(it) for _ in range(n_out)]
    x = _rms(x_ref[...], g_ref[...]).astype(BF16) if has_norm else x_ref[...]
    for k in range(n_out):
        acc = jnp.dot(x, w_refs[k][...], preferred_element_type=F32)
        if has_bias:
            acc = acc + b_refs[k][...]
        if act == "sigmoid":
            acc = jax.nn.sigmoid(acc)
        o_refs[k][...] = acc.astype(o_refs[k].dtype)


def _resident(shape):
    return pl.BlockSpec(shape, lambda *_: (0,) * len(shape), pipeline_mode=pl.Buffered(1))


def _proj(x, ws, out_dtypes, *, biases=None, norm_g=None, act=None, tm=PROJ_TOKEN_TILE):
    n, kdim = x.shape
    tm = _pick_tile(n, tm)
    n_out = len(ws)
    in_specs = [pl.BlockSpec((tm, kdim), lambda i: (i, 0))]
    args = [x]
    if norm_g is not None:
        in_specs.append(_resident((1, kdim)))
        args.append(norm_g)
    in_specs += [_resident(w.shape) for w in ws]
    args += list(ws)
    if biases is not None:
        in_specs += [_resident(b.shape) for b in biases]
        args += list(biases)
    return pl.pallas_call(
        functools.partial(_proj_kernel, n_out=n_out, has_bias=biases is not None,
                          has_norm=norm_g is not None, act=act),
        grid=(n // tm,),
        in_specs=in_specs,
        out_specs=[pl.BlockSpec((tm, w.shape[1]), lambda i: (i, 0)) for w in ws],
        out_shape=[jax.ShapeDtypeStruct((n, w.shape[1]), dt) for w, dt in zip(ws, out_dtypes)],
        compiler_params=_cparams(("parallel",)),
        name="proj",
    )(*args)


def _suffix_ones(n):
    r = lax.broadcasted_iota(jnp.int32, (n, n), 0)
    c = lax.broadcasted_iota(jnp.int32, (n, n), 1)
    return jnp.where(r >= c, 1.0, 0.0).astype(BF16)


def _sb_tiles(qs, ks, vs, u_mat, psums, *, scale, masked):
    n = len(qs)
    nt_dims = (((1,), (1,)), ((), ()))
    raw = [lax.dot_general(qs[g], ks[g], nt_dims, preferred_element_type=F32) for g in range(n)]
    zs = [r * scale for r in raw]
    sign = jnp.uint32(0x80000000)
    neg_abs = [lax.bitcast_convert_type(lax.bitcast_convert_type(r, jnp.uint32) | sign, F32) for r in raw]
    sps = [jnp.maximum(zs[g], 0.0) + jnp.log(1.0 + jnp.exp2(neg_abs[g] * (scale * LOG2_E))) for g in range(n)]
    if masked:
        rows = lax.broadcasted_iota(jnp.int32, zs[0].shape, 0)
        cols = lax.broadcasted_iota(jnp.int32, zs[0].shape, 1)
        mask = cols < rows
        sps = [jnp.where(mask, s, 0.0) for s in sps]
    splits = [_split_hi_lo(s) for s in sps]
    css = [jnp.dot(hi, u_mat, preferred_element_type=F32) + jnp.dot(lo, u_mat, preferred_element_type=F32)
           for hi, lo in splits]
    weights = [jnp.exp(zs[g] - css[g] - psums[g]) for g in range(n)]
    if masked:
        weights = [jnp.where(mask, a, 0.0) for a in weights]
    contribs = [jnp.dot(weights[g].astype(BF16), vs[g], preferred_element_type=F32) for g in range(n)]
    return [psums[g] + css[g][:, 0:1] for g in range(n)], contribs


def _sb_kernel(q_ref, kn_ref, vn_ref, o_ref, kb_sc, vb_sc, kmax_sc, zb_sc, r_sc, acc_sc, *,
               rg, groups, t_new, scale):
    qi = pl.program_id(2)
    sweep_left = t_new > rg

    @pl.when(qi == 0)
    def _():
        kb_sc[0:rg, :] = jnp.zeros((rg, kb_sc.shape[1]), BF16)
        vb_sc[0:rg, :] = jnp.zeros((rg, vb_sc.shape[1]), BF16)
        ch = min(t_new, SB_CONVERT_ROWS)

        def conv(i, kmax2):
            src = pl.ds(pl.multiple_of(i * ch, ch), ch)
            dst = pl.ds(pl.multiple_of(rg + i * ch, rg), ch)
            kb = kn_ref[0, src, :].astype(BF16)
            kb_sc[dst, :] = kb
            vb_sc[dst, :] = vn_ref[0, src, :].astype(BF16)
            kf = kb.astype(F32)
            n2 = jnp.sum(kf * kf, axis=1, keepdims=True)
            return jnp.maximum(kmax2, jnp.max(n2, axis=0, keepdims=True))

        kmax2 = lax.fori_loop(0, t_new // ch, conv, jnp.zeros((1, 1), F32))
        kmax_sc[...] = jnp.broadcast_to(kmax2, kmax_sc.shape)

    u_mat = _suffix_ones(rg)
    tiles = functools.partial(_sb_tiles, scale=scale)
    gslice = lambda g: slice(g * rg, (g + 1) * rg)
    qs = [q_ref[0, gslice(g), :] for g in range(groups)]
    first_tile = qi * groups

    def tile_rows(idx):
        return pl.ds(pl.multiple_of((jnp.maximum(idx, -1) + 1) * rg, rg), rg)

    def any_group_continues(idxs, psums):
        go = None
        for g in range(groups):
            go_g = (idxs[g] >= 1) & (jnp.max(zb_sc[gslice(g), :] - psums[g]) > SB_LOG_WEIGHT_CUTOFF)
            go = go_g if go is None else go | go_g
        return go.astype(jnp.int32)

    idxs = [first_tile + g for g in range(groups)]
    psums, contribs = tiles(qs, [kb_sc[tile_rows(i), :] for i in idxs], [vb_sc[tile_rows(i), :] for i in idxs],
                            u_mat, [jnp.zeros((rg, 1), F32)] * groups, masked=True)
    for g in range(groups):
        acc_sc[gslice(g), :] = contribs[g]
        r_sc[gslice(g), :] = psums[g]

    if sweep_left:
        kmax = jnp.sqrt(kmax_sc[0:1, 0:1]) * (scale * SB_BOUND_SLACK)
        for g in range(groups):
            qf = qs[g].astype(F32)
            zb_sc[gslice(g), :] = jnp.sqrt(jnp.sum(qf * qf, axis=1, keepdims=True)) * kmax

        def body(carry):
            dist, _ = carry
            idxs = [first_tile + g - dist for g in range(groups)]
            psums, contribs = tiles(qs, [kb_sc[tile_rows(i), :] for i in idxs],
                                    [vb_sc[tile_rows(i), :] for i in idxs], u_mat,
                                    [r_sc[gslice(g), :] for g in range(groups)], masked=False)
            for g in range(groups):
                acc_sc[gslice(g), :] += contribs[g]
                r_sc[gslice(g), :] = psums[g]
            return dist + 1, any_group_continues(idxs, psums)

        lax.while_loop(lambda carry: carry[1] != 0, body, (jnp.int32(1), any_group_continues(idxs, psums)))

    o_ref[0] = acc_sc[...].astype(o_ref.dtype)


def _sb_attn(q, k, v, *, heads, rg=V7X_MXU_DIM, groups=SB_ROW_GROUPS):
    b, t, hd_all = q.shape
    d = hd_all // heads
    rg = _pick_tile(t, rg)
    groups = _pick_tile(t // rg, groups)
    bq = rg * groups
    seq = lambda bi, hi, qi: (bi, 0, hi)
    blk = lambda bi, hi, qi: (bi, qi, hi)
    return pl.pallas_call(
        functools.partial(_sb_kernel, rg=rg, groups=groups, t_new=t, scale=d ** -0.5),
        grid=(b, heads, t // bq),
        in_specs=[pl.BlockSpec((1, bq, d), blk), pl.BlockSpec((1, t, d), seq), pl.BlockSpec((1, t, d), seq)],
        out_specs=pl.BlockSpec((1, bq, d), blk),
        out_shape=jax.ShapeDtypeStruct((b, t, hd_all), BF16),
        scratch_shapes=[pltpu.VMEM((rg + t, d), BF16), pltpu.VMEM((rg + t, d), BF16),
                        pltpu.VMEM((8, V7X_LANES), F32),
                        pltpu.VMEM((bq, 1), F32), pltpu.VMEM((bq, 1), F32),
                        pltpu.VMEM((bq, d), F32)],
        compiler_params=_cparams(("parallel", "parallel", "arbitrary")),
        name="sb_attn",
    )(q, k, v)


def _sb_decode_kernel(q_ref, kn_ref, vn_ref, kp_ref, vp_ref, o_ref, *, heads, bk, scale):
    t = q_ref.shape[1]
    d = q_ref.shape[2] // heads
    past = kp_ref.shape[1] // heads
    hs = range(heads)
    cols = lambda h: slice(h * d, (h + 1) * d)
    tiles = functools.partial(_sb_tiles, scale=scale)
    qs = [q_ref[0, :, cols(h)] for h in hs]
    psums, accs = tiles(qs, [kn_ref[0, :, cols(h)].astype(BF16) for h in hs],
                        [vn_ref[0, :, cols(h)].astype(BF16) for h in hs], _suffix_ones(t),
                        [jnp.zeros((t, 1), F32)] * heads, masked=True)
    u_past = _suffix_ones(bk)
    for i in range(past // bk - 1, -1, -1):
        head_rows = lambda h: pl.ds(i * bk * heads + h, bk, stride=heads)
        psums, contribs = tiles(qs, [kp_ref[0, head_rows(h), :].astype(BF16) for h in hs],
                                [vp_ref[0, head_rows(h), :].astype(BF16) for h in hs], u_past, psums,
                                masked=False)
        accs = [a + c for a, c in zip(accs, contribs)]
    for h in hs:
        o_ref[0, :, cols(h)] = accs[h].astype(o_ref.dtype)


def _sb_decode(q, k_new, v_new, k_cache, v_cache, *, bk=V7X_MXU_DIM):
    b, t, hd_all = q.shape
    _, p, heads, d = k_cache.shape
    bk = _pick_tile(p, bk)
    new = pl.BlockSpec((1, t, hd_all), lambda bi: (bi, 0, 0))
    old = pl.BlockSpec((1, p * heads, d), lambda bi: (bi, 0, 0))
    return pl.pallas_call(
        functools.partial(_sb_decode_kernel, heads=heads, bk=bk, scale=d ** -0.5),
        grid=(b,),
        in_specs=[new, new, new, old, old],
        out_specs=new,
        out_shape=jax.ShapeDtypeStruct((b, t, hd_all), BF16),
        compiler_params=_cparams(("parallel",)),
        name="sb_decode",
    )(q, k_new, v_new, k_cache.reshape(b, p * heads, d), v_cache.reshape(b, p * heads, d))


def _gla_kernel(*refs, chunk, n_chunks, heads, has_s0, dk_scale):
    if has_s0:
        q_ref, k_ref, v_ref, r_ref, ga_ref, wa_ref, ba_ref, ng_ref, s0_ref, o_ref, s_out_ref, s_sc = refs
    else:
        q_ref, k_ref, v_ref, r_ref, ga_ref, wa_ref, ba_ref, ng_ref, o_ref, s_out_ref, s_sc = refs
    tg = pl.program_id(1)

    @pl.when(tg == 0)
    def _():
        s_sc[...] = s0_ref[...] if has_s0 else jnp.zeros_like(s_sc)

    rows = lax.broadcasted_iota(jnp.int32, (chunk, chunk), 0)
    cols = lax.broadcasted_iota(jnp.int32, (chunk, chunk), 1)
    causal = cols <= rows
    tri = jnp.where(causal, 1.0, 0.0).astype(BF16)
    ones_cols = jnp.ones((chunk, V7X_LANES), BF16)
    nb, _, dk, dv = s_sc.shape
    mid = chunk // 2 - 1
    nt_dims = (((1,), (1,)), ((), ()))
    tn_dims = (((0,), (0,)), ((), ()))
    chains = [(bb, h) for bb in range(nb) for h in range(heads)]
    cs = range(len(chains))
    kcols = lambda h: slice(h * dk, (h + 1) * dk)
    vcols = lambda h: slice(h * dv, (h + 1) * dv)

    def body(c, carry):
        sl = pl.ds(pl.multiple_of(c * chunk, chunk), chunk)
        gs = [_log_sigmoid(jnp.dot(ga_ref[bb, sl, :], wa_ref[h], preferred_element_type=F32) + ba_ref[h])
              / GLA_GATE_NORM for bb, h in chains]
        g_split = [_split_hi_lo(g) for g in gs]
        bs = [jnp.dot(tri, hi, preferred_element_type=F32) + jnp.dot(tri, lo, preferred_element_type=F32)
              for hi, lo in g_split]
        b_last_cols = [lax.dot_general(hi, ones_cols, tn_dims, preferred_element_type=F32)
                       + lax.dot_general(lo, ones_cols, tn_dims, preferred_element_type=F32)
                       for hi, lo in g_split]
        qs = [q_ref[bb, sl, kcols(h)].astype(F32) * dk_scale for bb, h in chains]
        ks = [k_ref[bb, sl, kcols(h)].astype(F32) for bb, h in chains]
        vs = [v_ref[bb, sl, vcols(h)] for bb, h in chains]
        s_prev = [s_sc[bb, h] for bb, h in chains]
        q_in = [(qs[i] * jnp.exp(bs[i])).astype(BF16) for i in cs]
        q_m = [(qs[i] * jnp.exp(bs[i] - bs[i][mid:mid + 1, :])).astype(BF16) for i in cs]
        k_m = [(ks[i] * jnp.exp(bs[i][mid:mid + 1, :] - bs[i])).astype(BF16) for i in cs]
        k_st = [(ks[i] * jnp.exp(bs[i][chunk - 1:chunk, :] - bs[i])).astype(BF16) for i in cs]
        o_inter = [jnp.dot(q_in[i], s_prev[i].astype(BF16), preferred_element_type=F32) for i in cs]
        att = [jnp.where(causal, lax.dot_general(q_m[i], k_m[i], nt_dims, preferred_element_type=F32), 0.0)
               for i in cs]
        outs = [o_inter[i] + jnp.dot(att[i].astype(BF16), vs[i], preferred_element_type=F32) for i in cs]
        for i, (bb, h) in enumerate(chains):
            decay = jnp.concatenate([jnp.exp(b_last_cols[i])] * (dv // V7X_LANES), axis=1)
            s_sc[bb, h] = decay * s_prev[i] + lax.dot_general(k_st[i], vs[i], tn_dims,
                                                             preferred_element_type=F32)
        for i, (bb, h) in enumerate(chains):
            r = r_ref[bb, sl, vcols(h)].astype(F32)
            o_ref[bb, sl, vcols(h)] = (_rms(outs[i], ng_ref[...]) * (r * jax.nn.sigmoid(r))).astype(o_ref.dtype)
        return carry

    lax.fori_loop(0, n_chunks, body, 0)

    @pl.when(tg == pl.num_programs(1) - 1)
    def _():
        s_out_ref[...] = s_sc[...]


def _gla(q, k, v, r, ga, wa, ba, ng, s0, *, chunk, tg=GLA_TOKEN_GROUP, nb=GLA_BATCH_ROWS):
    b, t, _ = q.shape
    heads, _, dk = wa.shape
    dv = v.shape[-1] // heads
    chunk = min(chunk, t)
    tg = _pick_tile(t, tg)
    nb = _pick_tile(b, nb)
    assert tg % chunk == 0
    has_s0 = s0 is not None
    tok = lambda bi, ti: (bi, ti, 0)
    in_specs = [
        pl.BlockSpec((nb, tg, heads * dk), tok), pl.BlockSpec((nb, tg, heads * dk), tok),
        pl.BlockSpec((nb, tg, heads * dv), tok), pl.BlockSpec((nb, tg, heads * dv), tok),
        pl.BlockSpec((nb, tg, V7X_LANES), tok),
        _resident(wa.shape), _resident(ba.shape), _resident(ng.shape),
    ]
    args = [q, k, v, r, ga, wa, ba, ng]
    state_spec = pl.BlockSpec((nb, heads, dk, dv), lambda bi, ti: (bi, 0, 0, 0))
    if has_s0:
        in_specs.append(state_spec)
        args.append(s0)
    return pl.pallas_call(
        functools.partial(_gla_kernel, chunk=chunk, n_chunks=tg // chunk, heads=heads, has_s0=has_s0,
                          dk_scale=dk ** -0.5),
        grid=(b // nb, t // tg),
        in_specs=in_specs,
        out_specs=[pl.BlockSpec((nb, tg, heads * dv), tok), state_spec],
        out_shape=[jax.ShapeDtypeStruct((b, t, heads * dv), BF16),
                   jax.ShapeDtypeStruct((b, heads, dk, dv), F32)],
        scratch_shapes=[pltpu.VMEM((nb, heads, dk, dv), F32)],
        compiler_params=_cparams(("parallel", "arbitrary")),
        name="gla",
    )(*args)


def _mem_kernel(q_ref, k_ref, v_ref, o_ref, *, heads, scale):
    hd = q_ref.shape[-1] // heads
    per_head = len(k_ref.shape) == 4
    for h in range(heads):
        cs = slice(h * hd, (h + 1) * hd)
        q = q_ref[0, :, cs]
        k = (k_ref[0, :, h, :] if per_head else k_ref[0, :, cs]).astype(BF16)
        v = (v_ref[0, :, h, :] if per_head else v_ref[0, :, cs]).astype(BF16)
        s = lax.dot_general(q, k, (((1,), (1,)), ((), ())), preferred_element_type=F32) * scale
        e = jnp.exp(s - jnp.max(s, axis=-1, keepdims=True))
        p = e / jnp.sum(e, axis=-1, keepdims=True)
        o_ref[0, :, cs] = jnp.dot(p.astype(BF16), v, preferred_element_type=F32).astype(o_ref.dtype)


def _mem_attn(q, mk, mv, *, heads, tq=MEM_QUERY_TILE):
    b, t, w = q.shape
    tq = _pick_tile(t, tq)
    mem_block = (1,) + mk.shape[1:]
    mem = pl.BlockSpec(mem_block, lambda bi, ti: (bi,) + (0,) * (len(mem_block) - 1))
    return pl.pallas_call(
        functools.partial(_mem_kernel, heads=heads, scale=(w // heads) ** -0.5),
        grid=(b, t // tq),
        in_specs=[pl.BlockSpec((1, tq, w), lambda bi, ti: (bi, ti, 0)), mem, mem],
        out_specs=pl.BlockSpec((1, tq, w), lambda bi, ti: (bi, ti, 0)),
        out_shape=jax.ShapeDtypeStruct((b, t, w), BF16),
        compiler_params=_cparams(("parallel", "arbitrary")),
        name="mem_attn",
    )(q, mk, mv)


def _merge_kernel(h_ref, osb_ref, ogla_ref, omem_ref, gates_ref, wsb_ref, wgla_ref, wmem_ref, wout_ref,
                  post_g_ref, h2_ref):
    d = h_ref.shape[1]
    branches = ((osb_ref, wsb_ref), (ogla_ref, wgla_ref), (omem_ref, wmem_ref))
    merged = None
    for k, (o_ref, w_ref) in enumerate(branches):
        term = gates_ref[:, k * d:(k + 1) * d].astype(F32) * jnp.dot(o_ref[...], w_ref[...],
                                                                    preferred_element_type=F32)
        merged = term if merged is None else merged + term
    m = jnp.dot(merged.astype(BF16), wout_ref[...], preferred_element_type=F32)
    h2_ref[...] = h_ref[...] + _rms(m, post_g_ref[...])


def _merge(h, o_sb, o_gla, o_mem, gates, w_sb, w_gla, w_mem, w_out, post_g, *, tm=MERGE_TOKEN_TILE):
    n, d = h.shape
    tm = _pick_tile(n, tm)
    row = lambda a: pl.BlockSpec((tm, a.shape[1]), lambda i: (i, 0))
    return pl.pallas_call(
        _merge_kernel,
        grid=(n // tm,),
        in_specs=[row(h), row(o_sb), row(o_gla), row(o_mem), row(gates),
                  _resident(w_sb.shape), _resident(w_gla.shape), _resident(w_mem.shape),
                  _resident(w_out.shape), _resident(post_g.shape)],
        out_specs=row(h),
        out_shape=jax.ShapeDtypeStruct((n, d), F32),
        compiler_params=_cparams(("parallel",)),
        name="merge",
    )(h, o_sb, o_gla, o_mem, gates, w_sb, w_gla, w_mem, w_out, post_g)


def _prep_weights(p, d):
    sb_w = d // 2
    gla_kw = d // 2
    gla_vw = d
    mem_w = d // 2
    w_in = p["w_in"]
    c = 0
    pieces = {}
    for name, width in (("sq", sb_w), ("sk", sb_w), ("sv", sb_w), ("gq", gla_kw), ("gk", gla_kw),
                        ("gv", gla_vw), ("gr", gla_vw), ("ga", GLA_LOW_RANK), ("mq", mem_w)):
        pieces[name] = w_in[:, c:c + width].astype(BF16)
        c += width
    assert c == w_in.shape[1]
    pad = V7X_LANES - GLA_LOW_RANK
    dk = gla_kw // GLA_HEADS
    w = dict(pieces)
    w["ga"] = jnp.pad(pieces["ga"], ((0, 0), (0, pad)))
    wa = jnp.pad(p["gla_w_a2"].astype(BF16), ((0, pad), (0, 0)))
    w["wa"] = wa.reshape(V7X_LANES, GLA_HEADS, dk).transpose(1, 0, 2)
    w["ba"] = p["gla_b_a2"].reshape(GLA_HEADS, 1, dk)
    for name in ("ffn1_w_gu", "ffn1_w_d", "ffn2_w_gu", "ffn2_w_d", "w_sb_br", "w_gla_br", "w_mem_br",
                 "w_gate", "w_out"):
        w[name] = p[name].astype(BF16)
    for name in ("ffn1_pre_g", "ffn1_post_g", "mix_pre_g", "mix_post_g", "ffn2_pre_g", "ffn2_post_g",
                 "gla_norm_g", "b_gate"):
        w[name] = p[name].reshape(1, -1)
    return w


def _layer(x, w, mem_k, mem_v, sb_past_k, sb_past_v, gla_s0):
    b, t, d = x.shape
    n = b * t
    sb_heads = (d // 2) // SB_HEAD_DIM
    h1, u = _ffn(x.reshape(n, d), w["ffn1_pre_g"], w["ffn1_w_gu"], w["ffn1_w_d"], w["ffn1_post_g"],
                 next_g=w["mix_pre_g"])
    sq, sk, sv, gq, gk, mq = _proj(u, [w[k] for k in ("sq", "sk", "sv", "gq", "gk", "mq")],
                                   [BF16, F32, F32, BF16, BF16, BF16])
    gv, gr, ga = _proj(u, [w["gv"], w["gr"], w["ga"]], [BF16, BF16, BF16])
    (gates,) = _proj(u, [w["w_gate"]], [BF16], biases=[w["b_gate"]], act="sigmoid")

    r3 = lambda a: a.reshape(b, t, -1)
    if sb_past_k is None:
        o_sb = _sb_attn(r3(sq), r3(sk), r3(sv), heads=sb_heads)
    else:
        o_sb = _sb_decode(r3(sq), r3(sk), r3(sv), sb_past_k, sb_past_v)
    o_gla, s_new = _gla(r3(gq), r3(gk), r3(gv), r3(gr), r3(ga), w["wa"], w["ba"], w["gla_norm_g"], gla_s0,
                        chunk=GLA_CHUNK)
    o_mem = _mem_attn(r3(mq), mem_k, mem_v, heads=MEM_HEADS)

    h2 = _merge(h1, o_sb.reshape(n, -1), o_gla.reshape(n, -1), o_mem.reshape(n, -1), gates,
                w["w_sb_br"], w["w_gla_br"], w["w_mem_br"], w["w_out"], w["mix_post_g"])
    y = _ffn(h2, w["ffn2_pre_g"], w["ffn2_w_gu"], w["ffn2_w_d"], w["ffn2_post_g"])
    return y.reshape(b, t, d), sk, sv, s_new


def kernel(x_prompt, x_sample, mem_prompt, cache_sb_k, cache_sb_v, state_gla, cache_mem_k, cache_mem_v, ffn1_pre_g, ffn1_w_gu, ffn1_w_d, ffn1_post_g, mix_pre_g, w_in, gla_w_a2, gla_b_a2, gla_norm_g, mem_norm_g, w_mem_kv, w_sb_br, w_gla_br, w_mem_br, w_gate, b_gate, w_out, mix_post_g, ffn2_pre_g, ffn2_w_gu, ffn2_w_d, ffn2_post_g):
    params = dict(ffn1_pre_g=ffn1_pre_g, ffn1_w_gu=ffn1_w_gu, ffn1_w_d=ffn1_w_d, ffn1_post_g=ffn1_post_g,
                  mix_pre_g=mix_pre_g, w_in=w_in, gla_w_a2=gla_w_a2, gla_b_a2=gla_b_a2, gla_norm_g=gla_norm_g,
                  w_sb_br=w_sb_br, w_gla_br=w_gla_br, w_mem_br=w_mem_br, w_gate=w_gate, b_gate=b_gate,
                  w_out=w_out, mix_post_g=mix_post_g, ffn2_pre_g=ffn2_pre_g, ffn2_w_gu=ffn2_w_gu,
                  ffn2_w_d=ffn2_w_d, ffn2_post_g=ffn2_post_g)
    depth = w_in.shape[0]
    bp, tp, d = x_prompt.shape
    bs, ts, _ = x_sample.shape
    m = mem_prompt.shape[1]
    mem_w = d // 2
    h_p, h_s = x_prompt, x_sample
    outs = [[] for _ in range(8)]
    for l in range(depth):
        w = _prep_weights({k: v[l] for k, v in params.items()}, d)
        w_mkv = w_mem_kv[l].astype(BF16)
        mk, mv = _proj(mem_prompt.reshape(bp * m, d), [w_mkv[:, :mem_w], w_mkv[:, mem_w:]], [F32, F32],
                       norm_g=mem_norm_g[l].reshape(1, d))
        mk = mk.reshape(bp, m, mem_w)
        mv = mv.reshape(bp, m, mem_w)
        h_p, k_p, v_p, s_p = _layer(h_p, w, mk, mv, None, None, None)
        h_s, k_s, v_s, s_s = _layer(h_s, w, cache_mem_k[l], cache_mem_v[l], cache_sb_k[l], cache_sb_v[l],
                                    state_gla[l])
        sb_heads = mem_w // SB_HEAD_DIM
        for lst, val in zip(outs, (k_p.reshape(bp, tp, sb_heads, SB_HEAD_DIM), v_p.reshape(bp, tp, sb_heads, SB_HEAD_DIM),
                                   s_p, mk.reshape(bp, m, MEM_HEADS, -1), mv.reshape(bp, m, MEM_HEADS, -1),
                                   k_s.reshape(bs, ts, sb_heads, SB_HEAD_DIM), v_s.reshape(bs, ts, sb_heads, SB_HEAD_DIM),
                                   s_s)):
            lst.append(val)
    return (h_p, h_s) + tuple(jnp.stack(o) for o in outs)
```

```python
import functools

import jax
import jax.numpy as jnp
from jax import lax
from jax.experimental import pallas as pl
from jax.experimental.pallas import tpu as pltpu

F32 = jnp.float32
BF16 = jnp.bfloat16

EPS = 1e-6
LOG2_E = 1.4426950408889634
V7X_LANES = 128
V7X_MXU_DIM = 256
V7X_VMEM_BYTES = 64 * 1024 * 1024
VMEM_LIMIT_BYTES = V7X_VMEM_BYTES - 4 * 1024 * 1024

SB_HEAD_DIM = 128
GLA_HEADS = 4
GLA_LOW_RANK = 16
GLA_GATE_NORM = 16.0
MEM_HEADS = 4
GLA_CHUNK = 64
FFN_NORM_ROWS = 128

FFN_TOKEN_TILE = 1024
FFN_FF_TILE = 512
PROJ_TOKEN_TILE = 512
MERGE_TOKEN_TILE = 256
MEM_QUERY_TILE = 1024
GLA_TOKEN_GROUP = 256
GLA_BATCH_ROWS = 4
SB_ROW_GROUPS = 16
SB_CONVERT_ROWS = 512
SB_LOG_WEIGHT_CUTOFF = -110.0
SB_BOUND_SLACK = 1.01


def _cparams(semantics):
    return pltpu.CompilerParams(dimension_semantics=semantics, vmem_limit_bytes=VMEM_LIMIT_BYTES)


def _rms(x, g):
    ms = jnp.mean(x * x, axis=-1, keepdims=True)
    return x * lax.rsqrt(ms + EPS) * g


def _log_sigmoid(x):
    return jnp.minimum(x, 0.0) - jnp.log(1.0 + jnp.exp(-jnp.abs(x)))


def _split_hi_lo(x):
    hi = x.astype(BF16)
    lo = (x - hi.astype(F32)).astype(BF16)
    return hi, lo


def _pick_tile(n, target):
    t = min(n, target)
    while n % t:
        t //= 2
    return t


def _ffn_kernel(x_ref, pre_g_ref, wg_ref, wu_ref, wd_ref, post_g_ref, *refs, emit_next):
    if emit_next:
        next_g_ref, h_ref, u_ref = refs
        xn_sc = u_ref
    else:
        h_ref, xn_sc = refs
    j = pl.program_id(1)
    tm = x_ref.shape[0]
    rc = min(tm, FFN_NORM_ROWS)

    def for_row_chunks(fn):
        def body(c, carry):
            fn(pl.ds(pl.multiple_of(c * rc, rc), rc))
            return carry
        lax.fori_loop(0, tm // rc, body, 0)

    @pl.when(j == 0)
    def _():
        def prologue(rows):
            xn_sc[rows, :] = _rms(x_ref[rows, :], pre_g_ref[...]).astype(BF16)
            h_ref[rows, :] = jnp.zeros((rc, h_ref.shape[1]), F32)
        for_row_chunks(prologue)

    for rows in (slice(0, tm // 2), slice(tm // 2, tm)):
        xn = xn_sc[rows, :]
        g = jnp.dot(xn, wg_ref[...], preferred_element_type=F32)
        u = jnp.dot(xn, wu_ref[...], preferred_element_type=F32)
        act = (g * jax.nn.sigmoid(g) * u).astype(BF16)
        h_ref[rows, :] += jnp.dot(act, wd_ref[...], preferred_element_type=F32)

    @pl.when(j == pl.num_programs(1) - 1)
    def _():
        def epilogue(rows):
            h = x_ref[rows, :] + 0.5 * _rms(h_ref[rows, :], post_g_ref[...])
            h_ref[rows, :] = h
            if emit_next:
                u_ref[rows, :] = _rms(h, next_g_ref[...]).astype(BF16)
        for_row_chunks(epilogue)


def _ffn(x, pre_g, w_gu, w_d, post_g, next_g=None, *, tm=FFN_TOKEN_TILE, tf=FFN_FF_TILE):
    emit_next = next_g is not None
    n, d = x.shape
    d_ff = w_d.shape[0]
    tm = _pick_tile(n, tm)
    tf = max(t for t in range(V7X_LANES, tf + 1, V7X_LANES) if d_ff % t == 0)
    nf = d_ff // tf
    row = lambda i, j: (i, 0)
    out_shape = [jax.ShapeDtypeStruct((n, d), F32)]
    out_specs = [pl.BlockSpec((tm, d), row)]
    in_specs = [
        pl.BlockSpec((tm, d), row),
        _resident((1, d)),
        pl.BlockSpec((d, tf), lambda i, j: (0, j)),
        pl.BlockSpec((d, tf), lambda i, j: (0, nf + j)),
        pl.BlockSpec((tf, d), lambda i, j: (j, 0)),
        _resident((1, d)),
    ]
    args = [x, pre_g, w_gu, w_gu, w_d, post_g]
    if emit_next:
        in_specs.append(_resident((1, d)))
        args.append(next_g)
        out_shape.append(jax.ShapeDtypeStruct((n, d), BF16))
        out_specs.append(pl.BlockSpec((tm, d), row))
    outs = pl.pallas_call(
        functools.partial(_ffn_kernel, emit_next=emit_next),
        grid=(n // tm, nf),
        in_specs=in_specs,
        out_specs=out_specs,
        out_shape=out_shape,
        scratch_shapes=[] if emit_next else [pltpu.VMEM((tm, d), BF16)],
        compiler_params=_cparams(("parallel", "arbitrary")),
        name="ffn",
    )(*args)
    return outs if emit_next else outs[0]


def _proj_kernel(*refs, n_out, has_bias, has_norm, act):
    it = iter(refs)
    x_ref = next(it)
    g_ref = next(it) if has_norm else None
    w_refs = [next(it) for _ in range(n_out)]
    b_refs = [next(it) for _ in range(n_out)] if has_bias else None
    o_refs = [next(it) for _ in range(n_out)]
    x = _rms(x_ref[...], g_ref[...]).astype(BF16) if has_norm else x_ref[...]
    for k in range(n_out):
        acc = jnp.dot(x, w_refs[k][...], preferred_element_type=F32)
        if has_bias:
            acc = acc + b_refs[k][...]
        if act == "sigmoid":
            acc = jax.nn.sigmoid(acc)
        o_refs[k][...] = acc.astype(o_refs[k].dtype)


def _resident(shape):
    return pl.BlockSpec(shape, lambda *_: (0,) * len(shape), pipeline_mode=pl.Buffered(1))


def _proj(x, ws, out_dtypes, *, biases=None, norm_g=None, act=None, tm=PROJ_TOKEN_TILE):
    n, kdim = x.shape
    tm = _pick_tile(n, tm)
    n_out = len(ws)
    in_specs = [pl.BlockSpec((tm, kdim), lambda i: (i, 0))]
    args = [x]
    if norm_g is not None:
        in_specs.append(_resident((1, kdim)))
        args.append(norm_g)
    in_specs += [_resident(w.shape) for w in ws]
    args += list(ws)
    if biases is not None:
        in_specs += [_resident(b.shape) for b in biases]
        args += list(biases)
    return pl.pallas_call(
        functools.partial(_proj_kernel, n_out=n_out, has_bias=biases is not None,
                          has_norm=norm_g is not None, act=act),
        grid=(n // tm,),
        in_specs=in_specs,
        out_specs=[pl.BlockSpec((tm, w.shape[1]), lambda i: (i, 0)) for w in ws],
        out_shape=[jax.ShapeDtypeStruct((n, w.shape[1]), dt) for w, dt in zip(ws, out_dtypes)],
        compiler_params=_cparams(("parallel",)),
        name="proj",
    )(*args)


def _suffix_ones(n):
    r = lax.broadcasted_iota(jnp.int32, (n, n), 0)
    c = lax.broadcasted_iota(jnp.int32, (n, n), 1)
    return jnp.where(r >= c, 1.0, 0.0).astype(BF16)


def _sb_tiles(qs, ks, vs, u_mat, psums, *, scale, masked):
    n = len(qs)
    nt_dims = (((1,), (1,)), ((), ()))
    raw = [lax.dot_general(qs[g], ks[g], nt_dims, preferred_element_type=F32) for g in range(n)]
    zs = [r * scale for r in raw]
    sign = jnp.uint32(0x80000000)
    neg_abs = [lax.bitcast_convert_type(lax.bitcast_convert_type(r, jnp.uint32) | sign, F32) for r in raw]
    sps = [jnp.maximum(zs[g], 0.0) + jnp.log(1.0 + jnp.exp2(neg_abs[g] * (scale * LOG2_E))) for g in range(n)]
    if masked:
        rows = lax.broadcasted_iota(jnp.int32, zs[0].shape, 0)
        cols = lax.broadcasted_iota(jnp.int32, zs[0].shape, 1)
        mask = cols < rows
        sps = [jnp.where(mask, s, 0.0) for s in sps]
    splits = [_split_hi_lo(s) for s in sps]
    css = [jnp.dot(hi, u_mat, preferred_element_type=F32) + jnp.dot(lo, u_mat, preferred_element_type=F32)
           for hi, lo in splits]
    weights = [jnp.exp(zs[g] - css[g] - psums[g]) for g in range(n)]
    if masked:
        weights = [jnp.where(mask, a, 0.0) for a in weights]
    contribs = [jnp.dot(weights[g].astype(BF16), vs[g], preferred_element_type=F32) for g in range(n)]
    return [psums[g] + css[g][:, 0:1] for g in range(n)], contribs


def _sb_kernel(q_ref, kn_ref, vn_ref, o_ref, kb_sc, vb_sc, kmax_sc, zb_sc, r_sc, acc_sc, *,
               rg, groups, t_new, scale):
    qi = pl.program_id(2)
    sweep_left = t_new > rg

    @pl.when(qi == 0)
    def _():
        kb_sc[0:rg, :] = jnp.zeros((rg, kb_sc.shape[1]), BF16)
        vb_sc[0:rg, :] = jnp.zeros((rg, vb_sc.shape[1]), BF16)
        ch = min(t_new, SB_CONVERT_ROWS)

        def conv(i, kmax2):
            src = pl.ds(pl.multiple_of(i * ch, ch), ch)
            dst = pl.ds(pl.multiple_of(rg + i * ch, rg), ch)
            kb = kn_ref[0, src, :].astype(BF16)
            kb_sc[dst, :] = kb
            vb_sc[dst, :] = vn_ref[0, src, :].astype(BF16)
            kf = kb.astype(F32)
            n2 = jnp.dot((kf * kf).astype(BF16), jnp.ones((kf.shape[1], V7X_LANES), BF16),
                         preferred_element_type=F32)
            return jnp.maximum(kmax2, jnp.max(jnp.max(n2, axis=0, keepdims=True), axis=1, keepdims=True))

        kmax2 = lax.fori_loop(0, t_new // ch, conv, jnp.zeros((1, 1), F32))
        kmax_sc[...] = jnp.broadcast_to(kmax2, kmax_sc.shape)

    u_mat = _suffix_ones(rg)
    tiles = functools.partial(_sb_tiles, scale=scale)
    gslice = lambda g: slice(g * rg, (g + 1) * rg)
    qs = [q_ref[0, gslice(g), :] for g in range(groups)]
    first_tile = qi * groups

    def tile_rows(idx):
        return pl.ds(pl.multiple_of((jnp.maximum(idx, -1) + 1) * rg, rg), rg)

    def any_group_continues(idxs, psums):
        go = None
        for g in range(groups):
            go_g = (idxs[g] >= 1) & (jnp.max(zb_sc[gslice(g), :] - psums[g]) > SB_LOG_WEIGHT_CUTOFF)
            go = go_g if go is None else go | go_g
        return go.astype(jnp.int32)

    idxs = [first_tile + g for g in range(groups)]
    psums, contribs = tiles(qs, [kb_sc[tile_rows(i), :] for i in idxs], [vb_sc[tile_rows(i), :] for i in idxs],
                            u_mat, [jnp.zeros((rg, 1), F32)] * groups, masked=True)
    for g in range(groups):
        acc_sc[gslice(g), :] = contribs[g]
        r_sc[gslice(g), :] = psums[g]

    if sweep_left:
        kmax = jnp.sqrt(kmax_sc[0:1, 0:1]) * (scale * SB_BOUND_SLACK)
        for g in range(groups):
            qf = qs[g].astype(F32)
            zb_sc[gslice(g), :] = jnp.sqrt(jnp.sum(qf * qf, axis=1, keepdims=True)) * kmax

        def body(carry):
            dist, _ = carry
            idxs = [first_tile + g - dist for g in range(groups)]
            psums, contribs = tiles(qs, [kb_sc[tile_rows(i), :] for i in idxs],
                                    [vb_sc[tile_rows(i), :] for i in idxs], u_mat,
                                    [r_sc[gslice(g), :] for g in range(groups)], masked=False)
            for g in range(groups):
                acc_sc[gslice(g), :] += contribs[g]
                r_sc[gslice(g), :] = psums[g]
            return dist + 1, any_group_continues(idxs, psums)

        lax.while_loop(lambda carry: carry[1] != 0, body, (jnp.int32(1), any_group_continues(idxs, psums)))

    o_ref[0] = acc_sc[...].astype(o_ref.dtype)


def _sb_attn(q, k, v, *, heads, rg=V7X_MXU_DIM, groups=SB_ROW_GROUPS):
    b, t, hd_all = q.shape
    d = hd_all // heads
    rg = _pick_tile(t, rg)
    groups = _pick_tile(t // rg, groups)
    bq = rg * groups
    seq = lambda bi, hi, qi: (bi, 0, hi)
    blk = lambda bi, hi, qi: (bi, qi, hi)
    return pl.pallas_call(
        functools.partial(_sb_kernel, rg=rg, groups=groups, t_new=t, scale=d ** -0.5),
        grid=(b, heads, t // bq),
        in_specs=[pl.BlockSpec((1, bq, d), blk), pl.BlockSpec((1, t, d), seq), pl.BlockSpec((1, t, d), seq)],
        out_specs=pl.BlockSpec((1, bq, d), blk),
        out_shape=jax.ShapeDtypeStruct((b, t, hd_all), BF16),
        scratch_shapes=[pltpu.VMEM((rg + t, d), BF16), pltpu.VMEM((rg + t, d), BF16),
                        pltpu.VMEM((8, V7X_LANES), F32),
                        pltpu.VMEM((bq, 1), F32), pltpu.VMEM((bq, 1), F32),
                        pltpu.VMEM((bq, d), F32)],
        compiler_params=_cparams(("parallel", "parallel", "arbitrary")),
        name="sb_attn",
    )(q, k, v)


def _sb_decode_kernel(q_ref, kn_ref, vn_ref, kp_ref, vp_ref, o_ref, *, heads, bk, scale):
    t = q_ref.shape[1]
    d = q_ref.shape[2] // heads
    past = kp_ref.shape[1] // heads
    hs = range(heads)
    cols = lambda h: slice(h * d, (h + 1) * d)
    tiles = functools.partial(_sb_tiles, scale=scale)
    qs = [q_ref[0, :, cols(h)] for h in hs]
    psums, accs = tiles(qs, [kn_ref[0, :, cols(h)].astype(BF16) for h in hs],
                        [vn_ref[0, :, cols(h)].astype(BF16) for h in hs], _suffix_ones(t),
                        [jnp.zeros((t, 1), F32)] * heads, masked=True)
    u_past = _suffix_ones(bk)
    for i in range(past // bk - 1, -1, -1):
        head_rows = lambda h: pl.ds(i * bk * heads + h, bk, stride=heads)
        psums, contribs = tiles(qs, [kp_ref[0, head_rows(h), :].astype(BF16) for h in hs],
                                [vp_ref[0, head_rows(h), :].astype(BF16) for h in hs], u_past, psums,
                                masked=False)
        accs = [a + c for a, c in zip(accs, contribs)]
    for h in hs:
        o_ref[0, :, cols(h)] = accs[h].astype(o_ref.dtype)


def _sb_decode(q, k_new, v_new, k_cache, v_cache, *, bk=V7X_MXU_DIM):
    b, t, hd_all = q.shape
    _, p, heads, d = k_cache.shape
    bk = _pick_tile(p, bk)
    new = pl.BlockSpec((1, t, hd_all), lambda bi: (bi, 0, 0))
    old = pl.BlockSpec((1, p * heads, d), lambda bi: (bi, 0, 0))
    return pl.pallas_call(
        functools.partial(_sb_decode_kernel, heads=heads, bk=bk, scale=d ** -0.5),
        grid=(b,),
        in_specs=[new, new, new, old, old],
        out_specs=new,
        out_shape=jax.ShapeDtypeStruct((b, t, hd_all), BF16),
        compiler_params=_cparams(("parallel",)),
        name="sb_decode",
    )(q, k_new, v_new, k_cache.reshape(b, p * heads, d), v_cache.reshape(b, p * heads, d))


def _gla_kernel(*refs, chunk, n_chunks, heads, has_s0, dk_scale):
    if has_s0:
        q_ref, k_ref, v_ref, r_ref, ga_ref, wa_ref, ba_ref, ng_ref, s0_ref, o_ref, s_out_ref, s_sc = refs
    else:
        q_ref, k_ref, v_ref, r_ref, ga_ref, wa_ref, ba_ref, ng_ref, o_ref, s_out_ref, s_sc = refs
    tg = pl.program_id(1)

    @pl.when(tg == 0)
    def _():
        s_sc[...] = s0_ref[...] if has_s0 else jnp.zeros_like(s_sc)

    rows = lax.broadcasted_iota(jnp.int32, (chunk, chunk), 0)
    cols = lax.broadcasted_iota(jnp.int32, (chunk, chunk), 1)
    causal = cols <= rows
    tri = jnp.where(causal, 1.0, 0.0).astype(BF16)
    ones_cols = jnp.ones((chunk, V7X_LANES), BF16)
    nb, _, dk, dv = s_sc.shape
    mid = chunk // 2 - 1
    nt_dims = (((1,), (1,)), ((), ()))
    tn_dims = (((0,), (0,)), ((), ()))
    chains = [(bb, h) for bb in range(nb) for h in range(heads)]
    cs = range(len(chains))
    kcols = lambda h: slice(h * dk, (h + 1) * dk)
    vcols = lambda h: slice(h * dv, (h + 1) * dv)

    def body(c, carry):
        sl = pl.ds(pl.multiple_of(c * chunk, chunk), chunk)
        gs = [_log_sigmoid(jnp.dot(ga_ref[bb, sl, :], wa_ref[h], preferred_element_type=F32) + ba_ref[h])
              / GLA_GATE_NORM for bb, h in chains]
        g_split = [_split_hi_lo(g) for g in gs]
        bs = [jnp.dot(tri, hi, preferred_element_type=F32) + jnp.dot(tri, lo, preferred_element_type=F32)
              for hi, lo in g_split]
        b_last_cols = [lax.dot_general(hi, ones_cols, tn_dims, preferred_element_type=F32)
                       + lax.dot_general(lo, ones_cols, tn_dims, preferred_element_type=F32)
                       for hi, lo in g_split]
        qs = [q_ref[bb, sl, kcols(h)].astype(F32) * dk_scale for bb, h in chains]
        ks = [k_ref[bb, sl, kcols(h)].astype(F32) for bb, h in chains]
        vs = [v_ref[bb, sl, vcols(h)] for bb, h in chains]
        s_prev = [s_sc[bb, h] for bb, h in chains]
        q_in = [(qs[i] * jnp.exp(bs[i])).astype(BF16) for i in cs]
        q_m = [(qs[i] * jnp.exp(bs[i] - bs[i][mid:mid + 1, :])).astype(BF16) for i in cs]
        k_m = [(ks[i] * jnp.exp(bs[i][mid:mid + 1, :] - bs[i])).astype(BF16) for i in cs]
        k_st = [(ks[i] * jnp.exp(bs[i][chunk - 1:chunk, :] - bs[i])).astype(BF16) for i in cs]
        o_inter = [jnp.dot(q_in[i], s_prev[i].astype(BF16), preferred_element_type=F32) for i in cs]
        att = [jnp.where(causal, lax.dot_general(q_m[i], k_m[i], nt_dims, preferred_element_type=F32), 0.0)
               for i in cs]
        outs = [o_inter[i] + jnp.dot(att[i].astype(BF16), vs[i], preferred_element_type=F32) for i in cs]
        for i, (bb, h) in enumerate(chains):
            decay = jnp.concatenate([jnp.exp(b_last_cols[i])] * (dv // V7X_LANES), axis=1)
            s_sc[bb, h] = decay * s_prev[i] + lax.dot_general(k_st[i], vs[i], tn_dims,
                                                             preferred_element_type=F32)
        for i, (bb, h) in enumerate(chains):
            r = r_ref[bb, sl, vcols(h)].astype(F32)
            o_ref[bb, sl, vcols(h)] = (_rms(outs[i], ng_ref[...]) * (r * jax.nn.sigmoid(r))).astype(o_ref.dtype)
        return carry

    lax.fori_loop(0, n_chunks, body, 0)

    @pl.when(tg == pl.num_programs(1) - 1)
    def _():
        s_out_ref[...] = s_sc[...]


def _gla(q, k, v, r, ga, wa, ba, ng, s0, *, chunk, tg=GLA_TOKEN_GROUP, nb=GLA_BATCH_ROWS):
    b, t, _ = q.shape
    heads, _, dk = wa.shape
    dv = v.shape[-1] // heads
    chunk = min(chunk, t)
    tg = _pick_tile(t, tg)
    nb = _pick_tile(b, nb)
    assert tg % chunk == 0
    has_s0 = s0 is not None
    tok = lambda bi, ti: (bi, ti, 0)
    in_specs = [
        pl.BlockSpec((nb, tg, heads * dk), tok), pl.BlockSpec((nb, tg, heads * dk), tok),
        pl.BlockSpec((nb, tg, heads * dv), tok), pl.BlockSpec((nb, tg, heads * dv), tok),
        pl.BlockSpec((nb, tg, V7X_LANES), tok),
        _resident(wa.shape), _resident(ba.shape), _resident(ng.shape),
    ]
    args = [q, k, v, r, ga, wa, ba, ng]
    state_spec = pl.BlockSpec((nb, heads, dk, dv), lambda bi, ti: (bi, 0, 0, 0))
    if has_s0:
        in_specs.append(state_spec)
        args.append(s0)
    return pl.pallas_call(
        functools.partial(_gla_kernel, chunk=chunk, n_chunks=tg // chunk, heads=heads, has_s0=has_s0,
                          dk_scale=dk ** -0.5),
        grid=(b // nb, t // tg),
        in_specs=in_specs,
        out_specs=[pl.BlockSpec((nb, tg, heads * dv), tok), state_spec],
        out_shape=[jax.ShapeDtypeStruct((b, t, heads * dv), BF16),
                   jax.ShapeDtypeStruct((b, heads, dk, dv), F32)],
        scratch_shapes=[pltpu.VMEM((nb, heads, dk, dv), F32)],
        compiler_params=_cparams(("parallel", "arbitrary")),
        name="gla",
    )(*args)


def _mem_kernel(q_ref, k_ref, v_ref, o_ref, *, heads, scale):
    hd = q_ref.shape[-1] // heads
    per_head = len(k_ref.shape) == 4
    for h in range(heads):
        cs = slice(h * hd, (h + 1) * hd)
        q = q_ref[0, :, cs]
        k = (k_ref[0, :, h, :] if per_head else k_ref[0, :, cs]).astype(BF16)
        v = (v_ref[0, :, h, :] if per_head else v_ref[0, :, cs]).astype(BF16)
        s = lax.dot_general(q, k, (((1,), (1,)), ((), ())), preferred_element_type=F32) * scale
        e = jnp.exp(s - jnp.max(s, axis=-1, keepdims=True))
        p = e / jnp.sum(e, axis=-1, keepdims=True)
        o_ref[0, :, cs] = jnp.dot(p.astype(BF16), v, preferred_element_type=F32).astype(o_ref.dtype)


def _mem_attn(q, mk, mv, *, heads, tq=MEM_QUERY_TILE):
    b, t, w = q.shape
    tq = _pick_tile(t, tq)
    mem_block = (1,) + mk.shape[1:]
    mem = pl.BlockSpec(mem_block, lambda bi, ti: (bi,) + (0,) * (len(mem_block) - 1))
    return pl.pallas_call(
        functools.partial(_mem_kernel, heads=heads, scale=(w // heads) ** -0.5),
        grid=(b, t // tq),
        in_specs=[pl.BlockSpec((1, tq, w), lambda bi, ti: (bi, ti, 0)), mem, mem],
        out_specs=pl.BlockSpec((1, tq, w), lambda bi, ti: (bi, ti, 0)),
        out_shape=jax.ShapeDtypeStruct((b, t, w), BF16),
        compiler_params=_cparams(("parallel", "arbitrary")),
        name="mem_attn",
    )(q, mk, mv)


def _merge_kernel(h_ref, osb_ref, ogla_ref, omem_ref, gates_ref, wsb_ref, wgla_ref, wmem_ref, wout_ref,
                  post_g_ref, h2_ref):
    d = h_ref.shape[1]
    branches = ((osb_ref, wsb_ref), (ogla_ref, wgla_ref), (omem_ref, wmem_ref))
    merged = None
    for k, (o_ref, w_ref) in enumerate(branches):
        term = gates_ref[:, k * d:(k + 1) * d].astype(F32) * jnp.dot(o_ref[...], w_ref[...],
                                                                    preferred_element_type=F32)
        merged = term if merged is None else merged + term
    m = jnp.dot(merged.astype(BF16), wout_ref[...], preferred_element_type=F32)
    h2_ref[...] = h_ref[...] + _rms(m, post_g_ref[...])


def _merge(h, o_sb, o_gla, o_mem, gates, w_sb, w_gla, w_mem, w_out, post_g, *, tm=MERGE_TOKEN_TILE):
    n, d = h.shape
    tm = _pick_tile(n, tm)
    row = lambda a: pl.BlockSpec((tm, a.shape[1]), lambda i: (i, 0))
    return pl.pallas_call(
        _merge_kernel,
        grid=(n // tm,),
        in_specs=[row(h), row(o_sb), row(o_gla), row(o_mem), row(gates),
                  _resident(w_sb.shape), _resident(w_gla.shape), _resident(w_mem.shape),
                  _resident(w_out.shape), _resident(post_g.shape)],
        out_specs=row(h),
        out_shape=jax.ShapeDtypeStruct((n, d), F32),
        compiler_params=_cparams(("parallel",)),
        name="merge",
    )(h, o_sb, o_gla, o_mem, gates, w_sb, w_gla, w_mem, w_out, post_g)


def _prep_weights(p, d):
    sb_w = d // 2
    gla_kw = d // 2
    gla_vw = d
    mem_w = d // 2
    w_in = p["w_in"]
    c = 0
    pieces = {}
    for name, width in (("sq", sb_w), ("sk", sb_w), ("sv", sb_w), ("gq", gla_kw), ("gk", gla_kw),
                        ("gv", gla_vw), ("gr", gla_vw), ("ga", GLA_LOW_RANK), ("mq", mem_w)):
        pieces[name] = w_in[:, c:c + width].astype(BF16)
        c += width
    assert c == w_in.shape[1]
    pad = V7X_LANES - GLA_LOW_RANK
    dk = gla_kw // GLA_HEADS
    w = dict(pieces)
    w["ga"] = jnp.pad(pieces["ga"], ((0, 0), (0, pad)))
    wa = jnp.pad(p["gla_w_a2"].astype(BF16), ((0, pad), (0, 0)))
    w["wa"] = wa.reshape(V7X_LANES, GLA_HEADS, dk).transpose(1, 0, 2)
    w["ba"] = p["gla_b_a2"].reshape(GLA_HEADS, 1, dk)
    for name in ("ffn1_w_gu", "ffn1_w_d", "ffn2_w_gu", "ffn2_w_d", "w_sb_br", "w_gla_br", "w_mem_br",
                 "w_gate", "w_out"):
        w[name] = p[name].astype(BF16)
    for name in ("ffn1_pre_g", "ffn1_post_g", "mix_pre_g", "mix_post_g", "ffn2_pre_g", "ffn2_post_g",
                 "gla_norm_g", "b_gate"):
        w[name] = p[name].reshape(1, -1)
    return w


def _layer(x, w, mem_k, mem_v, sb_past_k, sb_past_v, gla_s0):
    b, t, d = x.shape
    n = b * t
    sb_heads = (d // 2) // SB_HEAD_DIM
    h1, u = _ffn(x.reshape(n, d), w["ffn1_pre_g"], w["ffn1_w_gu"], w["ffn1_w_d"], w["ffn1_post_g"],
                 next_g=w["mix_pre_g"])
    sq, sk, sv, gq, gk, mq = _proj(u, [w[k] for k in ("sq", "sk", "sv", "gq", "gk", "mq")],
                                   [BF16, F32, F32, BF16, BF16, BF16])
    gv, gr, ga = _proj(u, [w["gv"], w["gr"], w["ga"]], [BF16, BF16, BF16])
    (gates,) = _proj(u, [w["w_gate"]], [BF16], biases=[w["b_gate"]], act="sigmoid")

    r3 = lambda a: a.reshape(b, t, -1)
    if sb_past_k is None:
        o_sb = _sb_attn(r3(sq), r3(sk), r3(sv), heads=sb_heads)
    else:
        o_sb = _sb_decode(r3(sq), r3(sk), r3(sv), sb_past_k, sb_past_v)
    o_gla, s_new = _gla(r3(gq), r3(gk), r3(gv), r3(gr), r3(ga), w["wa"], w["ba"], w["gla_norm_g"], gla_s0,
                        chunk=GLA_CHUNK)
    o_mem = _mem_attn(r3(mq), mem_k, mem_v, heads=MEM_HEADS)

    h2 = _merge(h1, o_sb.reshape(n, -1), o_gla.reshape(n, -1), o_mem.reshape(n, -1), gates,
                w["w_sb_br"], w["w_gla_br"], w["w_mem_br"], w["w_out"], w["mix_post_g"])
    y = _ffn(h2, w["ffn2_pre_g"], w["ffn2_w_gu"], w["ffn2_w_d"], w["ffn2_post_g"])
    return y.reshape(b, t, d), sk, sv, s_new


def kernel(x_prompt, x_sample, mem_prompt, cache_sb_k, cache_sb_v, state_gla, cache_mem_k, cache_mem_v, ffn1_pre_g, ffn1_w_gu, ffn1_w_d, ffn1_post_g, mix_pre_g, w_in, gla_w_a2, gla_b_a2, gla_norm_g, mem_norm_g, w_mem_kv, w_sb_br, w_gla_br, w_mem_br, w_gate, b_gate, w_out, mix_post_g, ffn2_pre_g, ffn2_w_gu, ffn2_w_d, ffn2_post_g):
    params = dict(ffn1_pre_g=ffn1_pre_g, ffn1_w_gu=ffn1_w_gu, ffn1_w_d=ffn1_w_d, ffn1_post_g=ffn1_post_g,
                  mix_pre_g=mix_pre_g, w_in=w_in, gla_w_a2=gla_w_a2, gla_b_a2=gla_b_a2, gla_norm_g=gla_norm_g,
                  w_sb_br=w_sb_br, w_gla_br=w_gla_br, w_mem_br=w_mem_br, w_gate=w_gate, b_gate=b_gate,
                  w_out=w_out, mix_post_g=mix_post_g, ffn2_pre_g=ffn2_pre_g, ffn2_w_gu=ffn2_w_gu,
                  ffn2_w_d=ffn2_w_d, ffn2_post_g=ffn2_post_g)
    depth = w_in.shape[0]
    bp, tp, d = x_prompt.shape
    bs, ts, _ = x_sample.shape
    m = mem_prompt.shape[1]
    mem_w = d // 2
    h_p, h_s = x_prompt, x_sample
    outs = [[] for _ in range(8)]
    for l in range(depth):
        w = _prep_weights({k: v[l] for k, v in params.items()}, d)
        w_mkv = w_mem_kv[l].astype(BF16)
        mk, mv = _proj(mem_prompt.reshape(bp * m, d), [w_mkv[:, :mem_w], w_mkv[:, mem_w:]], [F32, F32],
                       norm_g=mem_norm_g[l].reshape(1, d))
        mk = mk.reshape(bp, m, mem_w)
        mv = mv.reshape(bp, m, mem_w)
        h_p, k_p, v_p, s_p = _layer(h_p, w, mk, mv, None, None, None)
        h_s, k_s, v_s, s_s = _layer(h_s, w, cache_mem_k[l], cache_mem_v[l], cache_sb_k[l], cache_sb_v[l],
                                    state_gla[l])
        sb_heads = mem_w // SB_HEAD_DIM
        for lst, val in zip(outs, (k_p.reshape(bp, tp, sb_heads, SB_HEAD_DIM), v_p.reshape(bp, tp, sb_heads, SB_HEAD_DIM),
                                   s_p, mk.reshape(bp, m, MEM_HEADS, -1), mv.reshape(bp, m, MEM_HEADS, -1),
                                   k_s.reshape(bs, ts, sb_heads, SB_HEAD_DIM), v_s.reshape(bs, ts, sb_heads, SB_HEAD_DIM),
                                   s_s)):
            lst.append(val)
    return (h_p, h_s) + tuple(jnp.stack(o) for o in outs)
```
